```python
import math
import jax, jax.numpy as jnp
from jax import lax
import numpy as np

D_MODEL = 2048
BATCH = 2
SEQ = 16384
DEPTH = 2
DEC_BATCH = 16
DEC_SEQ = 64
PAST_LEN = 2048

CHUNK = 64
Q_BLOCK = 128
A_HEADS = 8
A_DH = 64
A_QW = A_HEADS * 2 * A_DH
B_HEADS = 8
B_Q_RANK = 512
B_KV_RANK = 512
B_NOPE = 128
B_ROPE = 64
B_VDIM = 128
ROPE_THETA = 10000.0
C_WIDTH = 1024
C_GROUPS = 4
C_BLOCK = 128
D_WIDTH = 1024
D_WINDOWS = (2, 4, 8, 16)
D_HIST = 15
N_BRANCH = 4
BR_WIDTH = 1024
FF_DENSE = 5632
N_EXPERTS = 8
TOP_K = 2
FF_EXPERT = 7168
N_DENSE = (DEPTH + 1) // 2
N_MOE = DEPTH // 2
DN_ALPHA = (2.0 * DEPTH) ** 0.25
DN_BETA = (8.0 * DEPTH) ** -0.25
LN_EPS = 1e-5
RMS_EPS = 1e-6
IN_SIZES = (A_QW, A_QW, A_QW, B_Q_RANK, B_KV_RANK, B_ROPE, C_WIDTH, C_WIDTH, D_WIDTH, D_MODEL, D_MODEL, D_MODEL, D_MODEL)
IN_WIDTH = 3 * A_QW + B_Q_RANK + B_KV_RANK + B_ROPE + 2 * C_WIDTH + D_WIDTH + N_BRANCH * D_MODEL

kernel_name = 'hybrid_chunk_streaming_encoder_step'


def _layernorm(x, g, b):
    xf = x.astype(jnp.float32)
    mu = jnp.mean(xf, -1, keepdims=True)
    var = jnp.mean(jnp.square(xf - mu), -1, keepdims=True)
    return ((xf - mu) * lax.rsqrt(var + LN_EPS)).astype(x.dtype) * g + b


def _rmsnorm(x, g):
    xf = x.astype(jnp.float32)
    return (xf * lax.rsqrt(jnp.mean(xf * xf, -1, keepdims=True) + RMS_EPS)).astype(x.dtype) * g


def _rope(x, pos):
    half = x.shape[-1] // 2
    inv = ROPE_THETA ** (-jnp.arange(half, dtype=jnp.float32) / half)
    ang = pos.astype(jnp.float32)[:, None] * inv[None, :]
    shape = (1, pos.shape[0]) + (1,) * (x.ndim - 3) + (half,)
    cos = jnp.cos(ang).reshape(shape).astype(x.dtype)
    sin = jnp.sin(ang).reshape(shape).astype(x.dtype)
    x1, x2 = x[..., :half], x[..., half:]
    return jnp.concatenate([x1 * cos - x2 * sin, x1 * sin + x2 * cos], -1)


def _chunk_mask(q_pos, k_pos):
    return (k_pos[None, :] // CHUNK) <= (q_pos[:, None] // CHUNK)


def _proj(x, w, idx):
    start = sum(IN_SIZES[:idx])
    return x @ w[:, start:start + IN_SIZES[idx]]


def _sweep(core, qs, q_pos):
    L = q_pos.shape[0]
    if L % Q_BLOCK:
        return core(qs, q_pos)
    nb = L // Q_BLOCK
    qb = tuple(jnp.swapaxes(q.reshape((q.shape[0], nb, Q_BLOCK) + q.shape[2:]), 0, 1) for q in qs)
    out = lax.map(lambda a: core(a[0], a[1]), (qb, q_pos.reshape(nb, Q_BLOCK)))
    out = jnp.swapaxes(out, 0, 1)
    return out.reshape((out.shape[0], L) + out.shape[3:])


def _diff_attn_core(q, q_pos, k, v, k_pos, lam, slopes):
    s = jnp.einsum('bqhcd,bkhcd->bhcqk', q, k).astype(jnp.float32) * (A_DH ** -0.5)
    dist = jnp.abs(q_pos[:, None] - k_pos[None, :]).astype(jnp.float32)
    s = s - (slopes[:, None, None] * dist[None])[None, :, None]
    s = jnp.where(_chunk_mask(q_pos, k_pos), s, -jnp.inf)
    pr = jax.nn.softmax(s, axis=-1)
    a = pr[:, :, 0] - lam * pr[:, :, 1]
    return jnp.einsum('bhqk,bkhe->bqhe', a.astype(v.dtype), v)


def _mla_core(qn, qr, q_pos, kn, kr, v, k_pos):
    s = jnp.einsum('bqhd,bkhd->bhqk', qn, kn) + jnp.einsum('bqhr,bkr->bhqk', qr, kr)
    s = s.astype(jnp.float32) * ((B_NOPE + B_ROPE) ** -0.5)
    s = jnp.where(_chunk_mask(q_pos, k_pos), s, -jnp.inf)
    pr = jax.nn.softmax(s, axis=-1).astype(v.dtype)
    return jnp.einsum('bhqk,bkhe->bqhe', pr, v)


def _spatial_gate(v, w_s, b_s):
    Bn, L, W = v.shape
    pad = (-L) % C_BLOCK
    nc = (L + pad) // C_BLOCK
    vp = jnp.pad(v, ((0, 0), (0, pad), (0, 0))).reshape(Bn, nc, C_BLOCK, C_GROUPS, W // C_GROUPS)
    out = jnp.einsum('gts,bnsgc->bntgc', jnp.tril(w_s), vp) + jnp.transpose(b_s)[None, None, :, :, None]
    return out.reshape(Bn, nc * C_BLOCK, W)[:, :L]


def _swiglu(x, wg, wu, wd):
    return (jax.nn.silu(x @ wg) * (x @ wu)) @ wd


def _moe(x, router, wg, wu, wd):
    logits = (x @ router).astype(jnp.float32)
    top_v, top_i = lax.top_k(logits, TOP_K)
    probs = jax.nn.softmax(top_v, axis=-1)
    gate = jnp.sum(jax.nn.one_hot(top_i, N_EXPERTS, dtype=jnp.float32) * probs[..., None], -2).astype(x.dtype)
    y = jnp.zeros_like(x)
    for e in range(N_EXPERTS):
        y = y + gate[..., e:e + 1] * _swiglu(x, wg[e], wu[e], wd[e])
    return y


def _mixer(x, l, cache, p):
    Bn, L, _ = x.shape
    past = 0 if cache is None else cache[0].shape[1]
    pos = past + jnp.arange(L, dtype=jnp.int32)
    k_pos = jnp.arange(past + L, dtype=jnp.int32)
    w_in = p['w_in'][l]

    q_a = _proj(x, w_in, 0).reshape(Bn, L, A_HEADS, 2, A_DH)
    k_a = _proj(x, w_in, 1).reshape(Bn, L, A_HEADS, 2 * A_DH)
    v_a = _proj(x, w_in, 2).reshape(Bn, L, A_HEADS, 2 * A_DH)
    k_all = k_a if cache is None else jnp.concatenate([cache[0], k_a], 1)
    v_all = v_a if cache is None else jnp.concatenate([cache[1], v_a], 1)
    k_all5 = k_all.reshape(Bn, past + L, A_HEADS, 2, A_DH)
    lam_init = 0.8 - 0.6 * math.exp(-0.3 * l)
    lp = p['a_lambda'][l].astype(jnp.float32)
    lam = jnp.exp(jnp.sum(lp[0] * lp[1])) - jnp.exp(jnp.sum(lp[2] * lp[3])) + lam_init
    slopes = jnp.exp2(-8.0 * jnp.arange(1, A_HEADS + 1, dtype=jnp.float32) / A_HEADS)
    o_a = _sweep(lambda qs, qp: _diff_attn_core(qs[0], qp, k_all5, v_all, k_pos, lam, slopes), (q_a,), pos)
    o_a = (_rmsnorm(o_a, p['a_subln'][l]) * (1.0 - lam_init)).reshape(Bn, L, BR_WIDTH)

    cq = _rmsnorm(_proj(x, w_in, 3), p['b_q_norm'][l])
    qb = (cq @ p['b_w_uq'][l]).reshape(Bn, L, B_HEADS, B_NOPE + B_ROPE)
    qn = qb[..., :B_NOPE]
    qr = _rope(qb[..., B_NOPE:], pos)
    lat = _rmsnorm(_proj(x, w_in, 4), p['b_kv_norm'][l])
    kr = _rope(_proj(x, w_in, 5), pos)
    lat_all = lat if cache is None else jnp.concatenate([cache[2], lat], 1)
    kr_all = kr if cache is None else jnp.concatenate([cache[3], kr], 1)
    kv = (lat_all @ p['b_w_ukv'][l]).reshape(Bn, past + L, B_HEADS, B_NOPE + B_VDIM)
    kn, vb = kv[..., :B_NOPE], kv[..., B_NOPE:]
    o_b = _sweep(lambda qs, qp: _mla_core(qs[0], qs[1], qp, kn, kr_all, vb, k_pos), (qn, qr), pos)
    o_b = o_b.reshape(Bn, L, BR_WIDTH)

    u_c = jax.nn.gelu(_proj(x, w_in, 6))
    v_c = _layernorm(jax.nn.gelu(_proj(x, w_in, 7)), p['c_ln_g'][l], p['c_ln_b'][l])
    o_c = u_c * _spatial_gate(v_c, p['c_w_s'][l], p['c_b_s'][l])

    d_in = _proj(x, w_in, 8)
    hist = jnp.zeros((Bn, D_HIST, D_WIDTH), x.dtype) if cache is None else cache[4]
    xp = jnp.concatenate([hist, d_in], 1)
    new_pool = xp[:, -D_HIST:]
    cs = jnp.cumsum(jnp.pad(xp.astype(jnp.float32), ((0, 0), (1, 0), (0, 0))), axis=1)
    end = cs[:, D_HIST + 1:]
    gw = D_WIDTH // len(D_WINDOWS)
    parts = []
    for gi, w in enumerate(D_WINDOWS):
        lo, hi = gi * gw, (gi + 1) * gw
        start = cs[:, D_HIST + 1 - w:D_HIST + 1 - w + L, lo:hi]
        cnt = jnp.minimum(pos + 1, w).astype(jnp.float32)[None, :, None]
        pooled = ((end[..., lo:hi] - start) / cnt - d_in[..., lo:hi].astype(jnp.float32)).astype(x.dtype)
        parts.append(pooled @ p['d_w_grp'][l, gi])
    o_d = jnp.concatenate(parts, -1) * p['d_scale'][l]

    branches = (o_a, o_b, o_c, o_d)
    y = jnp.zeros_like(x)
    for n in range(N_BRANCH):
        gate = jax.nn.sigmoid(_proj(x, w_in, 9 + n))
        y = y + gate * (branches[n] @ p['w_branch'][l, n])
    return y @ p['w_out'][l], (k_a, v_a, lat, kr, v_c, new_pool)


def _layer(x, l, cache, p):
    m, st = _mixer(x, l, cache, p)
    x = _layernorm(DN_ALPHA * x + m, p['ln1_g'][l], p['ln1_b'][l])
    i = l // 2
    if l % 2 == 0:
        f = _swiglu(x, p['f_w_gate'][i], p['f_w_up'][i], p['f_w_down'][i])
    else:
        f = _moe(x, p['m_router'][i], p['m_w_gate'][i], p['m_w_up'][i], p['m_w_down'][i])
    x = _layernorm(DN_ALPHA * x + f, p['ln2_g'][l], p['ln2_b'][l])
    return x, st


def setup_inputs(seed: int = 0) -> dict:
    key = jax.random.key(seed)
    k = jax.random.split(key, 33)
    f32 = jnp.float32

    def nrm(kk, shape, scale=1.0):
        return jax.random.normal(kk, shape, f32) * scale

    return {
        'x_prompt': nrm(k[0], (BATCH, SEQ, D_MODEL)),
        'x_sample': nrm(k[1], (DEC_BATCH, DEC_SEQ, D_MODEL)),
        'cache_a_k': nrm(k[2], (DEPTH, DEC_BATCH, PAST_LEN, A_HEADS, 2 * A_DH)),
        'cache_a_v': nrm(k[3], (DEPTH, DEC_BATCH, PAST_LEN, A_HEADS, 2 * A_DH)),
        'cache_b_latent': nrm(k[4], (DEPTH, DEC_BATCH, PAST_LEN, B_KV_RANK)),
        'cache_b_krope': nrm(k[5], (DEPTH, DEC_BATCH, PAST_LEN, B_ROPE)),
        'state_d_pool': nrm(k[6], (DEPTH, DEC_BATCH, D_HIST, D_WIDTH)),
        'w_in': nrm(k[7], (DEPTH, D_MODEL, IN_WIDTH), D_MODEL ** -0.5),
        'a_lambda': nrm(k[8], (DEPTH, 4, A_DH), 0.1),
        'a_subln': 1.0 + nrm(k[9], (DEPTH, 2 * A_DH), 0.02),
        'b_q_norm': 1.0 + nrm(k[10], (DEPTH, B_Q_RANK), 0.02),
        'b_w_uq': nrm(k[11], (DEPTH, B_Q_RANK, B_HEADS * (B_NOPE + B_ROPE)), B_Q_RANK ** -0.5),
        'b_kv_norm': 1.0 + nrm(k[12], (DEPTH, B_KV_RANK), 0.02),
        'b_w_ukv': nrm(k[13], (DEPTH, B_KV_RANK, B_HEADS * (B_NOPE + B_VDIM)), B_KV_RANK ** -0.5),
        'c_ln_g': 1.0 + nrm(k[14], (DEPTH, C_WIDTH), 0.02),
        'c_ln_b': nrm(k[15], (DEPTH, C_WIDTH), 0.02),
        'c_w_s': nrm(k[16], (DEPTH, C_GROUPS, C_BLOCK, C_BLOCK), C_BLOCK ** -0.5),
        'c_b_s': 1.0 + nrm(k[17], (DEPTH, C_GROUPS, C_BLOCK), 0.02),
        'd_w_grp': nrm(k[18], (DEPTH, len(D_WINDOWS), D_WIDTH // len(D_WINDOWS), D_WIDTH // len(D_WINDOWS)), (D_WIDTH // len(D_WINDOWS)) ** -0.5),
        'd_scale': 1.0 + nrm(k[19], (DEPTH, D_WIDTH), 0.02),
        'w_branch': nrm(k[20], (DEPTH, N_BRANCH, BR_WIDTH, D_MODEL), BR_WIDTH ** -0.5),
        'w_out': nrm(k[21], (DEPTH, D_MODEL, D_MODEL), DN_BETA * D_MODEL ** -0.5),
        'ln1_g': 1.0 + nrm(k[22], (DEPTH, D_MODEL), 0.02),
        'ln1_b': nrm(k[23], (DEPTH, D_MODEL), 0.02),
        'ln2_g': 1.0 + nrm(k[24], (DEPTH, D_MODEL), 0.02),
        'ln2_b': nrm(k[25], (DEPTH, D_MODEL), 0.02),
        'f_w_gate': nrm(k[26], (N_DENSE, D_MODEL, FF_DENSE), D_MODEL ** -0.5),
        'f_w_up': nrm(k[27], (N_DENSE, D_MODEL, FF_DENSE), D_MODEL ** -0.5),
        'f_w_down': nrm(k[28], (N_DENSE, FF_DENSE, D_MODEL), DN_BETA * FF_DENSE ** -0.5),
        'm_router': nrm(k[29], (N_MOE, D_MODEL, N_EXPERTS), D_MODEL ** -0.5),
        'm_w_gate': nrm(k[30], (N_MOE, N_EXPERTS, D_MODEL, FF_EXPERT), D_MODEL ** -0.5),
        'm_w_up': nrm(k[31], (N_MOE, N_EXPERTS, D_MODEL, FF_EXPERT), D_MODEL ** -0.5),
        'm_w_down': nrm(k[32], (N_MOE, N_EXPERTS, FF_EXPERT, D_MODEL), DN_BETA * FF_EXPERT ** -0.5),
    }


def reference(x_prompt, x_sample, cache_a_k, cache_a_v, cache_b_latent, cache_b_krope, state_d_pool,
              w_in, a_lambda, a_subln, b_q_norm, b_w_uq, b_kv_norm, b_w_ukv, c_ln_g, c_ln_b, c_w_s, c_b_s,
              d_w_grp, d_scale, w_branch, w_out, ln1_g, ln1_b, ln2_g, ln2_b,
              f_w_gate, f_w_up, f_w_down, m_router, m_w_gate, m_w_up, m_w_down):
    p = {'w_in': w_in, 'a_lambda': a_lambda, 'a_subln': a_subln, 'b_q_norm': b_q_norm, 'b_w_uq': b_w_uq,
         'b_kv_norm': b_kv_norm, 'b_w_ukv': b_w_ukv, 'c_ln_g': c_ln_g, 'c_ln_b': c_ln_b, 'c_w_s': c_w_s,
         'c_b_s': c_b_s, 'd_w_grp': d_w_grp, 'd_scale': d_scale, 'w_branch': w_branch, 'w_out': w_out,
         'ln1_g': ln1_g, 'ln1_b': ln1_b, 'ln2_g': ln2_g, 'ln2_b': ln2_b, 'f_w_gate': f_w_gate,
         'f_w_up': f_w_up, 'f_w_down': f_w_down, 'm_router': m_router, 'm_w_gate': m_w_gate,
         'm_w_up': m_w_up, 'm_w_down': m_w_down}
    yp, ys = x_prompt, x_sample
    sp, ss = [], []
    for l in range(DEPTH):
        yp, st_p = _layer(yp, l, None, p)
        ys, st_s = _layer(ys, l, (cache_a_k[l], cache_a_v[l], cache_b_latent[l], cache_b_krope[l], state_d_pool[l]), p)
        sp.append(st_p)
        ss.append(st_s)
    new_a_k_prompt = jnp.stack([s[0] for s in sp])
    new_a_v_prompt = jnp.stack([s[1] for s in sp])
    new_b_latent_prompt = jnp.stack([s[2] for s in sp])
    new_b_krope_prompt = jnp.stack([s[3] for s in sp])
    new_d_pool_prompt = jnp.stack([s[5] for s in sp])
    new_a_k_sample = jnp.stack([s[0] for s in ss])
    new_a_v_sample = jnp.stack([s[1] for s in ss])
    new_b_latent_sample = jnp.stack([s[2] for s in ss])
    new_b_krope_sample = jnp.stack([s[3] for s in ss])
    new_c_v_sample = jnp.stack([s[4] for s in ss])
    new_d_pool_sample = jnp.stack([s[5] for s in ss])
    return (yp, ys, new_a_k_prompt, new_a_v_prompt, new_b_latent_prompt, new_b_krope_prompt, new_d_pool_prompt,
            new_a_k_sample, new_a_v_sample, new_b_latent_sample, new_b_krope_sample, new_c_v_sample, new_d_pool_sample)
```

```python
import functools
import math

import jax
import jax.numpy as jnp
from jax import lax
from jax.experimental import pallas as pl
from jax.experimental.pallas import tpu as pltpu

CHUNK = 64
A_HEADS = 8
A_DH = 64
B_HEADS = 8
B_NOPE = 128
B_ROPE = 64
B_VDIM = 128
C_GROUPS = 4
C_BLOCK = 128
D_WINDOWS = (2, 4, 8, 16)
D_HIST = 15
D_HALO = 16
N_BRANCH = 4
ROPE_THETA = 10000.0
LN_EPS = 1e-5
RMS_EPS = 1e-6
LOG2E = 1.4426950408889634
NEG = -1e30
LANE = 128
VMEM_LIMIT = 56 * 1024 * 1024

F32 = jnp.float32
BF16 = jnp.bfloat16


def _pick(n, cands):
    for c in cands:
        if n % c == 0:
            return c
    raise ValueError(f"no tile for {n} in {cands}")


def _cparams(sem):
    return pltpu.CompilerParams(dimension_semantics=sem, vmem_limit_bytes=VMEM_LIMIT)


def _ln_rows(z, g, b):
    mu = jnp.mean(z, axis=-1, keepdims=True)
    zc = z - mu
    var = jnp.mean(zc * zc, axis=-1, keepdims=True)
    return zc * lax.rsqrt(var + LN_EPS) * g + b


def _rms_rows(z, g):
    return z * lax.rsqrt(jnp.mean(z * z, axis=-1, keepdims=True) + RMS_EPS) * g


def _fused_mm_kernel(*refs, fn, nw, ntab, nvec, precision):
    x = refs[0][...]
    w_refs = refs[1:1 + nw]
    tab_refs = refs[1 + nw:1 + nw + ntab]
    vec_refs = refs[1 + nw + ntab:1 + nw + ntab + nvec]
    out_refs = refs[1 + nw + ntab + nvec:]
    accs = [jnp.dot(x, w[...], preferred_element_type=F32, precision=precision) for w in w_refs]
    res = fn(accs, [t[...] for t in tab_refs], [v[...] for v in vec_refs])
    for o, r in zip(out_refs, res):
        o[...] = r.astype(o.dtype)


def _fused_mm(fn, x, ws, out_dtypes, *, tn, tm=None, tabs=(), tabs_follow_cols=False, vecs=(),
              precision=None):
    m, kdim = x.shape
    n = ws[0].shape[1]
    tm = tm or _pick(m, (1024, 512, 256, 128))
    grid = (m // tm, n // tn)
    in_specs = [pl.BlockSpec((tm, kdim), lambda i, j: (i, 0))]
    in_specs += [pl.BlockSpec((kdim, tn), lambda i, j: (0, j)) for _ in ws]
    for t in tabs:
        if tabs_follow_cols:
            in_specs.append(pl.BlockSpec((tm, tn), lambda i, j: (i, j)))
        else:
            in_specs.append(pl.BlockSpec((tm, t.shape[1]), lambda i, j: (i, 0)))
    in_specs += [pl.BlockSpec((1, tn), lambda i, j: (0, j)) for _ in vecs]
    out_specs = [pl.BlockSpec((tm, tn), lambda i, j: (i, j)) for _ in out_dtypes]
    out_shape = [jax.ShapeDtypeStruct((m, n), d) for d in out_dtypes]
    kern = functools.partial(_fused_mm_kernel, fn=fn, nw=len(ws), ntab=len(tabs), nvec=len(vecs),
                             precision=precision)
    return pl.pallas_call(
        kern, grid=grid, in_specs=in_specs, out_specs=out_specs, out_shape=out_shape,
        compiler_params=_cparams(("parallel", "arbitrary")),
    )(x, *ws, *tabs, *vecs)


def _online_update(s, v, m_sc, l_sc, acc_sc):
    m_prev = m_sc[...]
    m_new = jnp.maximum(m_prev, jnp.max(s, axis=1, keepdims=True))
    alpha = jnp.exp2(m_prev - m_new)
    p = jnp.exp2(s - m_new)
    l_sc[...] = alpha * l_sc[...] + jnp.sum(p, axis=1, keepdims=True)
    acc_sc[...] = alpha * acc_sc[...] + jnp.dot(p.astype(BF16), v, preferred_element_type=F32)
    m_sc[...] = m_new


def _nt_dot(a, b):
    return lax.dot_general(a, b, (((1,), (1,)), ((), ())), preferred_element_type=F32)


def _chunk_mask_bias(q0, tq, slope):
    qpos = q0 + lax.broadcasted_iota(jnp.int32, (tq, tq), 0)
    kpos = q0 + lax.broadcasted_iota(jnp.int32, (tq, tq), 1)
    visible = (kpos >> 6) <= (qpos >> 6)
    if slope is None:
        return jnp.where(visible, 0.0, NEG)
    rel = (qpos - q0 - jnp.abs(qpos - kpos)).astype(F32)
    return jnp.where(visible, slope * rel, NEG)


def _diff_attn_kernel(q_ref, k_ref, v_ref, slope_ref, lam_ref, g_ref, o_ref,
                      qs_sc, m_sc, l_sc, acc_sc, *, tq, tk, q_off, lam_init):
    qi = pl.program_id(2)
    q0 = q_off + qi * tq
    q = q_ref[...]
    lane = lax.broadcasted_iota(jnp.int32, q.shape, 1)
    zero = jnp.zeros_like(q)
    qs_sc[0:tq, :] = jnp.where(lane < A_DH, q, zero)
    qs_sc[tq:2 * tq, :] = jnp.where(lane >= A_DH, q, zero)
    m_sc[...] = jnp.full(m_sc.shape, NEG, F32)
    l_sc[...] = jnp.zeros(l_sc.shape, F32)
    acc_sc[...] = jnp.zeros(acc_sc.shape, F32)
    slope = slope_ref[...][:, 0:1]

    def body(kb, carry):
        ks = pl.multiple_of(kb * tk, tk)
        k = k_ref[pl.ds(ks, tk), :]
        v = v_ref[pl.ds(ks, tk), :]
        s = _nt_dot(qs_sc[...], k)
        kpos = ks + lax.broadcasted_iota(jnp.int32, (1, tk), 1)
        s = s + slope * (kpos - q0).astype(F32)
        _online_update(s, v, m_sc, l_sc, acc_sc)
        return carry

    lax.fori_loop(0, q0 // tk, body, 0)

    kd = pl.multiple_of(q0, tq)
    k = k_ref[pl.ds(kd, tq), :]
    v = v_ref[pl.ds(kd, tq), :]
    s = _nt_dot(qs_sc[...], k)
    bias = _chunk_mask_bias(q0, tq, slope)
    s = s + jnp.concatenate([bias, bias], axis=0)
    _online_update(s, v, m_sc, l_sc, acc_sc)

    lp = lam_ref[...]
    lam = (jnp.exp(jnp.sum(lp[0:1, :] * lp[1:2, :], axis=1, keepdims=True))
           - jnp.exp(jnp.sum(lp[2:3, :] * lp[3:4, :], axis=1, keepdims=True)) + lam_init)
    acc = acc_sc[...]
    l = l_sc[...]
    o = acc[0:tq] / l[0:tq] - lam * (acc[tq:2 * tq] / l[tq:2 * tq])
    o = _rms_rows(o, g_ref[...]) * (1.0 - lam_init)
    o_ref[...] = o.astype(o_ref.dtype)


def _mla_attn_kernel(q_ref, kn_ref, kr_ref, v_ref, o_ref, m_sc, l_sc, acc_sc, *, tq, tk, q_off):
    qi = pl.program_id(2)
    q0 = q_off + qi * tq
    m_sc[...] = jnp.full(m_sc.shape, NEG, F32)
    l_sc[...] = jnp.zeros(l_sc.shape, F32)
    acc_sc[...] = jnp.zeros(acc_sc.shape, F32)

    def scores(ks, size):
        k = jnp.concatenate([kn_ref[pl.ds(ks, size), :], kr_ref[pl.ds(ks, size), :]], axis=1)
        return _nt_dot(q_ref[...], k)

    def body(kb, carry):
        ks = pl.multiple_of(kb * tk, tk)
        _online_update(scores(ks, tk), v_ref[pl.ds(ks, tk), :], m_sc, l_sc, acc_sc)
        return carry

    lax.fori_loop(0, q0 // tk, body, 0)

    kd = pl.multiple_of(q0, tq)
    s = scores(kd, tq) + _chunk_mask_bias(q0, tq, None)
    _online_update(s, v_ref[pl.ds(kd, tq), :], m_sc, l_sc, acc_sc)
    o_ref[...] = (acc_sc[...] / l_sc[...]).astype(o_ref.dtype)


def _attn_tiles(lq, q_off):
    tq = _pick(lq, (512, 256, 128, 64))
    tk = tq if q_off == 0 else _pick(q_off, (512, 256, 128, 64))
    return tq, tk


def _diff_attn(q, kv, slopes, lam_p, subln, *, nb, lq, lk, q_off, q_row0, lam_init):
    tq, tk = _attn_tiles(lq, q_off)
    nq = lq // tq
    qb0 = q_row0 // tq
    kern = functools.partial(_diff_attn_kernel, tq=tq, tk=tk, q_off=q_off, lam_init=lam_init)
    return pl.pallas_call(
        kern, grid=(nb, A_HEADS, nq),
        in_specs=[
            pl.BlockSpec((tq, LANE), lambda b, h, i: (qb0 + b * nq + i, h)),
            pl.BlockSpec((lk, LANE), lambda b, h, i: (b, h)),
            pl.BlockSpec((lk, LANE), lambda b, h, i: (b, A_HEADS + h)),
            pl.BlockSpec((None, 1, LANE), lambda b, h, i: (h, 0, 0)),
            pl.BlockSpec((4, A_DH), lambda b, h, i: (0, 0)),
            pl.BlockSpec((1, LANE), lambda b, h, i: (0, 0)),
        ],
        out_specs=pl.BlockSpec((tq, LANE), lambda b, h, i: (b * nq + i, h)),
        out_shape=jax.ShapeDtypeStruct((nb * lq, A_HEADS * LANE), BF16),
        scratch_shapes=[pltpu.VMEM((2 * tq, LANE), BF16), pltpu.VMEM((2 * tq, 1), F32),
                        pltpu.VMEM((2 * tq, 1), F32), pltpu.VMEM((2 * tq, LANE), F32)],
        compiler_params=_cparams(("parallel", "parallel", "arbitrary")),
    )(q, kv, kv, slopes, lam_p, subln)


def _mla_attn(q, kv, kr, *, nb, lq, lk, q_off, q_row0, kv_row0):
    tq, tk = _attn_tiles(lq, q_off)
    nq = lq // tq
    qb0 = q_row0 // tq
    kb0 = kv_row0 // lk
    kern = functools.partial(_mla_attn_kernel, tq=tq, tk=tk, q_off=q_off)
    return pl.pallas_call(
        kern, grid=(nb, B_HEADS, nq),
        in_specs=[
            pl.BlockSpec((tq, 2 * LANE), lambda b, h, i: (qb0 + b * nq + i, h)),
            pl.BlockSpec((lk, LANE), lambda b, h, i: (kb0 + b, h)),
            pl.BlockSpec((lk, LANE), lambda b, h, i: (kb0 + b, 0)),
            pl.BlockSpec((lk, LANE), lambda b, h, i: (kb0 + b, B_HEADS + h)),
        ],
        out_specs=pl.BlockSpec((tq, LANE), lambda b, h, i: (b * nq + i, h)),
        out_shape=jax.ShapeDtypeStruct((nb * lq, B_HEADS * LANE), BF16),
        scratch_shapes=[pltpu.VMEM((tq, 1), F32), pltpu.VMEM((tq, 1), F32),
                        pltpu.VMEM((tq, LANE), F32)],
        compiler_params=_cparams(("parallel", "parallel", "arbitrary")),
    )(q, kv, kr, kv)


def _spatial_gate_kernel(u_ref, v_ref, w_ref, b_ref, o_ref, *, n_chunks):
    gw = u_ref.shape[1] // C_GROUPS
    row = lax.broadcasted_iota(jnp.int32, (C_BLOCK, C_BLOCK), 0)
    col = lax.broadcasted_iota(jnp.int32, (C_BLOCK, C_BLOCK), 1)
    for g in range(C_GROUPS):
        w = jnp.where(row >= col, w_ref[g], jnp.zeros((C_BLOCK, C_BLOCK), BF16))
        bias = b_ref[g]
        for c in range(n_chunks):
            rows = slice(c * C_BLOCK, (c + 1) * C_BLOCK)
            cols = slice(g * gw, (g + 1) * gw)
            vv = v_ref[rows, cols].astype(BF16)
            sg = jnp.dot(w, vv, preferred_element_type=F32) + bias
            o_ref[rows, cols] = (u_ref[rows, cols].astype(F32) * sg).astype(o_ref.dtype)


def _spatial_gate(u, v, w_sel, b_sel, n_prompt_rows):
    m, width = u.shape
    tm = _pick(m, (1024, 512, 256, 128))
    tm = math.gcd(tm, n_prompt_rows)
    npt = n_prompt_rows // tm
    sel = lambda i: jnp.where(i >= npt, 1, 0)
    kern = functools.partial(_spatial_gate_kernel, n_chunks=tm // C_BLOCK)
    return pl.pallas_call(
        kern, grid=(m // tm,),
        in_specs=[
            pl.BlockSpec((tm, width), lambda i: (i, 0)),
            pl.BlockSpec((tm, width), lambda i: (i, 0)),
            pl.BlockSpec((None, C_GROUPS, C_BLOCK, C_BLOCK), lambda i: (sel(i), 0, 0, 0)),
            pl.BlockSpec((None, C_GROUPS, C_BLOCK, 1), lambda i: (sel(i), 0, 0, 0)),
        ],
        out_specs=pl.BlockSpec((tm, width), lambda i: (i, 0)),
        out_shape=jax.ShapeDtypeStruct((m, width), BF16),
        compiler_params=_cparams(("parallel",)),
    )(u, v, w_sel, b_sel)


def _pool_kernel(x_ref, prev_ref, hist_ref, w_ref, sc_ref, o_ref, xp_sc, *, tm, past):
    i = pl.program_id(1)
    xp_sc[0:D_HALO, :] = jnp.where(i == 0, hist_ref[...], prev_ref[...])
    xp_sc[D_HALO:D_HALO + tm, :] = x_ref[...]
    gw = x_ref.shape[1] // len(D_WINDOWS)
    pos = past + i * tm + lax.broadcasted_iota(jnp.int32, (tm, 1), 0)
    for gi, win in enumerate(D_WINDOWS):
        cols = slice(gi * gw, (gi + 1) * gw)
        tot = xp_sc[D_HALO:D_HALO + tm, cols]
        for j in range(1, win):
            tot = tot + xp_sc[D_HALO - j:D_HALO - j + tm, cols]
        cnt = jnp.minimum(pos + 1, win).astype(F32)
        pooled = tot / cnt - xp_sc[D_HALO:D_HALO + tm, cols]
        y = jnp.dot(pooled.astype(BF16), w_ref[gi], preferred_element_type=F32)
        o_ref[:, cols] = (y * sc_ref[:, cols]).astype(o_ref.dtype)


def _pool(d_in, hist, w_grp, scale, *, nb, lq, row0, past):
    width = d_in.shape[1]
    tm = _pick(lq, (512, 256, 128, 64))
    nt = lq // tm
    rb0 = row0 // tm
    hb0 = row0 // D_HALO
    per = tm // D_HALO
    kern = functools.partial(_pool_kernel, tm=tm, past=past)
    return pl.pallas_call(
        kern, grid=(nb, nt),
        in_specs=[
            pl.BlockSpec((tm, width), lambda b, i: (rb0 + b * nt + i, 0)),
            pl.BlockSpec((D_HALO, width),
                         lambda b, i: (jnp.maximum(hb0 + (b * nt + i) * per - 1, 0), 0)),
            pl.BlockSpec((None, D_HALO, width), lambda b, i: (b, 0, 0)),
            pl.BlockSpec(w_grp.shape, lambda b, i: (0, 0, 0)),
            pl.BlockSpec((1, width), lambda b, i: (0, 0)),
        ],
        out_specs=pl.BlockSpec((tm, width), lambda b, i: (b * nt + i, 0)),
        out_shape=jax.ShapeDtypeStruct((nb * lq, width), BF16),
        scratch_shapes=[pltpu.VMEM((D_HALO + tm, width), F32)],
        compiler_params=_cparams(("parallel", "arbitrary")),
    )(d_in, d_in, hist, w_grp, scale)


def _merge_kernel(x_ref, wg0, wg1, wg2, wg3, b0, b1, b2, b3, wb_ref, o_ref):
    x = x_ref[...]
    y = None
    for n, (wg, br) in enumerate(((wg0, b0), (wg1, b1), (wg2, b2), (wg3, b3))):
        gate = jax.nn.sigmoid(jnp.dot(x, wg[...], preferred_element_type=F32))
        t = gate * jnp.dot(br[...], wb_ref[n], preferred_element_type=F32)
        y = t if y is None else y + t
    o_ref[...] = y.astype(o_ref.dtype)


def _merge(xb, w_gate, branches, w_branch):
    m, d = xb.shape
    bw = branches[0].shape[1]
    tm = _pick(m, (512, 256, 128))
    tn = 512
    nj = d // tn
    in_specs = [pl.BlockSpec((tm, d), lambda i, j: (i, 0))]
    in_specs += [pl.BlockSpec((d, tn), lambda i, j, n=n: (0, n * nj + j)) for n in range(N_BRANCH)]
    in_specs += [pl.BlockSpec((tm, bw), lambda i, j: (i, 0)) for _ in range(N_BRANCH)]
    in_specs += [pl.BlockSpec((N_BRANCH, bw, tn), lambda i, j: (0, 0, j))]
    return pl.pallas_call(
        _merge_kernel, grid=(m // tm, nj), in_specs=in_specs,
        out_specs=pl.BlockSpec((tm, tn), lambda i, j: (i, j)),
        out_shape=jax.ShapeDtypeStruct((m, d), BF16),
        compiler_params=_cparams(("parallel", "arbitrary")),
    )(xb, w_gate, w_gate, w_gate, w_gate, *branches, w_branch)


def _out_ln_kernel(y_ref, x_ref, w_ref, g_ref, b_ref, o_ref, ob_ref, *, alpha):
    m = jnp.dot(y_ref[...], w_ref[...], preferred_element_type=F32)
    z = _ln_rows(alpha * x_ref[...] + m, g_ref[...], b_ref[...])
    o_ref[...] = z
    ob_ref[...] = z.astype(ob_ref.dtype)


def _out_ln(y, x, w_out, g, b, alpha):
    m, d = x.shape
    tm = _pick(m, (256, 128))
    row = lambda i: (i, 0)
    fixed = lambda i: (0, 0)
    return pl.pallas_call(
        functools.partial(_out_ln_kernel, alpha=alpha), grid=(m // tm,),
        in_specs=[pl.BlockSpec((tm, d), row), pl.BlockSpec((tm, d), row), pl.BlockSpec((d, d), fixed),
                  pl.BlockSpec((1, d), fixed), pl.BlockSpec((1, d), fixed)],
        out_specs=[pl.BlockSpec((tm, d), row), pl.BlockSpec((tm, d), row)],
        out_shape=[jax.ShapeDtypeStruct((m, d), F32), jax.ShapeDtypeStruct((m, d), BF16)],
        compiler_params=_cparams(("parallel",)),
    )(y, x, w_out, g, b)


def _ffn_kernel(*refs, alpha, gated):
    if gated:
        xb_ref, x_ref, wg_ref, wu_ref, wd_ref, gates_ref, g_ref, b_ref, o_ref, ob_ref, acc_sc = refs
    else:
        xb_ref, x_ref, wg_ref, wu_ref, wd_ref, g_ref, b_ref, o_ref, ob_ref, acc_sc = refs
    e = pl.program_id(1)
    f = pl.program_id(2)

    @pl.when((e == 0) & (f == 0))
    def _():
        acc_sc[...] = jnp.zeros(acc_sc.shape, F32)

    xb = xb_ref[...]
    hg = jnp.dot(xb, wg_ref[...], preferred_element_type=F32)
    hu = jnp.dot(xb, wu_ref[...], preferred_element_type=F32)
    h = hg * jax.nn.sigmoid(hg) * hu
    if gated:
        gates = gates_ref[...]
        lane = lax.broadcasted_iota(jnp.int32, gates.shape, 1)
        h = h * jnp.sum(jnp.where(lane == e, gates, 0.0), axis=1, keepdims=True)
    acc_sc[...] += jnp.dot(h.astype(BF16), wd_ref[...], preferred_element_type=F32)

    @pl.when((e == pl.num_programs(1) - 1) & (f == pl.num_programs(2) - 1))
    def _():
        z = _ln_rows(alpha * x_ref[...] + acc_sc[...], g_ref[...], b_ref[...])
        o_ref[...] = z
        ob_ref[...] = z.astype(ob_ref.dtype)


def _ffn(xb, x, wg, wu, wd, g, b, alpha, gates=None):
    m, d = x.shape
    ne, _, ff = wg.shape
    tm = _pick(m, (512, 256, 128))
    tf = _pick(ff, (512, 256, 128))
    row = lambda i, e, f: (i, 0)
    fixed = lambda i, e, f: (0, 0)
    in_specs = [pl.BlockSpec((tm, d), row), pl.BlockSpec((tm, d), row),
                pl.BlockSpec((None, d, tf), lambda i, e, f: (e, 0, f)),
                pl.BlockSpec((None, d, tf), lambda i, e, f: (e, 0, f)),
                pl.BlockSpec((None, tf, d), lambda i, e, f: (e, f, 0))]
    args = [xb, x, wg, wu, wd]
    if gates is not None:
        in_specs.append(pl.BlockSpec((tm, LANE), row))
        args.append(gates)
    in_specs += [pl.BlockSpec((1, d), fixed), pl.BlockSpec((1, d), fixed)]
    args += [g, b]
    return pl.pallas_call(
        functools.partial(_ffn_kernel, alpha=alpha, gated=gates is not None),
        grid=(m // tm, ne, ff // tf), in_specs=in_specs,
        out_specs=[pl.BlockSpec((tm, d), row), pl.BlockSpec((tm, d), row)],
        out_shape=[jax.ShapeDtypeStruct((m, d), F32), jax.ShapeDtypeStruct((m, d), BF16)],
        scratch_shapes=[pltpu.VMEM((tm, d), F32)],
        compiler_params=_cparams(("parallel", "arbitrary", "arbitrary")),
    )(*args)


def _router_kernel(x_ref, w_ref, o_ref, *, n_experts):
    logits = jnp.dot(x_ref[...], w_ref[...], preferred_element_type=F32,
                     precision=lax.Precision.HIGHEST)
    lane = lax.broadcasted_iota(jnp.int32, logits.shape, 1).astype(F32)
    lg = jnp.where(lane < n_experts, logits, NEG)
    m1 = jnp.max(lg, axis=1, keepdims=True)
    i1 = jnp.min(jnp.where(lg == m1, lane, float(LANE)), axis=1, keepdims=True)
    lg2 = jnp.where(lane == i1, NEG, lg)
    m2 = jnp.max(lg2, axis=1, keepdims=True)
    i2 = jnp.min(jnp.where(lg2 == m2, lane, float(LANE)), axis=1, keepdims=True)
    ex = jnp.exp(m2 - m1)
    p1 = 1.0 / (1.0 + ex)
    p2 = ex / (1.0 + ex)
    o_ref[...] = jnp.where(lane == i1, p1, 0.0) + jnp.where(lane == i2, p2, 0.0)


def _router(x, w_router):
    m, d = x.shape
    ne = w_router.shape[1]
    w = jnp.pad(w_router, ((0, 0), (0, LANE - ne)))
    tm = _pick(m, (512, 256, 128))
    return pl.pallas_call(
        functools.partial(_router_kernel, n_experts=ne), grid=(m // tm,),
        in_specs=[pl.BlockSpec((tm, d), lambda i: (i, 0)), pl.BlockSpec((d, LANE), lambda i: (0, 0))],
        out_specs=pl.BlockSpec((tm, LANE), lambda i: (i, 0)),
        out_shape=jax.ShapeDtypeStruct((m, LANE), F32),
        compiler_params=_cparams(("parallel",)),
    )(x, w)


def _epi_scale(scale):
    return lambda accs, tabs, vecs: [accs[0] * scale]


def _epi_f32_and_bf16(accs, tabs, vecs):
    return [accs[0], accs[0]]


def _epi_rms(accs, tabs, vecs):
    r = _rms_rows(accs[0], vecs[0])
    return [r, r]


def _epi_rope(accs, tabs, vecs):
    return [accs[0] * tabs[0] + accs[1] * tabs[1]]


def _epi_gelu(accs, tabs, vecs):
    return [jax.nn.gelu(accs[0])]


def _epi_gelu_ln(accs, tabs, vecs):
    return [_ln_rows(jax.nn.gelu(accs[0]), vecs[0], vecs[1])]


def _epi_plain(accs, tabs, vecs):
    return [accs[0]]


def _epi_q_rope(scale):
    return lambda accs, tabs, vecs: [(accs[0] * tabs[0] + accs[1] * tabs[1]) * scale]


def _rope_tables(pos):
    half = B_ROPE // 2
    inv = ROPE_THETA ** (-jnp.arange(half, dtype=F32) / half)
    ang = pos.astype(F32)[:, None] * inv[None, :]
    cos, sin = jnp.cos(ang), jnp.sin(ang)
    cos2 = jnp.concatenate([cos, cos], axis=1)
    sin2 = jnp.concatenate([-sin, sin], axis=1)
    return cos2, sin2


def _swap_halves(w):
    half = w.shape[-1] // 2
    return jnp.concatenate([w[..., half:], w[..., :half]], axis=-1)


def kernel(x_prompt, x_sample, cache_a_k, cache_a_v, cache_b_latent, cache_b_krope, state_d_pool, w_in, a_lambda, a_subln, b_q_norm, b_w_uq, b_kv_norm, b_w_ukv, c_ln_g, c_ln_b, c_w_s, c_b_s, d_w_grp, d_scale, w_branch, w_out, ln1_g, ln1_b, ln2_g, ln2_b, f_w_gate, f_w_up, f_w_down, m_router, m_w_gate, m_w_up, m_w_down):
    depth = w_in.shape[0]
    nbp, lp, d_model = x_prompt.shape
    nbs, ls, _ = x_sample.shape
    past = cache_a_k.shape[2]
    lks = past + ls
    tp, ts = nbp * lp, nbs * ls
    aw = A_HEADS * 2 * A_DH
    q_rank = b_q_norm.shape[1]
    kv_rank = b_kv_norm.shape[1]
    c_width = c_ln_g.shape[1]
    d_width = d_scale.shape[1]
    alpha = (2.0 * depth) ** 0.25
    sizes = (aw, aw, aw, q_rank, kv_rank, B_ROPE, c_width, c_width, d_width) + (d_model,) * N_BRANCH
    offs = [0]
    for s in sizes:
        offs.append(offs[-1] + s)

    x = jnp.concatenate([x_prompt.reshape(tp, d_model), x_sample.reshape(ts, d_model)], axis=0)
    xb = x.astype(BF16)

    pos = jnp.concatenate([jnp.tile(jnp.arange(lp, dtype=jnp.int32), nbp),
                           jnp.tile(past + jnp.arange(ls, dtype=jnp.int32), nbs)])
    cos2, sin2 = _rope_tables(pos)
    rows = pos.shape[0]
    ones = jnp.ones((rows, B_NOPE), F32)
    zeros_n = jnp.zeros((rows, B_NOPE), F32)
    zeros_r = jnp.zeros((rows, 2 * LANE - B_NOPE - B_ROPE), F32)
    q_cos = jnp.concatenate([ones, cos2, zeros_r], axis=1)
    q_sin = jnp.concatenate([zeros_n, sin2, zeros_r], axis=1)

    slopes = jnp.exp2(-8.0 * jnp.arange(1, A_HEADS + 1, dtype=F32) / A_HEADS) * LOG2E
    slopes = jnp.broadcast_to(slopes[:, None, None], (A_HEADS, 1, LANE))

    states_p, states_s = [], []
    for l in range(depth):
        wl = w_in[l]
        seg = lambda k: wl[:, offs[k]:offs[k + 1]].astype(BF16)
        lam_init = 0.8 - 0.6 * math.exp(-0.3 * l)

        (q_a,) = _fused_mm(_epi_scale(A_DH ** -0.5 * LOG2E), xb, [seg(0)], [BF16], tn=1024)
        w_kv = jnp.concatenate([seg(1), seg(2)], axis=1)
        kv_a, kv_a_b = _fused_mm(_epi_f32_and_bf16, xb, [w_kv], [F32, BF16], tn=1024)
        subln = a_subln[l].reshape(1, 2 * A_DH)
        o_a_p = _diff_attn(q_a, kv_a_b, slopes, a_lambda[l], subln, nb=nbp, lq=lp, lk=lp, q_off=0,
                           q_row0=0, lam_init=lam_init)
        cache_kv = jnp.concatenate([cache_a_k[l].reshape(nbs, past, aw),
                                    cache_a_v[l].reshape(nbs, past, aw)], axis=2).astype(BF16)
        kv_s = jnp.concatenate([cache_kv, kv_a_b[tp:].reshape(nbs, ls, 2 * aw)], axis=1)
        o_a_s = _diff_attn(q_a, kv_s.reshape(nbs * lks, 2 * aw), slopes, a_lambda[l], subln, nb=nbs,
                           lq=ls, lk=lks, q_off=past, q_row0=tp, lam_init=lam_init)
        o_a = jnp.concatenate([o_a_p, o_a_s], axis=0)

        (cq,) = _fused_mm(_epi_rms, xb, [seg(3)], [BF16], tn=q_rank,
                          vecs=[b_q_norm[l].reshape(1, q_rank)])
        lat, lat_b = _fused_mm(_epi_rms, xb, [seg(4)], [F32, BF16], tn=kv_rank,
                               vecs=[b_kv_norm[l].reshape(1, kv_rank)])
        w_kr = wl[:, offs[5]:offs[6]]
        (kr,) = _fused_mm(_epi_rope, xb, [w_kr.astype(BF16), _swap_halves(w_kr).astype(BF16)], [F32],
                          tn=B_ROPE, tabs=[cos2, sin2])
        w_uq = b_w_uq[l].reshape(q_rank, B_HEADS, B_NOPE + B_ROPE)
        w_uq_n, w_uq_r = w_uq[..., :B_NOPE], w_uq[..., B_NOPE:]
        zpad = jnp.zeros((q_rank, B_HEADS, 2 * LANE - B_NOPE - B_ROPE), F32)
        w_q1 = jnp.concatenate([w_uq_n, w_uq_r, zpad], axis=-1).reshape(q_rank, -1).astype(BF16)
        w_q2 = jnp.concatenate([jnp.zeros_like(w_uq_n), _swap_halves(w_uq_r), zpad],
                               axis=-1).reshape(q_rank, -1).astype(BF16)
        (q_b,) = _fused_mm(_epi_q_rope((B_NOPE + B_ROPE) ** -0.5 * LOG2E), cq, [w_q1, w_q2], [BF16],
                           tn=2 * LANE, tabs=[q_cos, q_sin])
        w_ukv = b_w_ukv[l].reshape(kv_rank, B_HEADS, B_NOPE + B_VDIM)
        w_up = jnp.concatenate([w_ukv[..., :B_NOPE].reshape(kv_rank, -1),
                                w_ukv[..., B_NOPE:].reshape(kv_rank, -1)], axis=1).astype(BF16)
        lat_s = jnp.concatenate([cache_b_latent[l].astype(BF16), lat_b[tp:].reshape(nbs, ls, kv_rank)],
                                axis=1).reshape(nbs * lks, kv_rank)
        lat_cat = jnp.concatenate([lat_b[:tp], lat_s], axis=0)
        (kv_b,) = _fused_mm(_epi_plain, lat_cat, [w_up], [BF16], tn=1024)
        kr_s = jnp.concatenate([cache_b_krope[l], kr[tp:].reshape(nbs, ls, B_ROPE)],
                               axis=1).reshape(nbs * lks, B_ROPE)
        kr_cat = jnp.pad(jnp.concatenate([kr[:tp], kr_s], axis=0),
                         ((0, 0), (0, LANE - B_ROPE))).astype(BF16)
        o_b_p = _mla_attn(q_b, kv_b, kr_cat, nb=nbp, lq=lp, lk=lp, q_off=0, q_row0=0, kv_row0=0)
        if tp % lks:
            kv_b_s, kr_b_s, s_row0 = kv_b[tp:], kr_cat[tp:], 0
        else:
            kv_b_s, kr_b_s, s_row0 = kv_b, kr_cat, tp
        o_b_s = _mla_attn(q_b, kv_b_s, kr_b_s, nb=nbs, lq=ls, lk=lks, q_off=past, q_row0=tp,
                          kv_row0=s_row0)
        o_b = jnp.concatenate([o_b_p, o_b_s], axis=0)

        (u_c,) = _fused_mm(_epi_gelu, xb, [seg(6)], [BF16], tn=1024)
        (v_c,) = _fused_mm(_epi_gelu_ln, xb, [seg(7)], [F32], tn=c_width,
                           vecs=[c_ln_g[l].reshape(1, c_width), c_ln_b[l].reshape(1, c_width)])
        w_s, b_s = c_w_s[l], c_b_s[l]
        reps = C_BLOCK // ls if ls < C_BLOCK else 1
        w_tl = w_s[:, :C_BLOCK // reps, :C_BLOCK // reps]
        w_smp = jnp.zeros_like(w_s)
        for r in range(reps):
            sl = slice(r * (C_BLOCK // reps), (r + 1) * (C_BLOCK // reps))
            w_smp = w_smp.at[:, sl, sl].set(w_tl)
        b_smp = jnp.tile(b_s[:, :C_BLOCK // reps], (1, reps))
        w_sel = jnp.stack([w_s, w_smp]).astype(BF16)
        b_sel = jnp.stack([b_s, b_smp])[..., None]
        o_c = _spatial_gate(u_c, v_c, w_sel, b_sel, tp)

        (d_in,) = _fused_mm(_epi_plain, xb, [seg(8)], [F32], tn=1024)
        w_grp = d_w_grp[l].astype(BF16)
        scale = d_scale[l].reshape(1, d_width)
        hist_p = jnp.zeros((nbp, D_HALO, d_width), F32)
        hist_s = jnp.pad(state_d_pool[l], ((0, 0), (D_HALO - D_HIST, 0), (0, 0)))
        o_d_p = _pool(d_in, hist_p, w_grp, scale, nb=nbp, lq=lp, row0=0, past=0)
        o_d_s = _pool(d_in, hist_s, w_grp, scale, nb=nbs, lq=ls, row0=tp, past=past)
        o_d = jnp.concatenate([o_d_p, o_d_s], axis=0)

        w_gate = wl[:, offs[9]:offs[13]].astype(BF16)
        y = _merge(xb, w_gate, [o_a, o_b, o_c, o_d], w_branch[l].astype(BF16))
        x, xb = _out_ln(y, x, w_out[l].astype(BF16), ln1_g[l].reshape(1, -1), ln1_b[l].reshape(1, -1),
                        alpha)

        i = l // 2
        g2, b2 = ln2_g[l].reshape(1, -1), ln2_b[l].reshape(1, -1)
        if l % 2 == 0:
            x, xb = _ffn(xb, x, f_w_gate[i][None].astype(BF16), f_w_up[i][None].astype(BF16),
                         f_w_down[i][None].astype(BF16), g2, b2, alpha)
        else:
            gates = _router(x, m_router[i])
            x, xb = _ffn(xb, x, m_w_gate[i].astype(BF16), m_w_up[i].astype(BF16),
                         m_w_down[i].astype(BF16), g2, b2, alpha, gates=gates)

        kv_p = kv_a[:tp].reshape(nbp, lp, 2, A_HEADS, 2 * A_DH)
        kv_smp = kv_a[tp:].reshape(nbs, ls, 2, A_HEADS, 2 * A_DH)
        states_p.append((kv_p[:, :, 0], kv_p[:, :, 1], lat[:tp].reshape(nbp, lp, kv_rank),
                         kr[:tp].reshape(nbp, lp, B_ROPE),
                         d_in[:tp].reshape(nbp, lp, d_width)[:, lp - D_HIST:]))
        states_s.append((kv_smp[:, :, 0], kv_smp[:, :, 1], lat[tp:].reshape(nbs, ls, kv_rank),
                         kr[tp:].reshape(nbs, ls, B_ROPE), v_c[tp:].reshape(nbs, ls, c_width),
                         jnp.concatenate([state_d_pool[l], d_in[tp:].reshape(nbs, ls, d_width)],
                                         axis=1)[:, -D_HIST:]))

    yp = x[:tp].reshape(nbp, lp, d_model)
    ys = x[tp:].reshape(nbs, ls, d_model)
    stack = lambda sts, k: jnp.stack([s[k] for s in sts])
    return (yp, ys, stack(states_p, 0), stack(states_p, 1), stack(states_p, 2), stack(states_p, 3),
            stack(states_p, 4), stack(states_s, 0), stack(states_s, 1), stack(states_s, 2),
            stack(states_s, 3), stack(states_s, 4), stack(states_s, 5))
```

```python
import functools
import math

import jax
import jax.numpy as jnp
from jax import lax
from jax.experimental import pallas as pl
from jax.experimental.pallas import tpu as pltpu

CHUNK = 64
A_HEADS = 8
A_DH = 64
B_HEADS = 8
B_NOPE = 128
B_ROPE = 64
B_VDIM = 128
C_GROUPS = 4
C_BLOCK = 128
D_WINDOWS = (2, 4, 8, 16)
D_HIST = 15
D_HALO = 16
N_BRANCH = 4
ROPE_THETA = 10000.0
LN_EPS = 1e-5
RMS_EPS = 1e-6
LOG2E = 1.4426950408889634
NEG = -1e30
LANE = 128
VMEM_LIMIT = 56 * 1024 * 1024

F32 = jnp.float32
BF16 = jnp.bfloat16


def _pick(n, cands):
    for c in cands:
        if n % c == 0:
            return c
    raise ValueError(f"no tile for {n} in {cands}")


def _cparams(sem):
    return pltpu.CompilerParams(dimension_semantics=sem, vmem_limit_bytes=VMEM_LIMIT)


def _ln_rows(z, g, b):
    mu = jnp.mean(z, axis=-1, keepdims=True)
    zc = z - mu
    var = jnp.mean(zc * zc, axis=-1, keepdims=True)
    return zc * lax.rsqrt(var + LN_EPS) * g + b


def _rms_rows(z, g):
    return z * lax.rsqrt(jnp.mean(z * z, axis=-1, keepdims=True) + RMS_EPS) * g


def _fused_mm_kernel(*refs, fn, nw, ntab, nvec, precision):
    x = refs[0][...]
    w_refs = refs[1:1 + nw]
    tab_refs = refs[1 + nw:1 + nw + ntab]
    vec_refs = refs[1 + nw + ntab:1 + nw + ntab + nvec]
    out_refs = refs[1 + nw + ntab + nvec:]
    accs = [jnp.dot(x, w[...], preferred_element_type=F32, precision=precision) for w in w_refs]
    res = fn(accs, [t[...] for t in tab_refs], [v[...] for v in vec_refs])
    for o, r in zip(out_refs, res):
        o[...] = r.astype(o.dtype)


def _fused_mm(fn, x, ws, out_dtypes, *, tn, tm=None, tabs=(), tabs_follow_cols=False, vecs=(),
              precision=None):
    m, kdim = x.shape
    n = ws[0].shape[1]
    tm = tm or _pick(m, (1024, 512, 256, 128))
    grid = (m // tm, n // tn)
    in_specs = [pl.BlockSpec((tm, kdim), lambda i, j: (i, 0))]
    in_specs += [pl.BlockSpec((kdim, tn), lambda i, j: (0, j)) for _ in ws]
    for t in tabs:
        if tabs_follow_cols:
            in_specs.append(pl.BlockSpec((tm, tn), lambda i, j: (i, j)))
        else:
            in_specs.append(pl.BlockSpec((tm, t.shape[1]), lambda i, j: (i, 0)))
    in_specs += [pl.BlockSpec((1, tn), lambda i, j: (0, j)) for _ in vecs]
    out_specs = [pl.BlockSpec((tm, tn), lambda i, j: (i, j)) for _ in out_dtypes]
    out_shape = [jax.ShapeDtypeStruct((m, n), d) for d in out_dtypes]
    kern = functools.partial(_fused_mm_kernel, fn=fn, nw=len(ws), ntab=len(tabs), nvec=len(vecs),
                             precision=precision)
    return pl.pallas_call(
        kern, grid=grid, in_specs=in_specs, out_specs=out_specs, out_shape=out_shape,
        compiler_params=_cparams(("parallel", "arbitrary")),
    )(x, *ws, *tabs, *vecs)


ATTN_ROW_GROUPS = 2


def _row_groups(rows):
    g = ATTN_ROW_GROUPS
    while rows % (16 * g):
        g //= 2
    return g


def _nt_dot(a, b):
    return lax.dot_general(a, b, (((1,), (1,)), ((), ())), preferred_element_type=F32)


def _softmax_pv(scores, v, biases, m_sc, acc_sc):
    width = v.shape[0]
    v_ext = jnp.concatenate([v, jnp.ones(v.shape, v.dtype)], axis=1)
    r0 = 0
    for s, bias in zip(scores, biases):
        rows = slice(r0, r0 + s.shape[0])
        r0 += s.shape[0]
        if bias is not None:
            s = s + bias
        m_prev = m_sc[rows, :]
        m_new = jnp.maximum(m_prev, jnp.max(s, axis=1, keepdims=True))
        alpha = jnp.exp2(m_prev - m_new)
        m_rep = jnp.tile(m_new, (1, width // LANE)) if width >= LANE else m_new[:, :width]
        p = jnp.exp2(s - m_rep)
        pv = jnp.dot(p.astype(BF16), v_ext, preferred_element_type=F32)
        acc_sc[rows, :] = jnp.tile(alpha, (1, 2)) * acc_sc[rows, :] + pv
        m_sc[rows, :] = m_new


def _sweep_keys(q_groups, keys, v_ref, off_biases, diag_biases, m_sc, acc_sc, s_sc, *, q0, tq, tk):
    m_sc[...] = jnp.full(m_sc.shape, NEG, F32)
    acc_sc[...] = jnp.zeros(acc_sc.shape, F32)
    n_off = q0 // tk
    kd = pl.multiple_of(q0, tq)

    def scores(start, size):
        k = keys(start, size)
        return [_nt_dot(q, k) for q in q_groups()]

    if s_sc is None:
        def body(j, carry):
            ks = pl.multiple_of(j * tk, tk)
            _softmax_pv(scores(ks, tk), v_ref[pl.ds(ks, tk), :], off_biases(ks), m_sc, acc_sc)
            return carry

        lax.fori_loop(0, n_off, body, 0)
        _softmax_pv(scores(kd, tq), v_ref[pl.ds(kd, tq), :], diag_biases, m_sc, acc_sc)
        return

    assert tq == tk
    bounds = []
    r0 = 0
    for q in q_groups():
        bounds.append(slice(r0, r0 + q.shape[0]))
        r0 += q.shape[0]

    def stash(new_scores):
        for rows, s in zip(bounds, new_scores):
            s_sc[rows, :] = s

    def staged():
        return [s_sc[rows, :] for rows in bounds]

    stash(scores(0, tk))

    def body(j, carry):
        ks = pl.multiple_of(j * tk, tk)
        nxt = scores(pl.multiple_of(ks + tk, tk), tk)
        _softmax_pv(staged(), v_ref[pl.ds(ks, tk), :], off_biases(ks), m_sc, acc_sc)
        stash(nxt)
        return carry

    lax.fori_loop(0, n_off, body, 0)
    _softmax_pv(staged(), v_ref[pl.ds(kd, tq), :], diag_biases, m_sc, acc_sc)


def _chunk_mask_bias(q0, tq, slope):
    qpos = q0 + lax.broadcasted_iota(jnp.int32, (tq, tq), 0)
    kpos = q0 + lax.broadcasted_iota(jnp.int32, (tq, tq), 1)
    visible = (kpos >> 6) <= (qpos >> 6)
    if slope is None:
        return jnp.where(visible, 0.0, NEG)
    rel = (qpos - q0 - jnp.abs(qpos - kpos)).astype(F32)
    return jnp.where(visible, slope * rel, NEG)


def _diff_attn_kernel(q_ref, k_ref, v_ref, slope_ref, lam_ref, g_ref, o_ref,
                      qs_sc, m_sc, acc_sc, *maybe_s_sc, tq, tk, q_off, lam_init):
    qi = pl.program_id(2)
    q0 = q_off + qi * tq
    q = q_ref[...]
    lane = lax.broadcasted_iota(jnp.int32, q.shape, 1)
    zero = jnp.zeros_like(q)
    qs_sc[0:tq, :] = jnp.where(lane < A_DH, q, zero)
    qs_sc[tq:2 * tq, :] = jnp.where(lane >= A_DH, q, zero)
    slope = slope_ref[...][:, 0:1]
    groups = _row_groups(2 * tq)
    gr = 2 * tq // groups

    def q_groups():
        return [qs_sc[g * gr:(g + 1) * gr, :] for g in range(groups)]

    def off_biases(ks):
        kpos = ks + lax.broadcasted_iota(jnp.int32, (1, tk), 1)
        return [slope * (kpos - q0).astype(F32)] * groups

    bias = _chunk_mask_bias(q0, tq, slope)
    diag_biases = [bias[(g * gr) % tq:(g * gr) % tq + gr, :] for g in range(groups)]
    _sweep_keys(q_groups, lambda ks, size: k_ref[pl.ds(ks, size), :], v_ref, off_biases, diag_biases,
                m_sc, acc_sc, maybe_s_sc[0] if maybe_s_sc else None, q0=q0, tq=tq, tk=tk)

    lp = lam_ref[...]
    lam = (jnp.exp(jnp.sum(lp[0:1, :] * lp[1:2, :], axis=1, keepdims=True))
           - jnp.exp(jnp.sum(lp[2:3, :] * lp[3:4, :], axis=1, keepdims=True)) + lam_init)
    acc = acc_sc[...]
    o1 = acc[0:tq, 0:LANE] / acc[0:tq, LANE:2 * LANE]
    o2 = acc[tq:2 * tq, 0:LANE] / acc[tq:2 * tq, LANE:2 * LANE]
    o = _rms_rows(o1 - lam * o2, g_ref[...]) * (1.0 - lam_init)
    o_ref[...] = o.astype(o_ref.dtype)


def _mla_attn_kernel(q_ref, kn_ref, kr_ref, v_ref, o_ref, m_sc, acc_sc, *maybe_s_sc,
                     tq, tk, q_off):
    qi = pl.program_id(2)
    q0 = q_off + qi * tq
    groups = _row_groups(tq)
    gr = tq // groups

    def q_groups():
        return [q_ref[g * gr:(g + 1) * gr, :] for g in range(groups)]

    def keys(ks, size):
        return jnp.concatenate([kn_ref[pl.ds(ks, size), :], kr_ref[pl.ds(ks, size), :]], axis=1)

    mask = _chunk_mask_bias(q0, tq, None)
    diag_biases = [mask[g * gr:(g + 1) * gr, :] for g in range(groups)]
    _sweep_keys(q_groups, keys, v_ref, lambda ks: [None] * groups, diag_biases, m_sc, acc_sc,
                maybe_s_sc[0] if maybe_s_sc else None, q0=q0, tq=tq, tk=tk)
    acc = acc_sc[...]
    o_ref[...] = (acc[:, 0:LANE] / acc[:, LANE:2 * LANE]).astype(o_ref.dtype)


def _attn_tiles(lq, q_off):
    tq = _pick(lq, (512, 256, 128, 64))
    tk = tq if q_off == 0 else _pick(q_off, (512, 256, 128, 64))
    return tq, tk


def _score_stage(rows, tq, tk):
    return [pltpu.VMEM((rows, tk), F32)] if tq == tk else []


def _diff_attn(q, kv, slopes, lam_p, subln, *, nb, lq, lk, q_off, q_row0, lam_init):
    tq, tk = _attn_tiles(lq, q_off)
    nq = lq // tq
    qb0 = q_row0 // tq
    kern = functools.partial(_diff_attn_kernel, tq=tq, tk=tk, q_off=q_off, lam_init=lam_init)
    return pl.pallas_call(
        kern, grid=(nb, A_HEADS, nq),
        in_specs=[
            pl.BlockSpec((tq, LANE), lambda b, h, i: (qb0 + b * nq + i, h)),
            pl.BlockSpec((lk, LANE), lambda b, h, i: (b, h)),
            pl.BlockSpec((lk, LANE), lambda b, h, i: (b, A_HEADS + h)),
            pl.BlockSpec((None, 1, LANE), lambda b, h, i: (h, 0, 0)),
            pl.BlockSpec((4, A_DH), lambda b, h, i: (0, 0)),
            pl.BlockSpec((1, LANE), lambda b, h, i: (0, 0)),
        ],
        out_specs=pl.BlockSpec((tq, LANE), lambda b, h, i: (b * nq + i, h)),
        out_shape=jax.ShapeDtypeStruct((nb * lq, A_HEADS * LANE), BF16),
        scratch_shapes=[pltpu.VMEM((2 * tq, LANE), BF16), pltpu.VMEM((2 * tq, LANE), F32),
                        pltpu.VMEM((2 * tq, 2 * LANE), F32)] + _score_stage(2 * tq, tq, tk),
        compiler_params=_cparams(("parallel", "parallel", "arbitrary")),
    )(q, kv, kv, slopes, lam_p, subln)


def _mla_attn(q, kv, kr, *, nb, lq, lk, q_off, q_row0, kv_row0):
    tq, tk = _attn_tiles(lq, q_off)
    nq = lq // tq
    qb0 = q_row0 // tq
    kb0 = kv_row0 // lk
    kern = functools.partial(_mla_attn_kernel, tq=tq, tk=tk, q_off=q_off)
    return pl.pallas_call(
        kern, grid=(nb, B_HEADS, nq),
        in_specs=[
            pl.BlockSpec((tq, 2 * LANE), lambda b, h, i: (qb0 + b * nq + i, h)),
            pl.BlockSpec((lk, LANE), lambda b, h, i: (kb0 + b, h)),
            pl.BlockSpec((lk, LANE), lambda b, h, i: (kb0 + b, 0)),
            pl.BlockSpec((lk, LANE), lambda b, h, i: (kb0 + b, B_HEADS + h)),
        ],
        out_specs=pl.BlockSpec((tq, LANE), lambda b, h, i: (b * nq + i, h)),
        out_shape=jax.ShapeDtypeStruct((nb * lq, B_HEADS * LANE), BF16),
        scratch_shapes=[pltpu.VMEM((tq, LANE), F32), pltpu.VMEM((tq, 2 * LANE), F32)]
        + _score_stage(tq, tq, tk),
        compiler_params=_cparams(("parallel", "parallel", "arbitrary")),
    )(q, kv, kr, kv)


def _spatial_gate_kernel(u_ref, v_ref, w_ref, b_ref, o_ref, *, n_chunks):
    gw = u_ref.shape[1] // C_GROUPS
    row = lax.broadcasted_iota(jnp.int32, (C_BLOCK, C_BLOCK), 0)
    col = lax.broadcasted_iota(jnp.int32, (C_BLOCK, C_BLOCK), 1)
    for g in range(C_GROUPS):
        w = jnp.where(row >= col, w_ref[g], jnp.zeros((C_BLOCK, C_BLOCK), BF16))
        bias = b_ref[g]
        for c in range(n_chunks):
            rows = slice(c * C_BLOCK, (c + 1) * C_BLOCK)
            cols = slice(g * gw, (g + 1) * gw)
            vv = v_ref[rows, cols].astype(BF16)
            sg = jnp.dot(w, vv, preferred_element_type=F32) + bias
            o_ref[rows, cols] = (u_ref[rows, cols].astype(F32) * sg).astype(o_ref.dtype)


def _spatial_gate(u, v, w_sel, b_sel, n_prompt_rows):
    m, width = u.shape
    tm = _pick(m, (1024, 512, 256, 128))
    tm = math.gcd(tm, n_prompt_rows)
    npt = n_prompt_rows // tm
    sel = lambda i: jnp.where(i >= npt, 1, 0)
    kern = functools.partial(_spatial_gate_kernel, n_chunks=tm // C_BLOCK)
    return pl.pallas_call(
        kern, grid=(m // tm,),
        in_specs=[
            pl.BlockSpec((tm, width), lambda i: (i, 0)),
            pl.BlockSpec((tm, width), lambda i: (i, 0)),
            pl.BlockSpec((None, C_GROUPS, C_BLOCK, C_BLOCK), lambda i: (sel(i), 0, 0, 0)),
            pl.BlockSpec((None, C_GROUPS, C_BLOCK, 1), lambda i: (sel(i), 0, 0, 0)),
        ],
        out_specs=pl.BlockSpec((tm, width), lambda i: (i, 0)),
        out_shape=jax.ShapeDtypeStruct((m, width), BF16),
        compiler_params=_cparams(("parallel",)),
    )(u, v, w_sel, b_sel)


def _pool_kernel(x_ref, prev_ref, hist_ref, w_ref, sc_ref, o_ref, xp_sc, *, tm, past):
    i = pl.program_id(1)
    xp_sc[0:D_HALO, :] = jnp.where(i == 0, hist_ref[...], prev_ref[...])
    xp_sc[D_HALO:D_HALO + tm, :] = x_ref[...]
    gw = x_ref.shape[1] // len(D_WINDOWS)
    pos = past + i * tm + lax.broadcasted_iota(jnp.int32, (tm, 1), 0)
    for gi, win in enumerate(D_WINDOWS):
        cols = slice(gi * gw, (gi + 1) * gw)
        tot = xp_sc[D_HALO:D_HALO + tm, cols]
        for j in range(1, win):
            tot = tot + xp_sc[D_HALO - j:D_HALO - j + tm, cols]
        cnt = jnp.minimum(pos + 1, win).astype(F32)
        pooled = tot / cnt - xp_sc[D_HALO:D_HALO + tm, cols]
        y = jnp.dot(pooled.astype(BF16), w_ref[gi], preferred_element_type=F32)
        o_ref[:, cols] = (y * sc_ref[:, cols]).astype(o_ref.dtype)


def _pool(d_in, hist, w_grp, scale, *, nb, lq, row0, past):
    width = d_in.shape[1]
    tm = _pick(lq, (512, 256, 128, 64))
    nt = lq // tm
    rb0 = row0 // tm
    hb0 = row0 // D_HALO
    per = tm // D_HALO
    kern = functools.partial(_pool_kernel, tm=tm, past=past)
    return pl.pallas_call(
        kern, grid=(nb, nt),
        in_specs=[
            pl.BlockSpec((tm, width), lambda b, i: (rb0 + b * nt + i, 0)),
            pl.BlockSpec((D_HALO, width),
                         lambda b, i: (jnp.maximum(hb0 + (b * nt + i) * per - 1, 0), 0)),
            pl.BlockSpec((None, D_HALO, width), lambda b, i: (b, 0, 0)),
            pl.BlockSpec(w_grp.shape, lambda b, i: (0, 0, 0)),
            pl.BlockSpec((1, width), lambda b, i: (0, 0)),
        ],
        out_specs=pl.BlockSpec((tm, width), lambda b, i: (b * nt + i, 0)),
        out_shape=jax.ShapeDtypeStruct((nb * lq, width), BF16),
        scratch_shapes=[pltpu.VMEM((D_HALO + tm, width), F32)],
        compiler_params=_cparams(("parallel", "arbitrary")),
    )(d_in, d_in, hist, w_grp, scale)


def _merge_kernel(x_ref, wg0, wg1, wg2, wg3, b0, b1, b2, b3, wb_ref, o_ref):
    x = x_ref[...]
    y = None
    for n, (wg, br) in enumerate(((wg0, b0), (wg1, b1), (wg2, b2), (wg3, b3))):
        gate = jax.nn.sigmoid(jnp.dot(x, wg[...], preferred_element_type=F32))
        t = gate * jnp.dot(br[...], wb_ref[n], preferred_element_type=F32)
        y = t if y is None else y + t
    o_ref[...] = y.astype(o_ref.dtype)


def _merge(xb, w_gate, branches, w_branch):
    m, d = xb.shape
    bw = branches[0].shape[1]
    tm = _pick(m, (512, 256, 128))
    tn = 512
    nj = d // tn
    in_specs = [pl.BlockSpec((tm, d), lambda i, j: (i, 0))]
    in_specs += [pl.BlockSpec((d, tn), lambda i, j, n=n: (0, n * nj + j)) for n in range(N_BRANCH)]
    in_specs += [pl.BlockSpec((tm, bw), lambda i, j: (i, 0)) for _ in range(N_BRANCH)]
    in_specs += [pl.BlockSpec((N_BRANCH, bw, tn), lambda i, j: (0, 0, j))]
    return pl.pallas_call(
        _merge_kernel, grid=(m // tm, nj), in_specs=in_specs,
        out_specs=pl.BlockSpec((tm, tn), lambda i, j: (i, j)),
        out_shape=jax.ShapeDtypeStruct((m, d), BF16),
        compiler_params=_cparams(("parallel", "arbitrary")),
    )(xb, w_gate, w_gate, w_gate, w_gate, *branches, w_branch)


def _out_ln_kernel(y_ref, x_ref, w_ref, g_ref, b_ref, o_ref, ob_ref, *, alpha):
    m = jnp.dot(y_ref[...], w_ref[...], preferred_element_type=F32)
    z = _ln_rows(alpha * x_ref[...] + m, g_ref[...], b_ref[...])
    o_ref[...] = z
    ob_ref[...] = z.astype(ob_ref.dtype)


def _out_ln(y, x, w_out, g, b, alpha):
    m, d = x.shape
    tm = _pick(m, (256, 128))
    row = lambda i: (i, 0)
    fixed = lambda i: (0, 0)
    return pl.pallas_call(
        functools.partial(_out_ln_kernel, alpha=alpha), grid=(m // tm,),
        in_specs=[pl.BlockSpec((tm, d), row), pl.BlockSpec((tm, d), row), pl.BlockSpec((d, d), fixed),
                  pl.BlockSpec((1, d), fixed), pl.BlockSpec((1, d), fixed)],
        out_specs=[pl.BlockSpec((tm, d), row), pl.BlockSpec((tm, d), row)],
        out_shape=[jax.ShapeDtypeStruct((m, d), F32), jax.ShapeDtypeStruct((m, d), BF16)],
        compiler_params=_cparams(("parallel",)),
    )(y, x, w_out, g, b)


def _ffn_kernel(*refs, alpha, gated):
    if gated:
        xb_ref, x_ref, wg_ref, wu_ref, wd_ref, gates_ref, g_ref, b_ref, o_ref, ob_ref, acc_sc = refs
    else:
        xb_ref, x_ref, wg_ref, wu_ref, wd_ref, g_ref, b_ref, o_ref, ob_ref, acc_sc = refs
    e = pl.program_id(1)
    f = pl.program_id(2)

    @pl.when((e == 0) & (f == 0))
    def _():
        acc_sc[...] = jnp.zeros(acc_sc.shape, F32)

    xb = xb_ref[...]
    hg = jnp.dot(xb, wg_ref[...], preferred_element_type=F32)
    hu = jnp.dot(xb, wu_ref[...], preferred_element_type=F32)
    h = hg * jax.nn.sigmoid(hg) * hu
    if gated:
        gates = gates_ref[...]
        lane = lax.broadcasted_iota(jnp.int32, gates.shape, 1)
        h = h * jnp.sum(jnp.where(lane == e, gates, 0.0), axis=1, keepdims=True)
    acc_sc[...] += jnp.dot(h.astype(BF16), wd_ref[...], preferred_element_type=F32)

    @pl.when((e == pl.num_programs(1) - 1) & (f == pl.num_programs(2) - 1))
    def _():
        z = _ln_rows(alpha * x_ref[...] + acc_sc[...], g_ref[...], b_ref[...])
        o_ref[...] = z
        ob_ref[...] = z.astype(ob_ref.dtype)


def _ffn(xb, x, wg, wu, wd, g, b, alpha, gates=None):
    m, d = x.shape
    ne, _, ff = wg.shape
    tm = _pick(m, (512, 256, 128))
    tf = _pick(ff, (512, 256, 128))
    row = lambda i, e, f: (i, 0)
    fixed = lambda i, e, f: (0, 0)
    in_specs = [pl.BlockSpec((tm, d), row), pl.BlockSpec((tm, d), row),
                pl.BlockSpec((None, d, tf), lambda i, e, f: (e, 0, f)),
                pl.BlockSpec((None, d, tf), lambda i, e, f: (e, 0, f)),
                pl.BlockSpec((None, tf, d), lambda i, e, f: (e, f, 0))]
    args = [xb, x, wg, wu, wd]
    if gates is not None:
        in_specs.append(pl.BlockSpec((tm, LANE), row))
        args.append(gates)
    in_specs += [pl.BlockSpec((1, d), fixed), pl.BlockSpec((1, d), fixed)]
    args += [g, b]
    return pl.pallas_call(
        functools.partial(_ffn_kernel, alpha=alpha, gated=gates is not None),
        grid=(m // tm, ne, ff // tf), in_specs=in_specs,
        out_specs=[pl.BlockSpec((tm, d), row), pl.BlockSpec((tm, d), row)],
        out_shape=[jax.ShapeDtypeStruct((m, d), F32), jax.ShapeDtypeStruct((m, d), BF16)],
        scratch_shapes=[pltpu.VMEM((tm, d), F32)],
        compiler_params=_cparams(("parallel", "arbitrary", "arbitrary")),
    )(*args)


def _router_kernel(x_ref, w_ref, o_ref, *, n_experts):
    logits = jnp.dot(x_ref[...], w_ref[...], preferred_element_type=F32,
                     precision=lax.Precision.HIGHEST)
    lane = lax.broadcasted_iota(jnp.int32, logits.shape, 1).astype(F32)
    lg = jnp.where(lane < n_experts, logits, NEG)
    m1 = jnp.max(lg, axis=1, keepdims=True)
    i1 = jnp.min(jnp.where(lg == m1, lane, float(LANE)), axis=1, keepdims=True)
    lg2 = jnp.where(lane == i1, NEG, lg)
    m2 = jnp.max(lg2, axis=1, keepdims=True)
    i2 = jnp.min(jnp.where(lg2 == m2, lane, float(LANE)), axis=1, keepdims=True)
    ex = jnp.exp(m2 - m1)
    p1 = 1.0 / (1.0 + ex)
    p2 = ex / (1.0 + ex)
    o_ref[...] = jnp.where(lane == i1, p1, 0.0) + jnp.where(lane == i2, p2, 0.0)


def _router(x, w_router):
    m, d = x.shape
    ne = w_router.shape[1]
    w = jnp.pad(w_router, ((0, 0), (0, LANE - ne)))
    tm = _pick(m, (512, 256, 128))
    return pl.pallas_call(
        functools.partial(_router_kernel, n_experts=ne), grid=(m // tm,),
        in_specs=[pl.BlockSpec((tm, d), lambda i: (i, 0)), pl.BlockSpec((d, LANE), lambda i: (0, 0))],
        out_specs=pl.BlockSpec((tm, LANE), lambda i: (i, 0)),
        out_shape=jax.ShapeDtypeStruct((m, LANE), F32),
        compiler_params=_cparams(("parallel",)),
    )(x, w)


def _epi_scale(scale):
    return lambda accs, tabs, vecs: [accs[0] * scale]


def _epi_f32_and_bf16(accs, tabs, vecs):
    return [accs[0], accs[0]]


def _epi_rms(accs, tabs, vecs):
    r = _rms_rows(accs[0], vecs[0])
    return [r, r]


def _epi_rope(accs, tabs, vecs):
    return [accs[0] * tabs[0] + accs[1] * tabs[1]]


def _epi_gelu(accs, tabs, vecs):
    return [jax.nn.gelu(accs[0])]


def _epi_gelu_ln(accs, tabs, vecs):
    return [_ln_rows(jax.nn.gelu(accs[0]), vecs[0], vecs[1])]


def _epi_plain(accs, tabs, vecs):
    return [accs[0]]


def _epi_q_rope(scale):
    return lambda accs, tabs, vecs: [(accs[0] * tabs[0] + accs[1] * tabs[1]) * scale]


def _rope_tables(pos):
    half = B_ROPE // 2
    inv = ROPE_THETA ** (-jnp.arange(half, dtype=F32) / half)
    ang = pos.astype(F32)[:, None] * inv[None, :]
    cos, sin = jnp.cos(ang), jnp.sin(ang)
    cos2 = jnp.concatenate([cos, cos], axis=1)
    sin2 = jnp.concatenate([-sin, sin], axis=1)
    return cos2, sin2


def _swap_halves(w):
    half = w.shape[-1] // 2
    return jnp.concatenate([w[..., half:], w[..., :half]], axis=-1)


def kernel(x_prompt, x_sample, cache_a_k, cache_a_v, cache_b_latent, cache_b_krope, state_d_pool, w_in, a_lambda, a_subln, b_q_norm, b_w_uq, b_kv_norm, b_w_ukv, c_ln_g, c_ln_b, c_w_s, c_b_s, d_w_grp, d_scale, w_branch, w_out, ln1_g, ln1_b, ln2_g, ln2_b, f_w_gate, f_w_up, f_w_down, m_router, m_w_gate, m_w_up, m_w_down):
    depth = w_in.shape[0]
    nbp, lp, d_model = x_prompt.shape
    nbs, ls, _ = x_sample.shape
    past = cache_a_k.shape[2]
    lks = past + ls
    tp, ts = nbp * lp, nbs * ls
    aw = A_HEADS * 2 * A_DH
    q_rank = b_q_norm.shape[1]
    kv_rank = b_kv_norm.shape[1]
    c_width = c_ln_g.shape[1]
    d_width = d_scale.shape[1]
    alpha = (2.0 * depth) ** 0.25
    sizes = (aw, aw, aw, q_rank, kv_rank, B_ROPE, c_width, c_width, d_width) + (d_model,) * N_BRANCH
    offs = [0]
    for s in sizes:
        offs.append(offs[-1] + s)

    x = jnp.concatenate([x_prompt.reshape(tp, d_model), x_sample.reshape(ts, d_model)], axis=0)
    xb = x.astype(BF16)

    pos = jnp.concatenate([jnp.tile(jnp.arange(lp, dtype=jnp.int32), nbp),
                           jnp.tile(past + jnp.arange(ls, dtype=jnp.int32), nbs)])
    cos2, sin2 = _rope_tables(pos)
    rows = pos.shape[0]
    ones = jnp.ones((rows, B_NOPE), F32)
    zeros_n = jnp.zeros((rows, B_NOPE), F32)
    zeros_r = jnp.zeros((rows, 2 * LANE - B_NOPE - B_ROPE), F32)
    q_cos = jnp.concatenate([ones, cos2, zeros_r], axis=1)
    q_sin = jnp.concatenate([zeros_n, sin2, zeros_r], axis=1)

    slopes = jnp.exp2(-8.0 * jnp.arange(1, A_HEADS + 1, dtype=F32) / A_HEADS) * LOG2E
    slopes = jnp.broadcast_to(slopes[:, None, None], (A_HEADS, 1, LANE))

    states_p, states_s = [], []
    for l in range(depth):
        wl = w_in[l]
        seg = lambda k: wl[:, offs[k]:offs[k + 1]].astype(BF16)
        lam_init = 0.8 - 0.6 * math.exp(-0.3 * l)

        (q_a,) = _fused_mm(_epi_scale(A_DH ** -0.5 * LOG2E), xb, [seg(0)], [BF16], tn=1024)
        w_kv = jnp.concatenate([seg(1), seg(2)], axis=1)
        kv_a, kv_a_b = _fused_mm(_epi_f32_and_bf16, xb, [w_kv], [F32, BF16], tn=1024)
        subln = a_subln[l].reshape(1, 2 * A_DH)
        o_a_p = _diff_attn(q_a, kv_a_b, slopes, a_lambda[l], subln, nb=nbp, lq=lp, lk=lp, q_off=0,
                           q_row0=0, lam_init=lam_init)
        cache_kv = jnp.concatenate([cache_a_k[l].reshape(nbs, past, aw),
                                    cache_a_v[l].reshape(nbs, past, aw)], axis=2).astype(BF16)
        kv_s = jnp.concatenate([cache_kv, kv_a_b[tp:].reshape(nbs, ls, 2 * aw)], axis=1)
        o_a_s = _diff_attn(q_a, kv_s.reshape(nbs * lks, 2 * aw), slopes, a_lambda[l], subln, nb=nbs,
                           lq=ls, lk=lks, q_off=past, q_row0=tp, lam_init=lam_init)
        o_a = jnp.concatenate([o_a_p, o_a_s], axis=0)

        (cq,) = _fused_mm(_epi_rms, xb, [seg(3)], [BF16], tn=q_rank,
                          vecs=[b_q_norm[l].reshape(1, q_rank)])
        lat, lat_b = _fused_mm(_epi_rms, xb, [seg(4)], [F32, BF16], tn=kv_rank,
                               vecs=[b_kv_norm[l].reshape(1, kv_rank)])
        w_kr = wl[:, offs[5]:offs[6]]
        (kr,) = _fused_mm(_epi_rope, xb, [w_kr.astype(BF16), _swap_halves(w_kr).astype(BF16)], [F32],
                          tn=B_ROPE, tabs=[cos2, sin2])
        w_uq = b_w_uq[l].reshape(q_rank, B_HEADS, B_NOPE + B_ROPE)
        w_uq_n, w_uq_r = w_uq[..., :B_NOPE], w_uq[..., B_NOPE:]
        zpad = jnp.zeros((q_rank, B_HEADS, 2 * LANE - B_NOPE - B_ROPE), F32)
        w_q1 = jnp.concatenate([w_uq_n, w_uq_r, zpad], axis=-1).reshape(q_rank, -1).astype(BF16)
        w_q2 = jnp.concatenate([jnp.zeros_like(w_uq_n), _swap_halves(w_uq_r), zpad],
                               axis=-1).reshape(q_rank, -1).astype(BF16)
        (q_b,) = _fused_mm(_epi_q_rope((B_NOPE + B_ROPE) ** -0.5 * LOG2E), cq, [w_q1, w_q2], [BF16],
                           tn=2 * LANE, tabs=[q_cos, q_sin])
        w_ukv = b_w_ukv[l].reshape(kv_rank, B_HEADS, B_NOPE + B_VDIM)
        w_up = jnp.concatenate([w_ukv[..., :B_NOPE].reshape(kv_rank, -1),
                                w_ukv[..., B_NOPE:].reshape(kv_rank, -1)], axis=1).astype(BF16)
        lat_s = jnp.concatenate([cache_b_latent[l].astype(BF16), lat_b[tp:].reshape(nbs, ls, kv_rank)],
                                axis=1).reshape(nbs * lks, kv_rank)
        lat_cat = jnp.concatenate([lat_b[:tp], lat_s], axis=0)
        (kv_b,) = _fused_mm(_epi_plain, lat_cat, [w_up], [BF16], tn=1024)
        kr_s = jnp.concatenate([cache_b_krope[l], kr[tp:].reshape(nbs, ls, B_ROPE)],
                               axis=1).reshape(nbs * lks, B_ROPE)
        kr_cat = jnp.pad(jnp.concatenate([kr[:tp], kr_s], axis=0),
                         ((0, 0), (0, LANE - B_ROPE))).astype(BF16)
        o_b_p = _mla_attn(q_b, kv_b, kr_cat, nb=nbp, lq=lp, lk=lp, q_off=0, q_row0=0, kv_row0=0)
        if tp % lks:
            kv_b_s, kr_b_s, s_row0 = kv_b[tp:], kr_cat[tp:], 0
        else:
            kv_b_s, kr_b_s, s_row0 = kv_b, kr_cat, tp
        o_b_s = _mla_attn(q_b, kv_b_s, kr_b_s, nb=nbs, lq=ls, lk=lks, q_off=past, q_row0=tp,
                          kv_row0=s_row0)
        o_b = jnp.concatenate([o_b_p, o_b_s], axis=0)

        (u_c,) = _fused_mm(_epi_gelu, xb, [seg(6)], [BF16], tn=1024)
        (v_c,) = _fused_mm(_epi_gelu_ln, xb, [seg(7)], [F32], tn=c_width,
                           vecs=[c_ln_g[l].reshape(1, c_width), c_ln_b[l].reshape(1, c_width)])
        w_s, b_s = c_w_s[l], c_b_s[l]
        reps = C_BLOCK // ls if ls < C_BLOCK else 1
        w_tl = w_s[:, :C_BLOCK // reps, :C_BLOCK // reps]
        w_smp = jnp.zeros_like(w_s)
        for r in range(reps):
            sl = slice(r * (C_BLOCK // reps), (r + 1) * (C_BLOCK // reps))
            w_smp = w_smp.at[:, sl, sl].set(w_tl)
        b_smp = jnp.tile(b_s[:, :C_BLOCK // reps], (1, reps))
        w_sel = jnp.stack([w_s, w_smp]).astype(BF16)
        b_sel = jnp.stack([b_s, b_smp])[..., None]
        o_c = _spatial_gate(u_c, v_c, w_sel, b_sel, tp)

        (d_in,) = _fused_mm(_epi_plain, xb, [seg(8)], [F32], tn=1024)
        w_grp = d_w_grp[l].astype(BF16)
        scale = d_scale[l].reshape(1, d_width)
        hist_p = jnp.zeros((nbp, D_HALO, d_width), F32)
        hist_s = jnp.pad(state_d_pool[l], ((0, 0), (D_HALO - D_HIST, 0), (0, 0)))
        o_d_p = _pool(d_in, hist_p, w_grp, scale, nb=nbp, lq=lp, row0=0, past=0)
        o_d_s = _pool(d_in, hist_s, w_grp, scale, nb=nbs, lq=ls, row0=tp, past=past)
        o_d = jnp.concatenate([o_d_p, o_d_s], axis=0)

        w_gate = wl[:, offs[9]:offs[13]].astype(BF16)
        y = _merge(xb, w_gate, [o_a, o_b, o_c, o_d], w_branch[l].astype(BF16))
        x, xb = _out_ln(y, x, w_out[l].astype(BF16), ln1_g[l].reshape(1, -1), ln1_b[l].reshape(1, -1),
                        alpha)

        i = l // 2
        g2, b2 = ln2_g[l].reshape(1, -1), ln2_b[l].reshape(1, -1)
        if l % 2 == 0:
            x, xb = _ffn(xb, x, f_w_gate[i][None].astype(BF16), f_w_up[i][None].astype(BF16),
                         f_w_down[i][None].astype(BF16), g2, b2, alpha)
        else:
            gates = _router(x, m_router[i])
            x, xb = _ffn(xb, x, m_w_gate[i].astype(BF16), m_w_up[i].astype(BF16),
                         m_w_down[i].astype(BF16), g2, b2, alpha, gates=gates)

        kv_p = kv_a[:tp].reshape(nbp, lp, 2, A_HEADS, 2 * A_DH)
        kv_smp = kv_a[tp:].reshape(nbs, ls, 2, A_HEADS, 2 * A_DH)
        states_p.append((kv_p[:, :, 0], kv_p[:, :, 1], lat[:tp].reshape(nbp, lp, kv_rank),
                         kr[:tp].reshape(nbp, lp, B_ROPE),
                         d_in[:tp].reshape(nbp, lp, d_width)[:, lp - D_HIST:]))
        states_s.append((kv_smp[:, :, 0], kv_smp[:, :, 1], lat[tp:].reshape(nbs, ls, kv_rank),
                         kr[tp:].reshape(nbs, ls, B_ROPE), v_c[tp:].reshape(nbs, ls, c_width),
                         jnp.concatenate([state_d_pool[l], d_in[tp:].reshape(nbs, ls, d_width)],
                                         axis=1)[:, -D_HIST:]))

    yp = x[:tp].reshape(nbp, lp, d_model)
    ys = x[tp:].reshape(nbs, ls, d_model)
    stack = lambda sts, k: jnp.stack([s[k] for s in sts])
    return (yp, ys, stack(states_p, 0), stack(states_p, 1), stack(states_p, 2), stack(states_p, 3),
            stack(states_p, 4), stack(states_s, 0), stack(states_s, 1), stack(states_s, 2),
            stack(states_s, 3), stack(states_s, 4), stack(states_s, 5))
```

```python
import functools
import math

import jax
import jax.numpy as jnp
from jax import lax
from jax.experimental import pallas as pl
from jax.experimental.pallas import tpu as pltpu

CHUNK = 64
A_HEADS = 8
A_DH = 64
B_HEADS = 8
B_NOPE = 128
B_ROPE = 64
B_VDIM = 128
C_GROUPS = 4
C_BLOCK = 128
D_WINDOWS = (2, 4, 8, 16)
D_HIST = 15
D_HALO = 16
N_BRANCH = 4
ROPE_THETA = 10000.0
LN_EPS = 1e-5
RMS_EPS = 1e-6
LOG2E = 1.4426950408889634
NEG = -1e30
LANE = 128
VMEM_LIMIT = 56 * 1024 * 1024

F32 = jnp.float32
BF16 = jnp.bfloat16


def _pick(n, cands):
    for c in cands:
        if n % c == 0:
            return c
    raise ValueError(f"no tile for {n} in {cands}")


def _cparams(sem):
    return pltpu.CompilerParams(dimension_semantics=sem, vmem_limit_bytes=VMEM_LIMIT)


def _ln_rows(z, g, b):
    mu = jnp.mean(z, axis=-1, keepdims=True)
    zc = z - mu
    var = jnp.mean(zc * zc, axis=-1, keepdims=True)
    return zc * lax.rsqrt(var + LN_EPS) * g + b


def _rms_rows(z, g):
    return z * lax.rsqrt(jnp.mean(z * z, axis=-1, keepdims=True) + RMS_EPS) * g


def _fused_mm_kernel(*refs, fn, nw, ntab, nvec, precision):
    x = refs[0][...]
    w_refs = refs[1:1 + nw]
    tab_refs = refs[1 + nw:1 + nw + ntab]
    vec_refs = refs[1 + nw + ntab:1 + nw + ntab + nvec]
    out_refs = refs[1 + nw + ntab + nvec:]
    accs = [jnp.dot(x, w[...], preferred_element_type=F32, precision=precision) for w in w_refs]
    res = fn(accs, [t[...] for t in tab_refs], [v[...] for v in vec_refs])
    for o, r in zip(out_refs, res):
        o[...] = r.astype(o.dtype)


def _fused_mm(fn, x, ws, out_dtypes, *, tn, tm=None, tabs=(), tabs_follow_cols=False, vecs=(),
              precision=None):
    m, kdim = x.shape
    n = ws[0].shape[1]
    tm = tm or _pick(m, (1024, 512, 256, 128))
    grid = (m // tm, n // tn)
    in_specs = [pl.BlockSpec((tm, kdim), lambda i, j: (i, 0))]
    in_specs += [pl.BlockSpec((kdim, tn), lambda i, j: (0, j)) for _ in ws]
    for t in tabs:
        if tabs_follow_cols:
            in_specs.append(pl.BlockSpec((tm, tn), lambda i, j: (i, j)))
        else:
            in_specs.append(pl.BlockSpec((tm, t.shape[1]), lambda i, j: (i, 0)))
    in_specs += [pl.BlockSpec((1, tn), lambda i, j: (0, j)) for _ in vecs]
    out_specs = [pl.BlockSpec((tm, tn), lambda i, j: (i, j)) for _ in out_dtypes]
    out_shape = [jax.ShapeDtypeStruct((m, n), d) for d in out_dtypes]
    kern = functools.partial(_fused_mm_kernel, fn=fn, nw=len(ws), ntab=len(tabs), nvec=len(vecs),
                             precision=precision)
    return pl.pallas_call(
        kern, grid=grid, in_specs=in_specs, out_specs=out_specs, out_shape=out_shape,
        compiler_params=_cparams(("parallel", "arbitrary")),
    )(x, *ws, *tabs, *vecs)


ATTN_ROW_GROUPS = 2


def _row_groups(rows):
    g = ATTN_ROW_GROUPS
    while rows % (16 * g):
        g //= 2
    return g


def _nt_dot(a, b):
    return lax.dot_general(a, b, (((1,), (1,)), ((), ())), preferred_element_type=F32)


def _softmax_pv(scores, v, biases, m_sc, acc_sc):
    width = v.shape[0]
    v_ext = jnp.concatenate([v, jnp.ones(v.shape, v.dtype)], axis=1)
    r0 = 0
    for s, bias in zip(scores, biases):
        rows = slice(r0, r0 + s.shape[0])
        r0 += s.shape[0]
        if bias is not None:
            s = s + bias
        m_prev = m_sc[rows, :]
        m_new = jnp.maximum(m_prev, jnp.max(s, axis=1, keepdims=True))
        alpha = jnp.exp2(m_prev - m_new)
        m_rep = jnp.tile(m_new, (1, width // LANE)) if width >= LANE else m_new[:, :width]
        p = jnp.exp2(s - m_rep)
        pv = jnp.dot(p.astype(BF16), v_ext, preferred_element_type=F32)
        acc_sc[rows, :] = jnp.tile(alpha, (1, 2)) * acc_sc[rows, :] + pv
        m_sc[rows, :] = m_new


def _sweep_keys(q_groups, keys, v_ref, off_biases, diag_biases, m_sc, acc_sc, s_sc, *, q0, tq, tk):
    m_sc[...] = jnp.full(m_sc.shape, NEG, F32)
    acc_sc[...] = jnp.zeros(acc_sc.shape, F32)
    n_off = q0 // tk
    kd = pl.multiple_of(q0, tq)

    def scores(start, size):
        k = keys(start, size)
        return [_nt_dot(q, k) for q in q_groups()]

    if s_sc is None:
        def body(j, carry):
            ks = pl.multiple_of(j * tk, tk)
            _softmax_pv(scores(ks, tk), v_ref[pl.ds(ks, tk), :], off_biases(ks), m_sc, acc_sc)
            return carry

        lax.fori_loop(0, n_off, body, 0)
        _softmax_pv(scores(kd, tq), v_ref[pl.ds(kd, tq), :], diag_biases, m_sc, acc_sc)
        return

    assert tq == tk
    bounds = []
    r0 = 0
    for q in q_groups():
        bounds.append(slice(r0, r0 + q.shape[0]))
        r0 += q.shape[0]

    def stash(new_scores):
        for rows, s in zip(bounds, new_scores):
            s_sc[rows, :] = s

    def staged():
        return [s_sc[rows, :] for rows in bounds]

    stash(scores(0, tk))

    def body(j, carry):
        ks = pl.multiple_of(j * tk, tk)
        nxt = scores(pl.multiple_of(ks + tk, tk), tk)
        _softmax_pv(staged(), v_ref[pl.ds(ks, tk), :], off_biases(ks), m_sc, acc_sc)
        stash(nxt)
        return carry

    lax.fori_loop(0, n_off, body, 0)
    _softmax_pv(staged(), v_ref[pl.ds(kd, tq), :], diag_biases, m_sc, acc_sc)


def _chunk_mask_bias(q0, tq, slope):
    qpos = q0 + lax.broadcasted_iota(jnp.int32, (tq, tq), 0)
    kpos = q0 + lax.broadcasted_iota(jnp.int32, (tq, tq), 1)
    visible = (kpos >> 6) <= (qpos >> 6)
    if slope is None:
        return jnp.where(visible, 0.0, NEG)
    rel = (qpos - q0 - jnp.abs(qpos - kpos)).astype(F32)
    return jnp.where(visible, slope * rel, NEG)


def _diff_attn_kernel(q_ref, k_ref, v_ref, slope_ref, lam_ref, g_ref, o_ref,
                      qs_sc, m_sc, acc_sc, *maybe_s_sc, tq, tk, q_off, lam_init):
    qi = pl.program_id(2)
    q0 = q_off + qi * tq
    q = q_ref[...]
    lane = lax.broadcasted_iota(jnp.int32, q.shape, 1)
    zero = jnp.zeros_like(q)
    qs_sc[0:tq, :] = jnp.where(lane < A_DH, q, zero)
    qs_sc[tq:2 * tq, :] = jnp.where(lane >= A_DH, q, zero)
    slope = slope_ref[...][:, 0:1]
    groups = _row_groups(2 * tq)
    gr = 2 * tq // groups

    def q_groups():
        return [qs_sc[g * gr:(g + 1) * gr, :] for g in range(groups)]

    def off_biases(ks):
        kpos = ks + lax.broadcasted_iota(jnp.int32, (1, tk), 1)
        return [slope * (kpos - q0).astype(F32)] * groups

    bias = _chunk_mask_bias(q0, tq, slope)
    diag_biases = [bias[(g * gr) % tq:(g * gr) % tq + gr, :] for g in range(groups)]
    _sweep_keys(q_groups, lambda ks, size: k_ref[pl.ds(ks, size), :], v_ref, off_biases, diag_biases,
                m_sc, acc_sc, maybe_s_sc[0] if maybe_s_sc else None, q0=q0, tq=tq, tk=tk)

    lp = lam_ref[...]
    lam = (jnp.exp(jnp.sum(lp[0:1, :] * lp[1:2, :], axis=1, keepdims=True))
           - jnp.exp(jnp.sum(lp[2:3, :] * lp[3:4, :], axis=1, keepdims=True)) + lam_init)
    acc = acc_sc[...]
    o1 = acc[0:tq, 0:LANE] / acc[0:tq, LANE:2 * LANE]
    o2 = acc[tq:2 * tq, 0:LANE] / acc[tq:2 * tq, LANE:2 * LANE]
    o = _rms_rows(o1 - lam * o2, g_ref[...]) * (1.0 - lam_init)
    o_ref[...] = o.astype(o_ref.dtype)


def _mla_attn_kernel(q_ref, kn_ref, kr_ref, v_ref, o_ref, m_sc, acc_sc, *maybe_s_sc,
                     tq, tk, q_off):
    qi = pl.program_id(2)
    q0 = q_off + qi * tq
    groups = _row_groups(tq)
    gr = tq // groups

    def q_groups():
        return [q_ref[g * gr:(g + 1) * gr, :] for g in range(groups)]

    def keys(ks, size):
        return jnp.concatenate([kn_ref[pl.ds(ks, size), :], kr_ref[pl.ds(ks, size), :]], axis=1)

    mask = _chunk_mask_bias(q0, tq, None)
    diag_biases = [mask[g * gr:(g + 1) * gr, :] for g in range(groups)]
    _sweep_keys(q_groups, keys, v_ref, lambda ks: [None] * groups, diag_biases, m_sc, acc_sc,
                maybe_s_sc[0] if maybe_s_sc else None, q0=q0, tq=tq, tk=tk)
    acc = acc_sc[...]
    o_ref[...] = (acc[:, 0:LANE] / acc[:, LANE:2 * LANE]).astype(o_ref.dtype)


def _attn_tiles(lq, q_off):
    tq = _pick(lq, (512, 256, 128, 64))
    tk = tq if q_off == 0 else _pick(q_off, (512, 256, 128, 64))
    return tq, tk


def _score_stage(rows, tq, tk):
    return [pltpu.VMEM((rows, tk), F32)] if tq == tk else []


def _diff_attn(q, kv, slopes, lam_p, subln, *, nb, lq, lk, q_off, q_row0, lam_init):
    tq, tk = _attn_tiles(lq, q_off)
    nq = lq // tq
    qb0 = q_row0 // tq
    kern = functools.partial(_diff_attn_kernel, tq=tq, tk=tk, q_off=q_off, lam_init=lam_init)
    return pl.pallas_call(
        kern, grid=(nb, A_HEADS, nq),
        in_specs=[
            pl.BlockSpec((tq, LANE), lambda b, h, i: (qb0 + b * nq + i, h)),
            pl.BlockSpec((lk, LANE), lambda b, h, i: (b, h)),
            pl.BlockSpec((lk, LANE), lambda b, h, i: (b, A_HEADS + h)),
            pl.BlockSpec((None, 1, LANE), lambda b, h, i: (h, 0, 0)),
            pl.BlockSpec((4, A_DH), lambda b, h, i: (0, 0)),
            pl.BlockSpec((1, LANE), lambda b, h, i: (0, 0)),
        ],
        out_specs=pl.BlockSpec((tq, LANE), lambda b, h, i: (b * nq + i, h)),
        out_shape=jax.ShapeDtypeStruct((nb * lq, A_HEADS * LANE), BF16),
        scratch_shapes=[pltpu.VMEM((2 * tq, LANE), BF16), pltpu.VMEM((2 * tq, LANE), F32),
                        pltpu.VMEM((2 * tq, 2 * LANE), F32)] + _score_stage(2 * tq, tq, tk),
        compiler_params=_cparams(("parallel", "parallel", "arbitrary")),
    )(q, kv, kv, slopes, lam_p, subln)


def _mla_attn(q, kv, kr, *, nb, lq, lk, q_off, q_row0, kv_row0):
    tq, tk = _attn_tiles(lq, q_off)
    nq = lq // tq
    qb0 = q_row0 // tq
    kb0 = kv_row0 // lk
    kern = functools.partial(_mla_attn_kernel, tq=tq, tk=tk, q_off=q_off)
    return pl.pallas_call(
        kern, grid=(nb, B_HEADS, nq),
        in_specs=[
            pl.BlockSpec((tq, 2 * LANE), lambda b, h, i: (qb0 + b * nq + i, h)),
            pl.BlockSpec((lk, LANE), lambda b, h, i: (kb0 + b, h)),
            pl.BlockSpec((lk, LANE), lambda b, h, i: (kb0 + b, 0)),
            pl.BlockSpec((lk, LANE), lambda b, h, i: (kb0 + b, B_HEADS + h)),
        ],
        out_specs=pl.BlockSpec((tq, LANE), lambda b, h, i: (b * nq + i, h)),
        out_shape=jax.ShapeDtypeStruct((nb * lq, B_HEADS * LANE), BF16),
        scratch_shapes=[pltpu.VMEM((tq, LANE), F32), pltpu.VMEM((tq, 2 * LANE), F32)]
        + _score_stage(tq, tq, tk),
        compiler_params=_cparams(("parallel", "parallel", "arbitrary")),
    )(q, kv, kr, kv)


def _spatial_gate_kernel(u_ref, v_ref, w_ref, b_ref, o_ref, *, n_chunks):
    gw = u_ref.shape[1] // C_GROUPS
    row = lax.broadcasted_iota(jnp.int32, (C_BLOCK, C_BLOCK), 0)
    col = lax.broadcasted_iota(jnp.int32, (C_BLOCK, C_BLOCK), 1)
    for g in range(C_GROUPS):
        w = jnp.where(row >= col, w_ref[g], jnp.zeros((C_BLOCK, C_BLOCK), BF16))
        bias = b_ref[g]
        for c in range(n_chunks):
            rows = slice(c * C_BLOCK, (c + 1) * C_BLOCK)
            cols = slice(g * gw, (g + 1) * gw)
            vv = v_ref[rows, cols].astype(BF16)
            sg = jnp.dot(w, vv, preferred_element_type=F32) + bias
            o_ref[rows, cols] = (u_ref[rows, cols].astype(F32) * sg).astype(o_ref.dtype)


def _spatial_gate(u, v, w_sel, b_sel, n_prompt_rows):
    m, width = u.shape
    tm = _pick(m, (1024, 512, 256, 128))
    tm = math.gcd(tm, n_prompt_rows)
    npt = n_prompt_rows // tm
    sel = lambda i: jnp.where(i >= npt, 1, 0)
    kern = functools.partial(_spatial_gate_kernel, n_chunks=tm // C_BLOCK)
    return pl.pallas_call(
        kern, grid=(m // tm,),
        in_specs=[
            pl.BlockSpec((tm, width), lambda i: (i, 0)),
            pl.BlockSpec((tm, width), lambda i: (i, 0)),
            pl.BlockSpec((None, C_GROUPS, C_BLOCK, C_BLOCK), lambda i: (sel(i), 0, 0, 0)),
            pl.BlockSpec((None, C_GROUPS, C_BLOCK, 1), lambda i: (sel(i), 0, 0, 0)),
        ],
        out_specs=pl.BlockSpec((tm, width), lambda i: (i, 0)),
        out_shape=jax.ShapeDtypeStruct((m, width), BF16),
        compiler_params=_cparams(("parallel",)),
    )(u, v, w_sel, b_sel)


def _pool_kernel(x_ref, prev_ref, hist_ref, w_ref, sc_ref, o_ref, xp_sc, *, tm, past):
    i = pl.program_id(1)
    xp_sc[0:D_HALO, :] = jnp.where(i == 0, hist_ref[...], prev_ref[...])
    xp_sc[D_HALO:D_HALO + tm, :] = x_ref[...]
    gw = x_ref.shape[1] // len(D_WINDOWS)
    pos = past + i * tm + lax.broadcasted_iota(jnp.int32, (tm, 1), 0)
    for gi, win in enumerate(D_WINDOWS):
        cols = slice(gi * gw, (gi + 1) * gw)
        tot = xp_sc[D_HALO:D_HALO + tm, cols]
        for j in range(1, win):
            tot = tot + xp_sc[D_HALO - j:D_HALO - j + tm, cols]
        cnt = jnp.minimum(pos + 1, win).astype(F32)
        pooled = tot / cnt - xp_sc[D_HALO:D_HALO + tm, cols]
        y = jnp.dot(pooled.astype(BF16), w_ref[gi], preferred_element_type=F32)
        o_ref[:, cols] = (y * sc_ref[:, cols]).astype(o_ref.dtype)


def _pool(d_in, hist, w_grp, scale, *, nb, lq, row0, past):
    width = d_in.shape[1]
    tm = _pick(lq, (512, 256, 128, 64))
    nt = lq // tm
    rb0 = row0 // tm
    hb0 = row0 // D_HALO
    per = tm // D_HALO
    kern = functools.partial(_pool_kernel, tm=tm, past=past)
    return pl.pallas_call(
        kern, grid=(nb, nt),
        in_specs=[
            pl.BlockSpec((tm, width), lambda b, i: (rb0 + b * nt + i, 0)),
            pl.BlockSpec((D_HALO, width),
                         lambda b, i: (jnp.maximum(hb0 + (b * nt + i) * per - 1, 0), 0)),
            pl.BlockSpec((None, D_HALO, width), lambda b, i: (b, 0, 0)),
            pl.BlockSpec(w_grp.shape, lambda b, i: (0, 0, 0)),
            pl.BlockSpec((1, width), lambda b, i: (0, 0)),
        ],
        out_specs=pl.BlockSpec((tm, width), lambda b, i: (b * nt + i, 0)),
        out_shape=jax.ShapeDtypeStruct((nb * lq, width), BF16),
        scratch_shapes=[pltpu.VMEM((D_HALO + tm, width), F32)],
        compiler_params=_cparams(("parallel", "arbitrary")),
    )(d_in, d_in, hist, w_grp, scale)


def _merge_kernel(x_ref, wg0, wg1, wg2, wg3, b0, b1, b2, b3, wb_ref, o_ref):
    x = x_ref[...]
    y = None
    for n, (wg, br) in enumerate(((wg0, b0), (wg1, b1), (wg2, b2), (wg3, b3))):
        gate = jax.nn.sigmoid(jnp.dot(x, wg[...], preferred_element_type=F32))
        t = gate * jnp.dot(br[...], wb_ref[n], preferred_element_type=F32)
        y = t if y is None else y + t
    o_ref[...] = y.astype(o_ref.dtype)


def _merge(xb, w_gate, branches, w_branch):
    m, d = xb.shape
    bw = branches[0].shape[1]
    tm = _pick(m, (512, 256, 128))
    tn = 512
    nj = d // tn
    in_specs = [pl.BlockSpec((tm, d), lambda i, j: (i, 0))]
    in_specs += [pl.BlockSpec((d, tn), lambda i, j, n=n: (0, n * nj + j)) for n in range(N_BRANCH)]
    in_specs += [pl.BlockSpec((tm, bw), lambda i, j: (i, 0)) for _ in range(N_BRANCH)]
    in_specs += [pl.BlockSpec((N_BRANCH, bw, tn), lambda i, j: (0, 0, j))]
    return pl.pallas_call(
        _merge_kernel, grid=(m // tm, nj), in_specs=in_specs,
        out_specs=pl.BlockSpec((tm, tn), lambda i, j: (i, j)),
        out_shape=jax.ShapeDtypeStruct((m, d), BF16),
        compiler_params=_cparams(("parallel", "arbitrary")),
    )(xb, w_gate, w_gate, w_gate, w_gate, *branches, w_branch)


def _out_ln_kernel(y_ref, x_ref, w_ref, g_ref, b_ref, o_ref, ob_ref, *, alpha):
    m = jnp.dot(y_ref[...], w_ref[...], preferred_element_type=F32)
    z = _ln_rows(alpha * x_ref[...] + m, g_ref[...], b_ref[...])
    o_ref[...] = z
    ob_ref[...] = z.astype(ob_ref.dtype)


def _out_ln(y, x, w_out, g, b, alpha):
    m, d = x.shape
    tm = _pick(m, (256, 128))
    row = lambda i: (i, 0)
    fixed = lambda i: (0, 0)
    return pl.pallas_call(
        functools.partial(_out_ln_kernel, alpha=alpha), grid=(m // tm,),
        in_specs=[pl.BlockSpec((tm, d), row), pl.BlockSpec((tm, d), row), pl.BlockSpec((d, d), fixed),
                  pl.BlockSpec((1, d), fixed), pl.BlockSpec((1, d), fixed)],
        out_specs=[pl.BlockSpec((tm, d), row), pl.BlockSpec((tm, d), row)],
        out_shape=[jax.ShapeDtypeStruct((m, d), F32), jax.ShapeDtypeStruct((m, d), BF16)],
        compiler_params=_cparams(("parallel",)),
    )(y, x, w_out, g, b)


def _ffn_kernel(xb_ref, x_ref, wg_ref, wu_ref, wd_ref, g_ref, b_ref, o_ref, ob_ref, acc_sc, *, alpha):
    f = pl.program_id(1)

    @pl.when(f == 0)
    def _():
        acc_sc[...] = jnp.zeros(acc_sc.shape, F32)

    xb = xb_ref[...]
    hg = jnp.dot(xb, wg_ref[...], preferred_element_type=F32)
    hu = jnp.dot(xb, wu_ref[...], preferred_element_type=F32)
    h = hg * jax.nn.sigmoid(hg) * hu
    acc_sc[...] += jnp.dot(h.astype(BF16), wd_ref[...], preferred_element_type=F32)

    @pl.when(f == pl.num_programs(1) - 1)
    def _():
        z = _ln_rows(alpha * x_ref[...] + acc_sc[...], g_ref[...], b_ref[...])
        o_ref[...] = z
        ob_ref[...] = z.astype(ob_ref.dtype)


def _ffn(xb, x, wg, wu, wd, g, b, alpha):
    m, d = x.shape
    ff = wg.shape[1]
    tm = _pick(m, (512, 256, 128))
    tf = _pick(ff, (512, 256, 128))
    row = lambda i, f: (i, 0)
    fixed = lambda i, f: (0, 0)
    return pl.pallas_call(
        functools.partial(_ffn_kernel, alpha=alpha), grid=(m // tm, ff // tf),
        in_specs=[pl.BlockSpec((tm, d), row), pl.BlockSpec((tm, d), row),
                  pl.BlockSpec((d, tf), lambda i, f: (0, f)), pl.BlockSpec((d, tf), lambda i, f: (0, f)),
                  pl.BlockSpec((tf, d), lambda i, f: (f, 0)),
                  pl.BlockSpec((1, d), fixed), pl.BlockSpec((1, d), fixed)],
        out_specs=[pl.BlockSpec((tm, d), row), pl.BlockSpec((tm, d), row)],
        out_shape=[jax.ShapeDtypeStruct((m, d), F32), jax.ShapeDtypeStruct((m, d), BF16)],
        scratch_shapes=[pltpu.VMEM((tm, d), F32)],
        compiler_params=_cparams(("parallel", "arbitrary")),
    )(xb, x, wg, wu, wd, g, b)


def _router_kernel(x_ref, w_ref, o_ref, sel_ref, *, n_experts):
    logits = jnp.dot(x_ref[...], w_ref[...], preferred_element_type=F32,
                     precision=lax.Precision.HIGHEST)
    lane = lax.broadcasted_iota(jnp.int32, logits.shape, 1).astype(F32)
    lg = jnp.where(lane < n_experts, logits, NEG)
    m1 = jnp.max(lg, axis=1, keepdims=True)
    i1 = jnp.min(jnp.where(lg == m1, lane, float(LANE)), axis=1, keepdims=True)
    lg2 = jnp.where(lane == i1, NEG, lg)
    m2 = jnp.max(lg2, axis=1, keepdims=True)
    i2 = jnp.min(jnp.where(lg2 == m2, lane, float(LANE)), axis=1, keepdims=True)
    ex = jnp.exp(m2 - m1)
    p1 = 1.0 / (1.0 + ex)
    p2 = ex / (1.0 + ex)
    o_ref[...] = (jnp.where(lane == 0.0, i1, 0.0) + jnp.where(lane == 1.0, i2, 0.0)
                  + jnp.where(lane == 2.0, p1, 0.0) + jnp.where(lane == 3.0, p2, 0.0))
    sel_ref[...] = jnp.where((lane == i1) | (lane == i2), 1.0, 0.0).astype(sel_ref.dtype)


def _router(x, w_router):
    m, d = x.shape
    ne = w_router.shape[1]
    w = jnp.pad(w_router, ((0, 0), (0, LANE - ne)))
    tm = _pick(m, (512, 256, 128))
    row = lambda i: (i, 0)
    return pl.pallas_call(
        functools.partial(_router_kernel, n_experts=ne), grid=(m // tm,),
        in_specs=[pl.BlockSpec((tm, d), row), pl.BlockSpec((d, LANE), lambda i: (0, 0))],
        out_specs=[pl.BlockSpec((tm, LANE), row), pl.BlockSpec((tm, LANE), row)],
        out_shape=[jax.ShapeDtypeStruct((m, LANE), F32), jax.ShapeDtypeStruct((m, LANE), BF16)],
        compiler_params=_cparams(("parallel",)),
    )(x, w)


def _slot_kernel(sel_ref, rec_ref, pos_ref, meta_ref, cnt_sc, off_sc, *, tile, n_experts):
    phase = pl.program_id(0)
    i = pl.program_id(1)
    sel = sel_ref[...]
    tm = sel.shape[0]
    lane = lax.broadcasted_iota(jnp.int32, (1, LANE), 1)

    @pl.when((phase == 0) & (i == 0))
    def _():
        cnt_sc[...] = jnp.zeros(cnt_sc.shape, F32)

    @pl.when(phase == 0)
    def _():
        cnt_sc[...] += jnp.sum(sel.astype(F32), axis=0, keepdims=True)

    @pl.when((phase == 1) & (i == 0))
    def _():
        counts = cnt_sc[...]
        padded = jnp.ceil(counts / tile) * tile
        starts = jnp.zeros((1, LANE), F32)
        for e in range(1, n_experts):
            before = jnp.sum(jnp.where(lane < e, padded, 0.0), axis=1, keepdims=True)
            starts = jnp.where(lane == e, before, starts)
        off_sc[...] = starts
        row = lax.broadcasted_iota(jnp.int32, meta_ref.shape, 0)
        meta_ref[...] = jnp.where(row == 0, counts, jnp.where(row == 1, starts, 0.0))
        cnt_sc[...] = jnp.zeros(cnt_sc.shape, F32)

    @pl.when(phase == 1)
    def _():
        r = lax.broadcasted_iota(jnp.int32, (tm, tm), 0)
        c = lax.broadcasted_iota(jnp.int32, (tm, tm), 1)
        tri = jnp.where(r >= c, 1.0, 0.0).astype(BF16)
        csum = jnp.dot(tri, sel, preferred_element_type=F32)
        slot = off_sc[...] + cnt_sc[...] + csum - sel.astype(F32)
        rec = rec_ref[...]
        lanef = lane.astype(F32)
        pos1 = jnp.sum(jnp.where(lanef == rec[:, 0:1], slot, 0.0), axis=1, keepdims=True)
        pos2 = jnp.sum(jnp.where(lanef == rec[:, 1:2], slot, 0.0), axis=1, keepdims=True)
        pos_ref[...] = jnp.where(lane == 0, pos1, 0.0) + jnp.where(lane == 1, pos2, 0.0)
        cnt_sc[...] += csum[tm - 1:tm, :]


def _slots(sel, rec, tile, n_experts):
    m = sel.shape[0]
    tm = _pick(m, (512, 256, 128))
    row = lambda p, i: (i, 0)
    return pl.pallas_call(
        functools.partial(_slot_kernel, tile=tile, n_experts=n_experts), grid=(2, m // tm),
        in_specs=[pl.BlockSpec((tm, LANE), row), pl.BlockSpec((tm, LANE), row)],
        out_specs=[pl.BlockSpec((tm, LANE), lambda p, i: (i * p, 0)),
                   pl.BlockSpec((8, LANE), lambda p, i: (0, 0))],
        out_shape=[jax.ShapeDtypeStruct((m, LANE), F32), jax.ShapeDtypeStruct((8, LANE), F32)],
        scratch_shapes=[pltpu.VMEM((1, LANE), F32), pltpu.VMEM((1, LANE), F32)],
        compiler_params=_cparams(("arbitrary", "arbitrary")),
    )(sel, rec)


def _row_copy(src, src_row, dst, dst_row, sem):
    return pltpu.make_async_copy(src.at[pl.ds(src_row, 1)], dst.at[pl.ds(dst_row, 1)], sem)


def _dispatch_kernel(pos1_ref, pos2_ref, x_hbm, zeros_hbm, xs_hbm, sem, *, tm):
    del zeros_hbm
    base = pl.program_id(0) * tm

    def issue(r, carry):
        t = base + r
        _row_copy(x_hbm, t, xs_hbm, pos1_ref[t], sem).start()
        _row_copy(x_hbm, t, xs_hbm, pos2_ref[t], sem).start()
        return carry

    lax.fori_loop(0, tm, issue, 0)

    def drain(r, carry):
        _row_copy(x_hbm, 0, xs_hbm, 0, sem).wait()
        _row_copy(x_hbm, 0, xs_hbm, 0, sem).wait()
        return carry

    lax.fori_loop(0, tm, drain, 0)


def _dispatch(x_words, pos1, pos2, n_rows):
    m, w = x_words.shape
    tm = _pick(m, (1024, 512, 256, 128))
    kern = functools.partial(_dispatch_kernel, tm=tm)
    return pl.pallas_call(
        kern,
        grid_spec=pltpu.PrefetchScalarGridSpec(
            num_scalar_prefetch=2, grid=(m // tm,),
            in_specs=[pl.BlockSpec(memory_space=pl.ANY), pl.BlockSpec(memory_space=pl.ANY)],
            out_specs=pl.BlockSpec(memory_space=pl.ANY),
            scratch_shapes=[pltpu.SemaphoreType.DMA]),
        out_shape=jax.ShapeDtypeStruct((n_rows, w), x_words.dtype),
        input_output_aliases={3: 0},
        compiler_params=_cparams(("arbitrary",)),
    )(pos1, pos2, x_words, jnp.zeros((n_rows, w), x_words.dtype))


def _grouped_ffn_kernel(tile_e_ref, n_used_ref, x_ref, wg_ref, wu_ref, wd_ref, o_ref, acc_sc):
    del tile_e_ref
    i = pl.program_id(0)
    f = pl.program_id(1)
    used = i < n_used_ref[0]

    @pl.when(f == 0)
    def _():
        acc_sc[...] = jnp.zeros(acc_sc.shape, F32)

    @pl.when(used)
    def _():
        xb = x_ref[...]
        hg = jnp.dot(xb, wg_ref[...], preferred_element_type=F32)
        hu = jnp.dot(xb, wu_ref[...], preferred_element_type=F32)
        h = hg * jax.nn.sigmoid(hg) * hu
        acc_sc[...] += jnp.dot(h.astype(BF16), wd_ref[...], preferred_element_type=F32)

    @pl.when(f == pl.num_programs(1) - 1)
    def _():
        o_ref[...] = acc_sc[...]


def _grouped_ffn(xs, tile_e, n_used, wg, wu, wd, tile):
    p, d = xs.shape
    ff = wg.shape[2]
    tf = _pick(ff, (512, 256, 128))
    row = lambda i, f, te, nu: (i, 0)
    return pl.pallas_call(
        _grouped_ffn_kernel,
        grid_spec=pltpu.PrefetchScalarGridSpec(
            num_scalar_prefetch=2, grid=(p // tile, ff // tf),
            in_specs=[pl.BlockSpec((tile, d), row),
                      pl.BlockSpec((None, d, tf), lambda i, f, te, nu: (te[i], 0, f)),
                      pl.BlockSpec((None, d, tf), lambda i, f, te, nu: (te[i], 0, f)),
                      pl.BlockSpec((None, tf, d), lambda i, f, te, nu: (te[i], f, 0))],
            out_specs=pl.BlockSpec((tile, d), row),
            scratch_shapes=[pltpu.VMEM((tile, d), F32)]),
        out_shape=jax.ShapeDtypeStruct((p, d), F32),
        compiler_params=_cparams(("parallel", "arbitrary")),
    )(tile_e, n_used, xs, wg, wu, wd)


def _combine_kernel(pos1_ref, pos2_ref, x_ref, rec_ref, g_ref, b_ref, ys_hbm, o_ref, ob_ref,
                    y1_sc, y2_sc, sem, *, alpha, tm):
    base = pl.program_id(0) * tm

    def issue(r, carry):
        t = base + r
        _row_copy(ys_hbm, pos1_ref[t], y1_sc, r, sem).start()
        _row_copy(ys_hbm, pos2_ref[t], y2_sc, r, sem).start()
        return carry

    lax.fori_loop(0, tm, issue, 0)

    def drain(r, carry):
        _row_copy(ys_hbm, 0, y1_sc, 0, sem).wait()
        _row_copy(ys_hbm, 0, y2_sc, 0, sem).wait()
        return carry

    lax.fori_loop(0, tm, drain, 0)
    rec = rec_ref[...]
    f = rec[:, 2:3] * y1_sc[...] + rec[:, 3:4] * y2_sc[...]
    z = _ln_rows(alpha * x_ref[...] + f, g_ref[...], b_ref[...])
    o_ref[...] = z
    ob_ref[...] = z.astype(ob_ref.dtype)


def _combine(x, rec, ys, pos1, pos2, g, b, alpha):
    m, d = x.shape
    tm = _pick(m, (256, 128))
    row = lambda i, p1, p2: (i, 0)
    fixed = lambda i, p1, p2: (0, 0)
    return pl.pallas_call(
        functools.partial(_combine_kernel, alpha=alpha, tm=tm),
        grid_spec=pltpu.PrefetchScalarGridSpec(
            num_scalar_prefetch=2, grid=(m // tm,),
            in_specs=[pl.BlockSpec((tm, d), row), pl.BlockSpec((tm, LANE), row),
                      pl.BlockSpec((1, d), fixed), pl.BlockSpec((1, d), fixed),
                      pl.BlockSpec(memory_space=pl.ANY)],
            out_specs=[pl.BlockSpec((tm, d), row), pl.BlockSpec((tm, d), row)],
            scratch_shapes=[pltpu.VMEM((tm, d), F32), pltpu.VMEM((tm, d), F32),
                            pltpu.SemaphoreType.DMA]),
        out_shape=[jax.ShapeDtypeStruct((m, d), F32), jax.ShapeDtypeStruct((m, d), BF16)],
        compiler_params=_cparams(("arbitrary",)),
    )(pos1, pos2, x, rec, g, b, ys)


MOE_TILE = 512


def _routed_moe(x, xb, w_router, wg, wu, wd, g, b, alpha):
    m, d = x.shape
    ne = w_router.shape[1]
    rec, sel = _router(x, w_router)
    pos, meta = _slots(sel, rec, MOE_TILE, ne)
    pos1 = pos[:, 0].astype(jnp.int32)
    pos2 = pos[:, 1].astype(jnp.int32)
    n_tiles = (2 * m) // MOE_TILE + ne
    counts, starts = meta[0, :ne], meta[1, :ne]
    ends = starts + jnp.ceil(counts / MOE_TILE) * MOE_TILE
    tile_start = jnp.arange(n_tiles, dtype=F32) * MOE_TILE
    tile_e = jnp.minimum(jnp.sum(ends[None, :] <= tile_start[:, None], axis=1), ne - 1).astype(jnp.int32)
    n_used = (ends[ne - 1:ne] / MOE_TILE).astype(jnp.int32)
    x_words = lax.bitcast_convert_type(xb.reshape(m, d // 2, 2), jnp.uint32)
    xs_words = _dispatch(x_words, pos1, pos2, n_tiles * MOE_TILE)
    xs = lax.bitcast_convert_type(xs_words, BF16).reshape(n_tiles * MOE_TILE, d)
    ys = _grouped_ffn(xs, tile_e, n_used, wg, wu, wd, MOE_TILE)
    return _combine(x, rec, ys, pos1, pos2, g, b, alpha)


def _epi_scale(scale):
    return lambda accs, tabs, vecs: [accs[0] * scale]


def _epi_f32_and_bf16(accs, tabs, vecs):
    return [accs[0], accs[0]]


def _epi_rms(accs, tabs, vecs):
    r = _rms_rows(accs[0], vecs[0])
    return [r, r]


def _epi_rope(accs, tabs, vecs):
    return [accs[0] * tabs[0] + accs[1] * tabs[1]]


def _epi_gelu(accs, tabs, vecs):
    return [jax.nn.gelu(accs[0])]


def _epi_gelu_ln(accs, tabs, vecs):
    return [_ln_rows(jax.nn.gelu(accs[0]), vecs[0], vecs[1])]


def _epi_plain(accs, tabs, vecs):
    return [accs[0]]


def _epi_q_rope(scale):
    return lambda accs, tabs, vecs: [(accs[0] * tabs[0] + accs[1] * tabs[1]) * scale]


def _rope_tables(pos):
    half = B_ROPE // 2
    inv = ROPE_THETA ** (-jnp.arange(half, dtype=F32) / half)
    ang = pos.astype(F32)[:, None] * inv[None, :]
    cos, sin = jnp.cos(ang), jnp.sin(ang)
    cos2 = jnp.concatenate([cos, cos], axis=1)
    sin2 = jnp.concatenate([-sin, sin], axis=1)
    return cos2, sin2


def _swap_halves(w):
    half = w.shape[-1] // 2
    return jnp.concatenate([w[..., half:], w[..., :half]], axis=-1)


def kernel(x_prompt, x_sample, cache_a_k, cache_a_v, cache_b_latent, cache_b_krope, state_d_pool, w_in, a_lambda, a_subln, b_q_norm, b_w_uq, b_kv_norm, b_w_ukv, c_ln_g, c_ln_b, c_w_s, c_b_s, d_w_grp, d_scale, w_branch, w_out, ln1_g, ln1_b, ln2_g, ln2_b, f_w_gate, f_w_up, f_w_down, m_router, m_w_gate, m_w_up, m_w_down):
    depth = w_in.shape[0]
    nbp, lp, d_model = x_prompt.shape
    nbs, ls, _ = x_sample.shape
    past = cache_a_k.shape[2]
    lks = past + ls
    tp, ts = nbp * lp, nbs * ls
    aw = A_HEADS * 2 * A_DH
    q_rank = b_q_norm.shape[1]
    kv_rank = b_kv_norm.shape[1]
    c_width = c_ln_g.shape[1]
    d_width = d_scale.shape[1]
    alpha = (2.0 * depth) ** 0.25
    sizes = (aw, aw, aw, q_rank, kv_rank, B_ROPE, c_width, c_width, d_width) + (d_model,) * N_BRANCH
    offs = [0]
    for s in sizes:
        offs.append(offs[-1] + s)

    x = jnp.concatenate([x_prompt.reshape(tp, d_model), x_sample.reshape(ts, d_model)], axis=0)
    xb = x.astype(BF16)

    pos = jnp.concatenate([jnp.tile(jnp.arange(lp, dtype=jnp.int32), nbp),
                           jnp.tile(past + jnp.arange(ls, dtype=jnp.int32), nbs)])
    cos2, sin2 = _rope_tables(pos)
    rows = pos.shape[0]
    ones = jnp.ones((rows, B_NOPE), F32)
    zeros_n = jnp.zeros((rows, B_NOPE), F32)
    zeros_r = jnp.zeros((rows, 2 * LANE - B_NOPE - B_ROPE), F32)
    q_cos = jnp.concatenate([ones, cos2, zeros_r], axis=1)
    q_sin = jnp.concatenate([zeros_n, sin2, zeros_r], axis=1)

    slopes = jnp.exp2(-8.0 * jnp.arange(1, A_HEADS + 1, dtype=F32) / A_HEADS) * LOG2E
    slopes = jnp.broadcast_to(slopes[:, None, None], (A_HEADS, 1, LANE))

    states_p, states_s = [], []
    for l in range(depth):
        wl = w_in[l]
        seg = lambda k: wl[:, offs[k]:offs[k + 1]].astype(BF16)
        lam_init = 0.8 - 0.6 * math.exp(-0.3 * l)

        (q_a,) = _fused_mm(_epi_scale(A_DH ** -0.5 * LOG2E), xb, [seg(0)], [BF16], tn=1024)
        w_kv = jnp.concatenate([seg(1), seg(2)], axis=1)
        kv_a, kv_a_b = _fused_mm(_epi_f32_and_bf16, xb, [w_kv], [F32, BF16], tn=1024)
        subln = a_subln[l].reshape(1, 2 * A_DH)
        o_a_p = _diff_attn(q_a, kv_a_b, slopes, a_lambda[l], subln, nb=nbp, lq=lp, lk=lp, q_off=0,
                           q_row0=0, lam_init=lam_init)
        cache_kv = jnp.concatenate([cache_a_k[l].reshape(nbs, past, aw),
                                    cache_a_v[l].reshape(nbs, past, aw)], axis=2).astype(BF16)
        kv_s = jnp.concatenate([cache_kv, kv_a_b[tp:].reshape(nbs, ls, 2 * aw)], axis=1)
        o_a_s = _diff_attn(q_a, kv_s.reshape(nbs * lks, 2 * aw), slopes, a_lambda[l], subln, nb=nbs,
                           lq=ls, lk=lks, q_off=past, q_row0=tp, lam_init=lam_init)
        o_a = jnp.concatenate([o_a_p, o_a_s], axis=0)

        (cq,) = _fused_mm(_epi_rms, xb, [seg(3)], [BF16], tn=q_rank,
                          vecs=[b_q_norm[l].reshape(1, q_rank)])
        lat, lat_b = _fused_mm(_epi_rms, xb, [seg(4)], [F32, BF16], tn=kv_rank,
                               vecs=[b_kv_norm[l].reshape(1, kv_rank)])
        w_kr = wl[:, offs[5]:offs[6]]
        (kr,) = _fused_mm(_epi_rope, xb, [w_kr.astype(BF16), _swap_halves(w_kr).astype(BF16)], [F32],
                          tn=B_ROPE, tabs=[cos2, sin2])
        w_uq = b_w_uq[l].reshape(q_rank, B_HEADS, B_NOPE + B_ROPE)
        w_uq_n, w_uq_r = w_uq[..., :B_NOPE], w_uq[..., B_NOPE:]
        zpad = jnp.zeros((q_rank, B_HEADS, 2 * LANE - B_NOPE - B_ROPE), F32)
        w_q1 = jnp.concatenate([w_uq_n, w_uq_r, zpad], axis=-1).reshape(q_rank, -1).astype(BF16)
        w_q2 = jnp.concatenate([jnp.zeros_like(w_uq_n), _swap_halves(w_uq_r), zpad],
                               axis=-1).reshape(q_rank, -1).astype(BF16)
        (q_b,) = _fused_mm(_epi_q_rope((B_NOPE + B_ROPE) ** -0.5 * LOG2E), cq, [w_q1, w_q2], [BF16],
                           tn=2 * LANE, tabs=[q_cos, q_sin])
        w_ukv = b_w_ukv[l].reshape(kv_rank, B_HEADS, B_NOPE + B_VDIM)
        w_up = jnp.concatenate([w_ukv[..., :B_NOPE].reshape(kv_rank, -1),
                                w_ukv[..., B_NOPE:].reshape(kv_rank, -1)], axis=1).astype(BF16)
        lat_s = jnp.concatenate([cache_b_latent[l].astype(BF16), lat_b[tp:].reshape(nbs, ls, kv_rank)],
                                axis=1).reshape(nbs * lks, kv_rank)
        lat_cat = jnp.concatenate([lat_b[:tp], lat_s], axis=0)
        (kv_b,) = _fused_mm(_epi_plain, lat_cat, [w_up], [BF16], tn=1024)
        kr_s = jnp.concatenate([cache_b_krope[l], kr[tp:].reshape(nbs, ls, B_ROPE)],
                               axis=1).reshape(nbs * lks, B_ROPE)
        kr_cat = jnp.pad(jnp.concatenate([kr[:tp], kr_s], axis=0),
                         ((0, 0), (0, LANE - B_ROPE))).astype(BF16)
        o_b_p = _mla_attn(q_b, kv_b, kr_cat, nb=nbp, lq=lp, lk=lp, q_off=0, q_row0=0, kv_row0=0)
        if tp % lks:
            kv_b_s, kr_b_s, s_row0 = kv_b[tp:], kr_cat[tp:], 0
        else:
            kv_b_s, kr_b_s, s_row0 = kv_b, kr_cat, tp
        o_b_s = _mla_attn(q_b, kv_b_s, kr_b_s, nb=nbs, lq=ls, lk=lks, q_off=past, q_row0=tp,
                          kv_row0=s_row0)
        o_b = jnp.concatenate([o_b_p, o_b_s], axis=0)

        (u_c,) = _fused_mm(_epi_gelu, xb, [seg(6)], [BF16], tn=1024)
        (v_c,) = _fused_mm(_epi_gelu_ln, xb, [seg(7)], [F32], tn=c_width,
                           vecs=[c_ln_g[l].reshape(1, c_width), c_ln_b[l].reshape(1, c_width)])
        w_s, b_s = c_w_s[l], c_b_s[l]
        reps = C_BLOCK // ls if ls < C_BLOCK else 1
        w_tl = w_s[:, :C_BLOCK // reps, :C_BLOCK // reps]
        w_smp = jnp.zeros_like(w_s)
        for r in range(reps):
            sl = slice(r * (C_BLOCK // reps), (r + 1) * (C_BLOCK // reps))
            w_smp = w_smp.at[:, sl, sl].set(w_tl)
        b_smp = jnp.tile(b_s[:, :C_BLOCK // reps], (1, reps))
        w_sel = jnp.stack([w_s, w_smp]).astype(BF16)
        b_sel = jnp.stack([b_s, b_smp])[..., None]
        o_c = _spatial_gate(u_c, v_c, w_sel, b_sel, tp)

        (d_in,) = _fused_mm(_epi_plain, xb, [seg(8)], [F32], tn=1024)
        w_grp = d_w_grp[l].astype(BF16)
        scale = d_scale[l].reshape(1, d_width)
        hist_p = jnp.zeros((nbp, D_HALO, d_width), F32)
        hist_s = jnp.pad(state_d_pool[l], ((0, 0), (D_HALO - D_HIST, 0), (0, 0)))
        o_d_p = _pool(d_in, hist_p, w_grp, scale, nb=nbp, lq=lp, row0=0, past=0)
        o_d_s = _pool(d_in, hist_s, w_grp, scale, nb=nbs, lq=ls, row0=tp, past=past)
        o_d = jnp.concatenate([o_d_p, o_d_s], axis=0)

        w_gate = wl[:, offs[9]:offs[13]].astype(BF16)
        y = _merge(xb, w_gate, [o_a, o_b, o_c, o_d], w_branch[l].astype(BF16))
        x, xb = _out_ln(y, x, w_out[l].astype(BF16), ln1_g[l].reshape(1, -1), ln1_b[l].reshape(1, -1),
                        alpha)

        i = l // 2
        g2, b2 = ln2_g[l].reshape(1, -1), ln2_b[l].reshape(1, -1)
        if l % 2 == 0:
            x, xb = _ffn(xb, x, f_w_gate[i].astype(BF16), f_w_up[i].astype(BF16),
                         f_w_down[i].astype(BF16), g2, b2, alpha)
        else:
            x, xb = _routed_moe(x, xb, m_router[i], m_w_gate[i].astype(BF16), m_w_up[i].astype(BF16),
                                m_w_down[i].astype(BF16), g2, b2, alpha)

        kv_p = kv_a[:tp].reshape(nbp, lp, 2, A_HEADS, 2 * A_DH)
        kv_smp = kv_a[tp:].reshape(nbs, ls, 2, A_HEADS, 2 * A_DH)
        states_p.append((kv_p[:, :, 0], kv_p[:, :, 1], lat[:tp].reshape(nbp, lp, kv_rank),
                         kr[:tp].reshape(nbp, lp, B_ROPE),
                         d_in[:tp].reshape(nbp, lp, d_width)[:, lp - D_HIST:]))
        states_s.append((kv_smp[:, :, 0], kv_smp[:, :, 1], lat[tp:].reshape(nbs, ls, kv_rank),
                         kr[tp:].reshape(nbs, ls, B_ROPE), v_c[tp:].reshape(nbs, ls, c_width),
                         jnp.concatenate([state_d_pool[l], d_in[tp:].reshape(nbs, ls, d_width)],
                                         axis=1)[:, -D_HIST:]))

    yp = x[:tp].reshape(nbp, lp, d_model)
    ys = x[tp:].reshape(nbs, ls, d_model)
    stack = lambda sts, k: jnp.stack([s[k] for s in sts])
    return (yp, ys, stack(states_p, 0), stack(states_p, 1), stack(states_p, 2), stack(states_p, 3),
            stack(states_p, 4), stack(states_s, 0), stack(states_s, 1), stack(states_s, 2),
            stack(states_s, 3), stack(states_s, 4), stack(states_s, 5))
```

```python
import functools
import math

import jax
import jax.numpy as jnp
from jax import lax
from jax.experimental import pallas as pl
from jax.experimental.pallas import tpu as pltpu

CHUNK = 64
A_HEADS = 8
A_DH = 64
B_HEADS = 8
B_NOPE = 128
B_ROPE = 64
B_VDIM = 128
C_GROUPS = 4
C_BLOCK = 128
D_WINDOWS = (2, 4, 8, 16)
D_HIST = 15
D_HALO = 16
N_BRANCH = 4
ROPE_THETA = 10000.0
LN_EPS = 1e-5
RMS_EPS = 1e-6
LOG2E = 1.4426950408889634
NEG = -1e30
LANE = 128
VMEM_LIMIT = 56 * 1024 * 1024

F32 = jnp.float32
BF16 = jnp.bfloat16


def _pick(n, cands):
    for c in cands:
        if n % c == 0:
            return c
    raise ValueError(f"no tile for {n} in {cands}")


def _cparams(sem):
    return pltpu.CompilerParams(dimension_semantics=sem, vmem_limit_bytes=VMEM_LIMIT)


def _ln_rows(z, g, b):
    mu = jnp.mean(z, axis=-1, keepdims=True)
    zc = z - mu
    var = jnp.mean(zc * zc, axis=-1, keepdims=True)
    return zc * lax.rsqrt(var + LN_EPS) * g + b


def _rms_rows(z, g):
    return z * lax.rsqrt(jnp.mean(z * z, axis=-1, keepdims=True) + RMS_EPS) * g


def _fused_mm_kernel(*refs, fn, nw, ntab, nvec, precision):
    x = refs[0][...]
    w_refs = refs[1:1 + nw]
    tab_refs = refs[1 + nw:1 + nw + ntab]
    vec_refs = refs[1 + nw + ntab:1 + nw + ntab + nvec]
    out_refs = refs[1 + nw + ntab + nvec:]
    accs = [jnp.dot(x, w[...], preferred_element_type=F32, precision=precision) for w in w_refs]
    res = fn(accs, [t[...] for t in tab_refs], [v[...] for v in vec_refs])
    for o, r in zip(out_refs, res):
        o[...] = r.astype(o.dtype)


def _fused_mm(fn, x, ws, out_dtypes, *, tn, tm=None, tabs=(), tabs_follow_cols=False, vecs=(),
              precision=None):
    m, kdim = x.shape
    n = ws[0].shape[1]
    tm = tm or _pick(m, (1024, 512, 256, 128))
    grid = (m // tm, n // tn)
    in_specs = [pl.BlockSpec((tm, kdim), lambda i, j: (i, 0))]
    in_specs += [pl.BlockSpec((kdim, tn), lambda i, j: (0, j)) for _ in ws]
    for t in tabs:
        if tabs_follow_cols:
            in_specs.append(pl.BlockSpec((tm, tn), lambda i, j: (i, j)))
        else:
            in_specs.append(pl.BlockSpec((tm, t.shape[1]), lambda i, j: (i, 0)))
    in_specs += [pl.BlockSpec((1, tn), lambda i, j: (0, j)) for _ in vecs]
    out_specs = [pl.BlockSpec((tm, tn), lambda i, j: (i, j)) for _ in out_dtypes]
    out_shape = [jax.ShapeDtypeStruct((m, n), d) for d in out_dtypes]
    kern = functools.partial(_fused_mm_kernel, fn=fn, nw=len(ws), ntab=len(tabs), nvec=len(vecs),
                             precision=precision)
    return pl.pallas_call(
        kern, grid=grid, in_specs=in_specs, out_specs=out_specs, out_shape=out_shape,
        compiler_params=_cparams(("parallel", "arbitrary")),
    )(x, *ws, *tabs, *vecs)


ATTN_ROW_GROUPS = 2
DIFF_ATTN_TQ = 512
MLA_ATTN_TQ = 1024


def _row_groups(rows):
    g = ATTN_ROW_GROUPS
    while rows % (16 * g):
        g //= 2
    return g


def _nt_dot(a, b):
    return lax.dot_general(a, b, (((1,), (1,)), ((), ())), preferred_element_type=F32)


def _softmax_pv(scores, v, biases, m_sc, acc_sc):
    width = v.shape[0]
    v_ext = jnp.concatenate([v, jnp.ones(v.shape, v.dtype)], axis=1)
    r0 = 0
    for s, bias in zip(scores, biases):
        rows = slice(r0, r0 + s.shape[0])
        r0 += s.shape[0]
        if bias is not None:
            s = s + bias
        m_prev = m_sc[rows, :]
        m_new = jnp.maximum(m_prev, jnp.max(s, axis=1, keepdims=True))
        alpha = jnp.exp2(m_prev - m_new)
        m_rep = jnp.tile(m_new, (1, width // LANE)) if width >= LANE else m_new[:, :width]
        p = jnp.exp2(s - m_rep)
        pv = jnp.dot(p.astype(BF16), v_ext, preferred_element_type=F32)
        acc_sc[rows, :] = jnp.tile(alpha, (1, 2)) * acc_sc[rows, :] + pv
        m_sc[rows, :] = m_new


def _sweep_keys(q_groups, keys, v_ref, off_biases, diag_biases, m_sc, acc_sc, s_sc, *, q0, tq, tk):
    m_sc[...] = jnp.full(m_sc.shape, NEG, F32)
    acc_sc[...] = jnp.zeros(acc_sc.shape, F32)
    n_off = q0 // tk
    kd = pl.multiple_of(q0, tq)

    def scores(start, size):
        k = keys(start, size)
        return [_nt_dot(q, k) for q in q_groups()]

    if s_sc is None:
        def body(j, carry):
            ks = pl.multiple_of(j * tk, tk)
            _softmax_pv(scores(ks, tk), v_ref[pl.ds(ks, tk), :], off_biases(ks), m_sc, acc_sc)
            return carry

        lax.fori_loop(0, n_off, body, 0)
        _softmax_pv(scores(kd, tq), v_ref[pl.ds(kd, tq), :], diag_biases, m_sc, acc_sc)
        return

    assert tq == tk
    bounds = []
    r0 = 0
    for q in q_groups():
        bounds.append(slice(r0, r0 + q.shape[0]))
        r0 += q.shape[0]

    def stash(new_scores):
        for rows, s in zip(bounds, new_scores):
            s_sc[rows, :] = s

    def staged():
        return [s_sc[rows, :] for rows in bounds]

    stash(scores(0, tk))

    def body(j, carry):
        ks = pl.multiple_of(j * tk, tk)
        nxt = scores(pl.multiple_of(ks + tk, tk), tk)
        _softmax_pv(staged(), v_ref[pl.ds(ks, tk), :], off_biases(ks), m_sc, acc_sc)
        stash(nxt)
        return carry

    lax.fori_loop(0, n_off, body, 0)
    _softmax_pv(staged(), v_ref[pl.ds(kd, tq), :], diag_biases, m_sc, acc_sc)


def _chunk_mask_bias(q0, tq, slope):
    qpos = q0 + lax.broadcasted_iota(jnp.int32, (tq, tq), 0)
    kpos = q0 + lax.broadcasted_iota(jnp.int32, (tq, tq), 1)
    visible = (kpos >> 6) <= (qpos >> 6)
    if slope is None:
        return jnp.where(visible, 0.0, NEG)
    rel = (qpos - q0 - jnp.abs(qpos - kpos)).astype(F32)
    return jnp.where(visible, slope * rel, NEG)


def _diff_attn_kernel(q_ref, k_ref, v_ref, slope_ref, lam_ref, g_ref, o_ref,
                      qs_sc, m_sc, acc_sc, *maybe_s_sc, tq, tk, q_off, lam_init):
    qi = pl.program_id(2)
    q0 = q_off + qi * tq
    q = q_ref[...]
    lane = lax.broadcasted_iota(jnp.int32, q.shape, 1)
    zero = jnp.zeros_like(q)
    qs_sc[0:tq, :] = jnp.where(lane < A_DH, q, zero)
    qs_sc[tq:2 * tq, :] = jnp.where(lane >= A_DH, q, zero)
    slope = slope_ref[...][:, 0:1]
    groups = _row_groups(2 * tq)
    gr = 2 * tq // groups

    def q_groups():
        return [qs_sc[g * gr:(g + 1) * gr, :] for g in range(groups)]

    def off_biases(ks):
        kpos = ks + lax.broadcasted_iota(jnp.int32, (1, tk), 1)
        return [slope * (kpos - q0).astype(F32)] * groups

    bias = _chunk_mask_bias(q0, tq, slope)
    if gr > tq:
        diag_biases = [jnp.concatenate([bias] * (gr // tq), axis=0)] * groups
    else:
        diag_biases = [bias[(g * gr) % tq:(g * gr) % tq + gr, :] for g in range(groups)]
    _sweep_keys(q_groups, lambda ks, size: k_ref[pl.ds(ks, size), :], v_ref, off_biases, diag_biases,
                m_sc, acc_sc, maybe_s_sc[0] if maybe_s_sc else None, q0=q0, tq=tq, tk=tk)

    lp = lam_ref[...]
    lam = (jnp.exp(jnp.sum(lp[0:1, :] * lp[1:2, :], axis=1, keepdims=True))
           - jnp.exp(jnp.sum(lp[2:3, :] * lp[3:4, :], axis=1, keepdims=True)) + lam_init)
    acc = acc_sc[...]
    o1 = acc[0:tq, 0:LANE] / acc[0:tq, LANE:2 * LANE]
    o2 = acc[tq:2 * tq, 0:LANE] / acc[tq:2 * tq, LANE:2 * LANE]
    o = _rms_rows(o1 - lam * o2, g_ref[...]) * (1.0 - lam_init)
    o_ref[...] = o.astype(o_ref.dtype)


def _mla_attn_kernel(q_ref, kn_ref, kr_ref, v_ref, o_ref, m_sc, acc_sc, *maybe_s_sc,
                     tq, tk, q_off):
    qi = pl.program_id(2)
    q0 = q_off + qi * tq
    groups = _row_groups(tq)
    gr = tq // groups

    def q_groups():
        return [q_ref[g * gr:(g + 1) * gr, :] for g in range(groups)]

    def keys(ks, size):
        return jnp.concatenate([kn_ref[pl.ds(ks, size), :], kr_ref[pl.ds(ks, size), :]], axis=1)

    mask = _chunk_mask_bias(q0, tq, None)
    diag_biases = [mask[g * gr:(g + 1) * gr, :] for g in range(groups)]
    _sweep_keys(q_groups, keys, v_ref, lambda ks: [None] * groups, diag_biases, m_sc, acc_sc,
                maybe_s_sc[0] if maybe_s_sc else None, q0=q0, tq=tq, tk=tk)
    acc = acc_sc[...]
    o_ref[...] = (acc[:, 0:LANE] / acc[:, LANE:2 * LANE]).astype(o_ref.dtype)


def _attn_tiles(lq, q_off, tq_max):
    sizes = tuple(t for t in (1024, 512, 256, 128, 64) if t <= tq_max)
    tq = _pick(lq, sizes)
    tk = tq if q_off == 0 else _pick(q_off, (512, 256, 128, 64))
    return tq, tk


def _score_stage(rows, tq, tk):
    return [pltpu.VMEM((rows, tk), F32)] if tq == tk else []


def _diff_attn(q, kv, slopes, lam_p, subln, *, nb, lq, lk, q_off, q_row0, lam_init):
    tq, tk = _attn_tiles(lq, q_off, DIFF_ATTN_TQ)
    nq = lq // tq
    qb0 = q_row0 // tq
    kern = functools.partial(_diff_attn_kernel, tq=tq, tk=tk, q_off=q_off, lam_init=lam_init)
    return pl.pallas_call(
        kern, grid=(nb, A_HEADS, nq),
        in_specs=[
            pl.BlockSpec((tq, LANE), lambda b, h, i: (qb0 + b * nq + i, h)),
            pl.BlockSpec((lk, LANE), lambda b, h, i: (b, h)),
            pl.BlockSpec((lk, LANE), lambda b, h, i: (b, A_HEADS + h)),
            pl.BlockSpec((None, 1, LANE), lambda b, h, i: (h, 0, 0)),
            pl.BlockSpec((4, A_DH), lambda b, h, i: (0, 0)),
            pl.BlockSpec((1, LANE), lambda b, h, i: (0, 0)),
        ],
        out_specs=pl.BlockSpec((tq, LANE), lambda b, h, i: (b * nq + i, h)),
        out_shape=jax.ShapeDtypeStruct((nb * lq, A_HEADS * LANE), BF16),
        scratch_shapes=[pltpu.VMEM((2 * tq, LANE), BF16), pltpu.VMEM((2 * tq, LANE), F32),
                        pltpu.VMEM((2 * tq, 2 * LANE), F32)] + _score_stage(2 * tq, tq, tk),
        compiler_params=_cparams(("parallel", "parallel", "arbitrary")),
    )(q, kv, kv, slopes, lam_p, subln)


def _mla_attn(q, kv, kr, *, nb, lq, lk, q_off, q_row0, kv_row0):
    tq, tk = _attn_tiles(lq, q_off, MLA_ATTN_TQ)
    nq = lq // tq
    qb0 = q_row0 // tq
    kb0 = kv_row0 // lk
    kern = functools.partial(_mla_attn_kernel, tq=tq, tk=tk, q_off=q_off)
    return pl.pallas_call(
        kern, grid=(nb, B_HEADS, nq),
        in_specs=[
            pl.BlockSpec((tq, 2 * LANE), lambda b, h, i: (qb0 + b * nq + i, h)),
            pl.BlockSpec((lk, LANE), lambda b, h, i: (kb0 + b, h)),
            pl.BlockSpec((lk, LANE), lambda b, h, i: (kb0 + b, 0)),
            pl.BlockSpec((lk, LANE), lambda b, h, i: (kb0 + b, B_HEADS + h)),
        ],
        out_specs=pl.BlockSpec((tq, LANE), lambda b, h, i: (b * nq + i, h)),
        out_shape=jax.ShapeDtypeStruct((nb * lq, B_HEADS * LANE), BF16),
        scratch_shapes=[pltpu.VMEM((tq, LANE), F32), pltpu.VMEM((tq, 2 * LANE), F32)]
        + _score_stage(tq, tq, tk),
        compiler_params=_cparams(("parallel", "parallel", "arbitrary")),
    )(q, kv, kr, kv)


def _spatial_gate_kernel(u_ref, v_ref, w_ref, b_ref, o_ref, *, n_chunks):
    gw = u_ref.shape[1] // C_GROUPS
    row = lax.broadcasted_iota(jnp.int32, (C_BLOCK, C_BLOCK), 0)
    col = lax.broadcasted_iota(jnp.int32, (C_BLOCK, C_BLOCK), 1)
    for g in range(C_GROUPS):
        w = jnp.where(row >= col, w_ref[g], jnp.zeros((C_BLOCK, C_BLOCK), BF16))
        bias = b_ref[g]
        for c in range(n_chunks):
            rows = slice(c * C_BLOCK, (c + 1) * C_BLOCK)
            cols = slice(g * gw, (g + 1) * gw)
            vv = v_ref[rows, cols].astype(BF16)
            sg = jnp.dot(w, vv, preferred_element_type=F32) + bias
            o_ref[rows, cols] = (u_ref[rows, cols].astype(F32) * sg).astype(o_ref.dtype)


def _spatial_gate(u, v, w_sel, b_sel, n_prompt_rows):
    m, width = u.shape
    tm = _pick(m, (1024, 512, 256, 128))
    tm = math.gcd(tm, n_prompt_rows)
    npt = n_prompt_rows // tm
    sel = lambda i: jnp.where(i >= npt, 1, 0)
    kern = functools.partial(_spatial_gate_kernel, n_chunks=tm // C_BLOCK)
    return pl.pallas_call(
        kern, grid=(m // tm,),
        in_specs=[
            pl.BlockSpec((tm, width), lambda i: (i, 0)),
            pl.BlockSpec((tm, width), lambda i: (i, 0)),
            pl.BlockSpec((None, C_GROUPS, C_BLOCK, C_BLOCK), lambda i: (sel(i), 0, 0, 0)),
            pl.BlockSpec((None, C_GROUPS, C_BLOCK, 1), lambda i: (sel(i), 0, 0, 0)),
        ],
        out_specs=pl.BlockSpec((tm, width), lambda i: (i, 0)),
        out_shape=jax.ShapeDtypeStruct((m, width), BF16),
        compiler_params=_cparams(("parallel",)),
    )(u, v, w_sel, b_sel)


def _pool_kernel(x_ref, prev_ref, hist_ref, w_ref, sc_ref, o_ref, xp_sc, *, tm, past):
    i = pl.program_id(1)
    xp_sc[0:D_HALO, :] = jnp.where(i == 0, hist_ref[...], prev_ref[...])
    xp_sc[D_HALO:D_HALO + tm, :] = x_ref[...]
    gw = x_ref.shape[1] // len(D_WINDOWS)
    pos = past + i * tm + lax.broadcasted_iota(jnp.int32, (tm, 1), 0)
    for gi, win in enumerate(D_WINDOWS):
        cols = slice(gi * gw, (gi + 1) * gw)
        tot = xp_sc[D_HALO:D_HALO + tm, cols]
        for j in range(1, win):
            tot = tot + xp_sc[D_HALO - j:D_HALO - j + tm, cols]
        cnt = jnp.minimum(pos + 1, win).astype(F32)
        pooled = tot / cnt - xp_sc[D_HALO:D_HALO + tm, cols]
        y = jnp.dot(pooled.astype(BF16), w_ref[gi], preferred_element_type=F32)
        o_ref[:, cols] = (y * sc_ref[:, cols]).astype(o_ref.dtype)


def _pool(d_in, hist, w_grp, scale, *, nb, lq, row0, past):
    width = d_in.shape[1]
    tm = _pick(lq, (512, 256, 128, 64))
    nt = lq // tm
    rb0 = row0 // tm
    hb0 = row0 // D_HALO
    per = tm // D_HALO
    kern = functools.partial(_pool_kernel, tm=tm, past=past)
    return pl.pallas_call(
        kern, grid=(nb, nt),
        in_specs=[
            pl.BlockSpec((tm, width), lambda b, i: (rb0 + b * nt + i, 0)),
            pl.BlockSpec((D_HALO, width),
                         lambda b, i: (jnp.maximum(hb0 + (b * nt + i) * per - 1, 0), 0)),
            pl.BlockSpec((None, D_HALO, width), lambda b, i: (b, 0, 0)),
            pl.BlockSpec(w_grp.shape, lambda b, i: (0, 0, 0)),
            pl.BlockSpec((1, width), lambda b, i: (0, 0)),
        ],
        out_specs=pl.BlockSpec((tm, width), lambda b, i: (b * nt + i, 0)),
        out_shape=jax.ShapeDtypeStruct((nb * lq, width), BF16),
        scratch_shapes=[pltpu.VMEM((D_HALO + tm, width), F32)],
        compiler_params=_cparams(("parallel", "arbitrary")),
    )(d_in, d_in, hist, w_grp, scale)


def _merge_kernel(x_ref, wg0, wg1, wg2, wg3, b0, b1, b2, b3, wb_ref, o_ref):
    x = x_ref[...]
    y = None
    for n, (wg, br) in enumerate(((wg0, b0), (wg1, b1), (wg2, b2), (wg3, b3))):
        gate = jax.nn.sigmoid(jnp.dot(x, wg[...], preferred_element_type=F32))
        t = gate * jnp.dot(br[...], wb_ref[n], preferred_element_type=F32)
        y = t if y is None else y + t
    o_ref[...] = y.astype(o_ref.dtype)


def _merge(xb, w_gate, branches, w_branch):
    m, d = xb.shape
    bw = branches[0].shape[1]
    tm = _pick(m, (512, 256, 128))
    tn = 512
    nj = d // tn
    in_specs = [pl.BlockSpec((tm, d), lambda i, j: (i, 0))]
    in_specs += [pl.BlockSpec((d, tn), lambda i, j, n=n: (0, n * nj + j)) for n in range(N_BRANCH)]
    in_specs += [pl.BlockSpec((tm, bw), lambda i, j: (i, 0)) for _ in range(N_BRANCH)]
    in_specs += [pl.BlockSpec((N_BRANCH, bw, tn), lambda i, j: (0, 0, j))]
    return pl.pallas_call(
        _merge_kernel, grid=(m // tm, nj), in_specs=in_specs,
        out_specs=pl.BlockSpec((tm, tn), lambda i, j: (i, j)),
        out_shape=jax.ShapeDtypeStruct((m, d), BF16),
        compiler_params=_cparams(("parallel", "arbitrary")),
    )(xb, w_gate, w_gate, w_gate, w_gate, *branches, w_branch)


def _out_ln_kernel(y_ref, x_ref, w_ref, g_ref, b_ref, o_ref, ob_ref, *, alpha):
    m = jnp.dot(y_ref[...], w_ref[...], preferred_element_type=F32)
    z = _ln_rows(alpha * x_ref[...] + m, g_ref[...], b_ref[...])
    o_ref[...] = z
    ob_ref[...] = z.astype(ob_ref.dtype)


def _out_ln(y, x, w_out, g, b, alpha):
    m, d = x.shape
    tm = _pick(m, (256, 128))
    row = lambda i: (i, 0)
    fixed = lambda i: (0, 0)
    return pl.pallas_call(
        functools.partial(_out_ln_kernel, alpha=alpha), grid=(m // tm,),
        in_specs=[pl.BlockSpec((tm, d), row), pl.BlockSpec((tm, d), row), pl.BlockSpec((d, d), fixed),
                  pl.BlockSpec((1, d), fixed), pl.BlockSpec((1, d), fixed)],
        out_specs=[pl.BlockSpec((tm, d), row), pl.BlockSpec((tm, d), row)],
        out_shape=[jax.ShapeDtypeStruct((m, d), F32), jax.ShapeDtypeStruct((m, d), BF16)],
        compiler_params=_cparams(("parallel",)),
    )(y, x, w_out, g, b)


def _ffn_kernel(xb_ref, x_ref, wg_ref, wu_ref, wd_ref, g_ref, b_ref, o_ref, ob_ref, acc_sc, *, alpha):
    f = pl.program_id(1)

    @pl.when(f == 0)
    def _():
        acc_sc[...] = jnp.zeros(acc_sc.shape, F32)

    xb = xb_ref[...]
    hg = jnp.dot(xb, wg_ref[...], preferred_element_type=F32)
    hu = jnp.dot(xb, wu_ref[...], preferred_element_type=F32)
    h = hg * jax.nn.sigmoid(hg) * hu
    acc_sc[...] += jnp.dot(h.astype(BF16), wd_ref[...], preferred_element_type=F32)

    @pl.when(f == pl.num_programs(1) - 1)
    def _():
        z = _ln_rows(alpha * x_ref[...] + acc_sc[...], g_ref[...], b_ref[...])
        o_ref[...] = z
        ob_ref[...] = z.astype(ob_ref.dtype)


def _ffn(xb, x, wg, wu, wd, g, b, alpha):
    m, d = x.shape
    ff = wg.shape[1]
    tm = _pick(m, (512, 256, 128))
    tf = _pick(ff, (512, 256, 128))
    row = lambda i, f: (i, 0)
    fixed = lambda i, f: (0, 0)
    return pl.pallas_call(
        functools.partial(_ffn_kernel, alpha=alpha), grid=(m // tm, ff // tf),
        in_specs=[pl.BlockSpec((tm, d), row), pl.BlockSpec((tm, d), row),
                  pl.BlockSpec((d, tf), lambda i, f: (0, f)), pl.BlockSpec((d, tf), lambda i, f: (0, f)),
                  pl.BlockSpec((tf, d), lambda i, f: (f, 0)),
                  pl.BlockSpec((1, d), fixed), pl.BlockSpec((1, d), fixed)],
        out_specs=[pl.BlockSpec((tm, d), row), pl.BlockSpec((tm, d), row)],
        out_shape=[jax.ShapeDtypeStruct((m, d), F32), jax.ShapeDtypeStruct((m, d), BF16)],
        scratch_shapes=[pltpu.VMEM((tm, d), F32)],
        compiler_params=_cparams(("parallel", "arbitrary")),
    )(xb, x, wg, wu, wd, g, b)


def _router_kernel(x_ref, w_ref, o_ref, sel_ref, *, n_experts):
    logits = jnp.dot(x_ref[...], w_ref[...], preferred_element_type=F32,
                     precision=lax.Precision.HIGHEST)
    lane = lax.broadcasted_iota(jnp.int32, logits.shape, 1).astype(F32)
    lg = jnp.where(lane < n_experts, logits, NEG)
    m1 = jnp.max(lg, axis=1, keepdims=True)
    i1 = jnp.min(jnp.where(lg == m1, lane, float(LANE)), axis=1, keepdims=True)
    lg2 = jnp.where(lane == i1, NEG, lg)
    m2 = jnp.max(lg2, axis=1, keepdims=True)
    i2 = jnp.min(jnp.where(lg2 == m2, lane, float(LANE)), axis=1, keepdims=True)
    ex = jnp.exp(m2 - m1)
    p1 = 1.0 / (1.0 + ex)
    p2 = ex / (1.0 + ex)
    o_ref[...] = (jnp.where(lane == 0.0, i1, 0.0) + jnp.where(lane == 1.0, i2, 0.0)
                  + jnp.where(lane == 2.0, p1, 0.0) + jnp.where(lane == 3.0, p2, 0.0))
    sel_ref[...] = jnp.where((lane == i1) | (lane == i2), 1.0, 0.0).astype(sel_ref.dtype)


def _router(x, w_router):
    m, d = x.shape
    ne = w_router.shape[1]
    w = jnp.pad(w_router, ((0, 0), (0, LANE - ne)))
    tm = _pick(m, (512, 256, 128))
    row = lambda i: (i, 0)
    return pl.pallas_call(
        functools.partial(_router_kernel, n_experts=ne), grid=(m // tm,),
        in_specs=[pl.BlockSpec((tm, d), row), pl.BlockSpec((d, LANE), lambda i: (0, 0))],
        out_specs=[pl.BlockSpec((tm, LANE), row), pl.BlockSpec((tm, LANE), row)],
        out_shape=[jax.ShapeDtypeStruct((m, LANE), F32), jax.ShapeDtypeStruct((m, LANE), BF16)],
        compiler_params=_cparams(("parallel",)),
    )(x, w)


def _slot_kernel(sel_ref, rec_ref, pos_ref, meta_ref, cnt_sc, off_sc, *, tile, n_experts):
    phase = pl.program_id(0)
    i = pl.program_id(1)
    sel = sel_ref[...]
    tm = sel.shape[0]
    lane = lax.broadcasted_iota(jnp.int32, (1, LANE), 1)

    @pl.when((phase == 0) & (i == 0))
    def _():
        cnt_sc[...] = jnp.zeros(cnt_sc.shape, F32)

    @pl.when(phase == 0)
    def _():
        cnt_sc[...] += jnp.sum(sel.astype(F32), axis=0, keepdims=True)

    @pl.when((phase == 1) & (i == 0))
    def _():
        counts = cnt_sc[...]
        padded = jnp.ceil(counts / tile) * tile
        starts = jnp.zeros((1, LANE), F32)
        for e in range(1, n_experts):
            before = jnp.sum(jnp.where(lane < e, padded, 0.0), axis=1, keepdims=True)
            starts = jnp.where(lane == e, before, starts)
        off_sc[...] = starts
        row = lax.broadcasted_iota(jnp.int32, meta_ref.shape, 0)
        meta_ref[...] = jnp.where(row == 0, counts, jnp.where(row == 1, starts, 0.0))
        cnt_sc[...] = jnp.zeros(cnt_sc.shape, F32)

    @pl.when(phase == 1)
    def _():
        r = lax.broadcasted_iota(jnp.int32, (tm, tm), 0)
        c = lax.broadcasted_iota(jnp.int32, (tm, tm), 1)
        tri = jnp.where(r >= c, 1.0, 0.0).astype(BF16)
        csum = jnp.dot(tri, sel, preferred_element_type=F32)
        slot = off_sc[...] + cnt_sc[...] + csum - sel.astype(F32)
        rec = rec_ref[...]
        lanef = lane.astype(F32)
        pos1 = jnp.sum(jnp.where(lanef == rec[:, 0:1], slot, 0.0), axis=1, keepdims=True)
        pos2 = jnp.sum(jnp.where(lanef == rec[:, 1:2], slot, 0.0), axis=1, keepdims=True)
        pos_ref[...] = jnp.where(lane == 0, pos1, 0.0) + jnp.where(lane == 1, pos2, 0.0)
        cnt_sc[...] += csum[tm - 1:tm, :]


def _slots(sel, rec, tile, n_experts):
    m = sel.shape[0]
    tm = _pick(m, (512, 256, 128))
    row = lambda p, i: (i, 0)
    return pl.pallas_call(
        functools.partial(_slot_kernel, tile=tile, n_experts=n_experts), grid=(2, m // tm),
        in_specs=[pl.BlockSpec((tm, LANE), row), pl.BlockSpec((tm, LANE), row)],
        out_specs=[pl.BlockSpec((tm, LANE), lambda p, i: (i * p, 0)),
                   pl.BlockSpec((8, LANE), lambda p, i: (0, 0))],
        out_shape=[jax.ShapeDtypeStruct((m, LANE), F32), jax.ShapeDtypeStruct((8, LANE), F32)],
        scratch_shapes=[pltpu.VMEM((1, LANE), F32), pltpu.VMEM((1, LANE), F32)],
        compiler_params=_cparams(("arbitrary", "arbitrary")),
    )(sel, rec)


def _row_copy(src, src_row, dst, dst_row, sem):
    return pltpu.make_async_copy(src.at[pl.ds(src_row, 1)], dst.at[pl.ds(dst_row, 1)], sem)


def _dispatch_kernel(pos1_ref, pos2_ref, pad_row_ref, pad_len_ref, x_ref, xs_hbm, zero_sc, sem, *, tm):
    step = pl.program_id(0)
    base = step * tm

    @pl.when(step == 0)
    def _():
        zero_sc[...] = jnp.zeros(zero_sc.shape, zero_sc.dtype)
        for e in range(pad_row_ref.shape[0]):
            def fill(r, carry, e=e):
                _row_copy(zero_sc, 0, xs_hbm, pad_row_ref[e] + r, sem).start()
                return carry

            def fill_wait(r, carry):
                _row_copy(zero_sc, 0, xs_hbm, 0, sem).wait()
                return carry

            lax.fori_loop(0, pad_len_ref[e], fill, 0)
            lax.fori_loop(0, pad_len_ref[e], fill_wait, 0)

    def issue(r, carry):
        t = base + r
        _row_copy(x_ref, r, xs_hbm, pos1_ref[t], sem).start()
        _row_copy(x_ref, r, xs_hbm, pos2_ref[t], sem).start()
        return carry

    lax.fori_loop(0, tm, issue, 0)

    def drain(r, carry):
        _row_copy(x_ref, 0, xs_hbm, 0, sem).wait()
        _row_copy(x_ref, 0, xs_hbm, 0, sem).wait()
        return carry

    lax.fori_loop(0, tm, drain, 0)


def _dispatch(x, pos1, pos2, pad_row, pad_len, n_rows):
    m, d = x.shape
    tm = _pick(m, (512, 256, 128))
    kern = functools.partial(_dispatch_kernel, tm=tm)
    return pl.pallas_call(
        kern,
        grid_spec=pltpu.PrefetchScalarGridSpec(
            num_scalar_prefetch=4, grid=(m // tm,),
            in_specs=[pl.BlockSpec((tm, d), lambda i, *_: (i, 0))],
            out_specs=pl.BlockSpec(memory_space=pl.ANY),
            scratch_shapes=[pltpu.VMEM((8, d), x.dtype), pltpu.SemaphoreType.DMA]),
        out_shape=jax.ShapeDtypeStruct((n_rows, d), x.dtype),
        compiler_params=_cparams(("arbitrary",)),
    )(pos1, pos2, pad_row, pad_len, x)


def _grouped_ffn_kernel(tile_e_ref, n_used_ref, x_ref, wg_ref, wu_ref, wd_ref, o_ref, acc_sc):
    del tile_e_ref
    i = pl.program_id(0)
    f = pl.program_id(1)
    used = i < n_used_ref[0]

    @pl.when(f == 0)
    def _():
        acc_sc[...] = jnp.zeros(acc_sc.shape, F32)

    @pl.when(used)
    def _():
        xb = x_ref[...].astype(BF16)
        hg = jnp.dot(xb, wg_ref[...], preferred_element_type=F32)
        hu = jnp.dot(xb, wu_ref[...], preferred_element_type=F32)
        h = hg * jax.nn.sigmoid(hg) * hu
        acc_sc[...] += jnp.dot(h.astype(BF16), wd_ref[...], preferred_element_type=F32)

    @pl.when(f == pl.num_programs(1) - 1)
    def _():
        o_ref[...] = acc_sc[...]


def _grouped_ffn(xs, tile_e, n_used, wg, wu, wd, tile):
    p, d = xs.shape
    ff = wg.shape[2]
    tf = _pick(ff, (512, 256, 128))
    row = lambda i, f, te, nu: (i, 0)
    return pl.pallas_call(
        _grouped_ffn_kernel,
        grid_spec=pltpu.PrefetchScalarGridSpec(
            num_scalar_prefetch=2, grid=(p // tile, ff // tf),
            in_specs=[pl.BlockSpec((tile, d), lambda i, f, te, nu: (jnp.minimum(i, nu[0] - 1), 0)),
                      pl.BlockSpec((None, d, tf), lambda i, f, te, nu: (te[i], 0, f)),
                      pl.BlockSpec((None, d, tf), lambda i, f, te, nu: (te[i], 0, f)),
                      pl.BlockSpec((None, tf, d), lambda i, f, te, nu: (te[i], f, 0))],
            out_specs=pl.BlockSpec((tile, d), row),
            scratch_shapes=[pltpu.VMEM((tile, d), F32)]),
        out_shape=jax.ShapeDtypeStruct((p, d), F32),
        compiler_params=_cparams(("parallel", "arbitrary")),
    )(tile_e, n_used, xs, wg, wu, wd)


def _combine_kernel(pos1_ref, pos2_ref, x_ref, rec_ref, g_ref, b_ref, ys_hbm, o_ref, ob_ref,
                    y1_sc, y2_sc, sem, *, alpha, tm):
    base = pl.program_id(0) * tm

    def issue(r, carry):
        t = base + r
        _row_copy(ys_hbm, pos1_ref[t], y1_sc, r, sem).start()
        _row_copy(ys_hbm, pos2_ref[t], y2_sc, r, sem).start()
        return carry

    lax.fori_loop(0, tm, issue, 0)

    def drain(r, carry):
        _row_copy(ys_hbm, 0, y1_sc, 0, sem).wait()
        _row_copy(ys_hbm, 0, y2_sc, 0, sem).wait()
        return carry

    lax.fori_loop(0, tm, drain, 0)
    rec = rec_ref[...]
    f = rec[:, 2:3] * y1_sc[...] + rec[:, 3:4] * y2_sc[...]
    z = _ln_rows(alpha * x_ref[...] + f, g_ref[...], b_ref[...])
    o_ref[...] = z
    ob_ref[...] = z.astype(ob_ref.dtype)


def _combine(x, rec, ys, pos1, pos2, g, b, alpha):
    m, d = x.shape
    tm = _pick(m, (256, 128))
    row = lambda i, p1, p2: (i, 0)
    fixed = lambda i, p1, p2: (0, 0)
    return pl.pallas_call(
        functools.partial(_combine_kernel, alpha=alpha, tm=tm),
        grid_spec=pltpu.PrefetchScalarGridSpec(
            num_scalar_prefetch=2, grid=(m // tm,),
            in_specs=[pl.BlockSpec((tm, d), row), pl.BlockSpec((tm, LANE), row),
                      pl.BlockSpec((1, d), fixed), pl.BlockSpec((1, d), fixed),
                      pl.BlockSpec(memory_space=pl.ANY)],
            out_specs=[pl.BlockSpec((tm, d), row), pl.BlockSpec((tm, d), row)],
            scratch_shapes=[pltpu.VMEM((tm, d), F32), pltpu.VMEM((tm, d), F32),
                            pltpu.SemaphoreType.DMA]),
        out_shape=[jax.ShapeDtypeStruct((m, d), F32), jax.ShapeDtypeStruct((m, d), BF16)],
        compiler_params=_cparams(("arbitrary",)),
    )(pos1, pos2, x, rec, g, b, ys)


MOE_TILE = 512


def _routed_moe(x, w_router, wg, wu, wd, g, b, alpha):
    m, d = x.shape
    ne = w_router.shape[1]
    rec, sel = _router(x, w_router)
    pos, meta = _slots(sel, rec, MOE_TILE, ne)
    pos1 = pos[:, 0].astype(jnp.int32)
    pos2 = pos[:, 1].astype(jnp.int32)
    n_tiles = (2 * m) // MOE_TILE + ne
    counts, starts = meta[0, :ne], meta[1, :ne]
    ends = starts + jnp.ceil(counts / MOE_TILE) * MOE_TILE
    tile_start = jnp.arange(n_tiles, dtype=F32) * MOE_TILE
    tile_e = jnp.minimum(jnp.sum(ends[None, :] <= tile_start[:, None], axis=1), ne - 1).astype(jnp.int32)
    n_used = (ends[ne - 1:ne] / MOE_TILE).astype(jnp.int32)
    pad_row = (starts + counts).astype(jnp.int32)
    pad_len = (ends - starts - counts).astype(jnp.int32)
    xs = _dispatch(x, pos1, pos2, pad_row, pad_len, n_tiles * MOE_TILE)
    ys = _grouped_ffn(xs, tile_e, n_used, wg, wu, wd, MOE_TILE)
    return _combine(x, rec, ys, pos1, pos2, g, b, alpha)


def _epi_scale(scale):
    return lambda accs, tabs, vecs: [accs[0] * scale]


def _epi_f32_and_bf16(accs, tabs, vecs):
    return [accs[0], accs[0]]


def _epi_rms(accs, tabs, vecs):
    r = _rms_rows(accs[0], vecs[0])
    return [r, r]


def _epi_rope(accs, tabs, vecs):
    return [accs[0] * tabs[0] + accs[1] * tabs[1]]


def _epi_gelu(accs, tabs, vecs):
    return [jax.nn.gelu(accs[0])]


def _epi_gelu_ln(accs, tabs, vecs):
    return [_ln_rows(jax.nn.gelu(accs[0]), vecs[0], vecs[1])]


def _epi_plain(accs, tabs, vecs):
    return [accs[0]]


def _epi_q_rope(scale):
    return lambda accs, tabs, vecs: [(accs[0] * tabs[0] + accs[1] * tabs[1]) * scale]


def _rope_tables(pos):
    half = B_ROPE // 2
    inv = ROPE_THETA ** (-jnp.arange(half, dtype=F32) / half)
    ang = pos.astype(F32)[:, None] * inv[None, :]
    cos, sin = jnp.cos(ang), jnp.sin(ang)
    cos2 = jnp.concatenate([cos, cos], axis=1)
    sin2 = jnp.concatenate([-sin, sin], axis=1)
    return cos2, sin2


def _swap_halves(w):
    half = w.shape[-1] // 2
    return jnp.concatenate([w[..., half:], w[..., :half]], axis=-1)


def kernel(x_prompt, x_sample, cache_a_k, cache_a_v, cache_b_latent, cache_b_krope, state_d_pool, w_in, a_lambda, a_subln, b_q_norm, b_w_uq, b_kv_norm, b_w_ukv, c_ln_g, c_ln_b, c_w_s, c_b_s, d_w_grp, d_scale, w_branch, w_out, ln1_g, ln1_b, ln2_g, ln2_b, f_w_gate, f_w_up, f_w_down, m_router, m_w_gate, m_w_up, m_w_down):
    depth = w_in.shape[0]
    nbp, lp, d_model = x_prompt.shape
    nbs, ls, _ = x_sample.shape
    past = cache_a_k.shape[2]
    lks = past + ls
    tp, ts = nbp * lp, nbs * ls
    aw = A_HEADS * 2 * A_DH
    q_rank = b_q_norm.shape[1]
    kv_rank = b_kv_norm.shape[1]
    c_width = c_ln_g.shape[1]
    d_width = d_scale.shape[1]
    alpha = (2.0 * depth) ** 0.25
    sizes = (aw, aw, aw, q_rank, kv_rank, B_ROPE, c_width, c_width, d_width) + (d_model,) * N_BRANCH
    offs = [0]
    for s in sizes:
        offs.append(offs[-1] + s)

    x = jnp.concatenate([x_prompt.reshape(tp, d_model), x_sample.reshape(ts, d_model)], axis=0)
    xb = x.astype(BF16)

    pos = jnp.concatenate([jnp.tile(jnp.arange(lp, dtype=jnp.int32), nbp),
                           jnp.tile(past + jnp.arange(ls, dtype=jnp.int32), nbs)])
    cos2, sin2 = _rope_tables(pos)
    rows = pos.shape[0]
    ones = jnp.ones((rows, B_NOPE), F32)
    zeros_n = jnp.zeros((rows, B_NOPE), F32)
    zeros_r = jnp.zeros((rows, 2 * LANE - B_NOPE - B_ROPE), F32)
    q_cos = jnp.concatenate([ones, cos2, zeros_r], axis=1)
    q_sin = jnp.concatenate([zeros_n, sin2, zeros_r], axis=1)

    slopes = jnp.exp2(-8.0 * jnp.arange(1, A_HEADS + 1, dtype=F32) / A_HEADS) * LOG2E
    slopes = jnp.broadcast_to(slopes[:, None, None], (A_HEADS, 1, LANE))

    states_p, states_s = [], []
    for l in range(depth):
        wl = w_in[l]
        seg = lambda k: wl[:, offs[k]:offs[k + 1]].astype(BF16)
        lam_init = 0.8 - 0.6 * math.exp(-0.3 * l)

        (q_a,) = _fused_mm(_epi_scale(A_DH ** -0.5 * LOG2E), xb, [seg(0)], [BF16], tn=1024)
        w_kv = jnp.concatenate([seg(1), seg(2)], axis=1)
        kv_a, kv_a_b = _fused_mm(_epi_f32_and_bf16, xb, [w_kv], [F32, BF16], tn=1024)
        subln = a_subln[l].reshape(1, 2 * A_DH)
        o_a_p = _diff_attn(q_a, kv_a_b, slopes, a_lambda[l], subln, nb=nbp, lq=lp, lk=lp, q_off=0,
                           q_row0=0, lam_init=lam_init)
        cache_kv = jnp.concatenate([cache_a_k[l].reshape(nbs, past, aw),
                                    cache_a_v[l].reshape(nbs, past, aw)], axis=2).astype(BF16)
        kv_s = jnp.concatenate([cache_kv, kv_a_b[tp:].reshape(nbs, ls, 2 * aw)], axis=1)
        o_a_s = _diff_attn(q_a, kv_s.reshape(nbs * lks, 2 * aw), slopes, a_lambda[l], subln, nb=nbs,
                           lq=ls, lk=lks, q_off=past, q_row0=tp, lam_init=lam_init)
        o_a = jnp.concatenate([o_a_p, o_a_s], axis=0)

        (cq,) = _fused_mm(_epi_rms, xb, [seg(3)], [BF16], tn=q_rank,
                          vecs=[b_q_norm[l].reshape(1, q_rank)])
        lat, lat_b = _fused_mm(_epi_rms, xb, [seg(4)], [F32, BF16], tn=kv_rank,
                               vecs=[b_kv_norm[l].reshape(1, kv_rank)])
        w_kr = wl[:, offs[5]:offs[6]]
        (kr,) = _fused_mm(_epi_rope, xb, [w_kr.astype(BF16), _swap_halves(w_kr).astype(BF16)], [F32],
                          tn=B_ROPE, tabs=[cos2, sin2])
        w_uq = b_w_uq[l].reshape(q_rank, B_HEADS, B_NOPE + B_ROPE)
        w_uq_n, w_uq_r = w_uq[..., :B_NOPE], w_uq[..., B_NOPE:]
        zpad = jnp.zeros((q_rank, B_HEADS, 2 * LANE - B_NOPE - B_ROPE), F32)
        w_q1 = jnp.concatenate([w_uq_n, w_uq_r, zpad], axis=-1).reshape(q_rank, -1).astype(BF16)
        w_q2 = jnp.concatenate([jnp.zeros_like(w_uq_n), _swap_halves(w_uq_r), zpad],
                               axis=-1).reshape(q_rank, -1).astype(BF16)
        (q_b,) = _fused_mm(_epi_q_rope((B_NOPE + B_ROPE) ** -0.5 * LOG2E), cq, [w_q1, w_q2], [BF16],
                           tn=2 * LANE, tabs=[q_cos, q_sin])
        w_ukv = b_w_ukv[l].reshape(kv_rank, B_HEADS, B_NOPE + B_VDIM)
        w_up = jnp.concatenate([w_ukv[..., :B_NOPE].reshape(kv_rank, -1),
                                w_ukv[..., B_NOPE:].reshape(kv_rank, -1)], axis=1).astype(BF16)
        lat_s = jnp.concatenate([cache_b_latent[l].astype(BF16), lat_b[tp:].reshape(nbs, ls, kv_rank)],
                                axis=1).reshape(nbs * lks, kv_rank)
        lat_cat = jnp.concatenate([lat_b[:tp], lat_s], axis=0)
        (kv_b,) = _fused_mm(_epi_plain, lat_cat, [w_up], [BF16], tn=1024)
        kr_s = jnp.concatenate([cache_b_krope[l], kr[tp:].reshape(nbs, ls, B_ROPE)],
                               axis=1).reshape(nbs * lks, B_ROPE)
        kr_cat = jnp.pad(jnp.concatenate([kr[:tp], kr_s], axis=0),
                         ((0, 0), (0, LANE - B_ROPE))).astype(BF16)
        o_b_p = _mla_attn(q_b, kv_b, kr_cat, nb=nbp, lq=lp, lk=lp, q_off=0, q_row0=0, kv_row0=0)
        if tp % lks:
            kv_b_s, kr_b_s, s_row0 = kv_b[tp:], kr_cat[tp:], 0
        else:
            kv_b_s, kr_b_s, s_row0 = kv_b, kr_cat, tp
        o_b_s = _mla_attn(q_b, kv_b_s, kr_b_s, nb=nbs, lq=ls, lk=lks, q_off=past, q_row0=tp,
                          kv_row0=s_row0)
        o_b = jnp.concatenate([o_b_p, o_b_s], axis=0)

        (u_c,) = _fused_mm(_epi_gelu, xb, [seg(6)], [BF16], tn=1024)
        (v_c,) = _fused_mm(_epi_gelu_ln, xb, [seg(7)], [F32], tn=c_width,
                           vecs=[c_ln_g[l].reshape(1, c_width), c_ln_b[l].reshape(1, c_width)])
        w_s, b_s = c_w_s[l], c_b_s[l]
        reps = C_BLOCK // ls if ls < C_BLOCK else 1
        w_tl = w_s[:, :C_BLOCK // reps, :C_BLOCK // reps]
        w_smp = jnp.zeros_like(w_s)
        for r in range(reps):
            sl = slice(r * (C_BLOCK // reps), (r + 1) * (C_BLOCK // reps))
            w_smp = w_smp.at[:, sl, sl].set(w_tl)
        b_smp = jnp.tile(b_s[:, :C_BLOCK // reps], (1, reps))
        w_sel = jnp.stack([w_s, w_smp]).astype(BF16)
        b_sel = jnp.stack([b_s, b_smp])[..., None]
        o_c = _spatial_gate(u_c, v_c, w_sel, b_sel, tp)

        (d_in,) = _fused_mm(_epi_plain, xb, [seg(8)], [F32], tn=1024)
        w_grp = d_w_grp[l].astype(BF16)
        scale = d_scale[l].reshape(1, d_width)
        hist_p = jnp.zeros((nbp, D_HALO, d_width), F32)
        hist_s = jnp.pad(state_d_pool[l], ((0, 0), (D_HALO - D_HIST, 0), (0, 0)))
        o_d_p = _pool(d_in, hist_p, w_grp, scale, nb=nbp, lq=lp, row0=0, past=0)
        o_d_s = _pool(d_in, hist_s, w_grp, scale, nb=nbs, lq=ls, row0=tp, past=past)
        o_d = jnp.concatenate([o_d_p, o_d_s], axis=0)

        w_gate = wl[:, offs[9]:offs[13]].astype(BF16)
        y = _merge(xb, w_gate, [o_a, o_b, o_c, o_d], w_branch[l].astype(BF16))
        x, xb = _out_ln(y, x, w_out[l].astype(BF16), ln1_g[l].reshape(1, -1), ln1_b[l].reshape(1, -1),
                        alpha)

        i = l // 2
        g2, b2 = ln2_g[l].reshape(1, -1), ln2_b[l].reshape(1, -1)
        if l % 2 == 0:
            x, xb = _ffn(xb, x, f_w_gate[i].astype(BF16), f_w_up[i].astype(BF16),
                         f_w_down[i].astype(BF16), g2, b2, alpha)
        else:
            x, xb = _routed_moe(x, m_router[i], m_w_gate[i].astype(BF16), m_w_up[i].astype(BF16),
                                m_w_down[i].astype(BF16), g2, b2, alpha)

        kv_p = kv_a[:tp].reshape(nbp, lp, 2, A_HEADS, 2 * A_DH)
        kv_smp = kv_a[tp:].reshape(nbs, ls, 2, A_HEADS, 2 * A_DH)
        states_p.append((kv_p[:, :, 0], kv_p[:, :, 1], lat[:tp].reshape(nbp, lp, kv_rank),
                         kr[:tp].reshape(nbp, lp, B_ROPE),
                         d_in[:tp].reshape(nbp, lp, d_width)[:, lp - D_HIST:]))
        states_s.append((kv_smp[:, :, 0], kv_smp[:, :, 1], lat[tp:].reshape(nbs, ls, kv_rank),
                         kr[tp:].reshape(nbs, ls, B_ROPE), v_c[tp:].reshape(nbs, ls, c_width),
                         jnp.concatenate([state_d_pool[l], d_in[tp:].reshape(nbs, ls, d_width)],
                                         axis=1)[:, -D_HIST:]))

    yp = x[:tp].reshape(nbp, lp, d_model)
    ys = x[tp:].reshape(nbs, ls, d_model)
    stack = lambda sts, k: jnp.stack([s[k] for s in sts])
    return (yp, ys, stack(states_p, 0), stack(states_p, 1), stack(states_p, 2), stack(states_p, 3),
            stack(states_p, 4), stack(states_s, 0), stack(states_s, 1), stack(states_s, 2),
            stack(states_s, 3), stack(states_s, 4), stack(states_s, 5))
```

```python
import functools
import math

import jax
import jax.numpy as jnp
from jax import lax
from jax.experimental import pallas as pl
from jax.experimental.pallas import tpu as pltpu

CHUNK = 64
A_HEADS = 8
A_DH = 64
B_HEADS = 8
B_NOPE = 128
B_ROPE = 64
B_VDIM = 128
C_GROUPS = 4
C_BLOCK = 128
D_WINDOWS = (2, 4, 8, 16)
D_HIST = 15
D_HALO = 16
N_BRANCH = 4
ROPE_THETA = 10000.0
LN_EPS = 1e-5
RMS_EPS = 1e-6
LOG2E = 1.4426950408889634
NEG = -1e30
LANE = 128
VMEM_LIMIT = 56 * 1024 * 1024

F32 = jnp.float32
BF16 = jnp.bfloat16


def _pick(n, cands):
    for c in cands:
        if n % c == 0:
            return c
    raise ValueError(f"no tile for {n} in {cands}")


def _cparams(sem):
    return pltpu.CompilerParams(dimension_semantics=sem, vmem_limit_bytes=VMEM_LIMIT)


def _ln_rows(z, g, b):
    mu = jnp.mean(z, axis=-1, keepdims=True)
    zc = z - mu
    var = jnp.mean(zc * zc, axis=-1, keepdims=True)
    return zc * lax.rsqrt(var + LN_EPS) * g + b


def _rms_rows(z, g):
    return z * lax.rsqrt(jnp.mean(z * z, axis=-1, keepdims=True) + RMS_EPS) * g


def _fused_mm_kernel(*refs, fn, nw, ntab, nvec, precision):
    x = refs[0][...]
    w_refs = refs[1:1 + nw]
    tab_refs = refs[1 + nw:1 + nw + ntab]
    vec_refs = refs[1 + nw + ntab:1 + nw + ntab + nvec]
    out_refs = refs[1 + nw + ntab + nvec:]
    accs = [jnp.dot(x, w[...], preferred_element_type=F32, precision=precision) for w in w_refs]
    res = fn(accs, [t[...] for t in tab_refs], [v[...] for v in vec_refs])
    for o, r in zip(out_refs, res):
        o[...] = r.astype(o.dtype)


def _fused_mm(fn, x, ws, out_dtypes, *, tn, tm=None, tabs=(), tabs_follow_cols=False, vecs=(),
              precision=None):
    m, kdim = x.shape
    n = ws[0].shape[1]
    tm = tm or _pick(m, (1024, 512, 256, 128))
    grid = (m // tm, n // tn)
    in_specs = [pl.BlockSpec((tm, kdim), lambda i, j: (i, 0))]
    in_specs += [pl.BlockSpec((kdim, tn), lambda i, j: (0, j)) for _ in ws]
    for t in tabs:
        if tabs_follow_cols:
            in_specs.append(pl.BlockSpec((tm, tn), lambda i, j: (i, j)))
        else:
            in_specs.append(pl.BlockSpec((tm, t.shape[1]), lambda i, j: (i, 0)))
    in_specs += [pl.BlockSpec((1, tn), lambda i, j: (0, j)) for _ in vecs]
    out_specs = [pl.BlockSpec((tm, tn), lambda i, j: (i, j)) for _ in out_dtypes]
    out_shape = [jax.ShapeDtypeStruct((m, n), d) for d in out_dtypes]
    kern = functools.partial(_fused_mm_kernel, fn=fn, nw=len(ws), ntab=len(tabs), nvec=len(vecs),
                             precision=precision)
    return pl.pallas_call(
        kern, grid=grid, in_specs=in_specs, out_specs=out_specs, out_shape=out_shape,
        compiler_params=_cparams(("parallel", "arbitrary")),
    )(x, *ws, *tabs, *vecs)


def _state_proj_kernel(*refs, n_col, n_prompt_tiles, has_gain, has_prev):
    x_ref, w_ref = refs[0], refs[1]
    pos = 2
    g_ref = refs[pos] if has_gain else None
    pos += (1 if has_gain else 0) + (2 * n_col if has_prev else 0)
    joint_ref = refs[pos]
    state_refs = refs[pos + 1:]
    i = pl.program_id(0)
    j = pl.program_id(1)
    acc = jnp.dot(x_ref[...], w_ref[...], preferred_element_type=F32)
    if has_gain:
        acc = _rms_rows(acc, g_ref[...])
    joint_ref[...] = acc.astype(joint_ref.dtype)
    for c in range(n_col):
        @pl.when((j == c) & (i < n_prompt_tiles))
        def _(c=c):
            state_refs[2 * c][...] = acc

        @pl.when((j == c) & (i >= n_prompt_tiles))
        def _(c=c):
            state_refs[2 * c + 1][...] = acc


def _state_proj(x, w, gain, *, tn, layer, depth, n_prompt_rows, prev):
    m, kdim = x.shape
    n = w.shape[1]
    n_col = n // tn
    tm = math.gcd(_pick(m, (512, 256, 128)), n_prompt_rows)
    n_p = n_prompt_rows // tm
    n_s = (m - n_prompt_rows) // tm
    in_specs = [pl.BlockSpec((tm, kdim), lambda i, j: (i, 0)), pl.BlockSpec((kdim, tn), lambda i, j: (0, j))]
    args = [x, w]
    if gain is not None:
        in_specs.append(pl.BlockSpec((1, tn), lambda i, j: (0, j)))
        args.append(gain)
    aliases = {}
    if prev is not None:
        for k, buf in enumerate(prev):
            aliases[len(args)] = 1 + k
            in_specs.append(pl.BlockSpec(memory_space=pl.ANY))
            args.append(buf)
    out_specs = [pl.BlockSpec((tm, tn), lambda i, j: (i, j))]
    out_shape = [jax.ShapeDtypeStruct((m, n), BF16)]
    for _ in range(n_col):
        out_specs.append(pl.BlockSpec((tm, tn), lambda i, j: (layer * n_p + jnp.minimum(i, n_p - 1), 0)))
        out_shape.append(jax.ShapeDtypeStruct((depth * n_p * tm, tn), F32))
        out_specs.append(pl.BlockSpec((tm, tn), lambda i, j: (layer * n_s + jnp.maximum(i - n_p, 0), 0)))
        out_shape.append(jax.ShapeDtypeStruct((depth * n_s * tm, tn), F32))
    kern = functools.partial(_state_proj_kernel, n_col=n_col, n_prompt_tiles=n_p,
                             has_gain=gain is not None, has_prev=prev is not None)
    outs = pl.pallas_call(
        kern, grid=(m // tm, n_col), in_specs=in_specs, out_specs=out_specs, out_shape=out_shape,
        input_output_aliases=aliases, compiler_params=_cparams(("arbitrary", "arbitrary")),
    )(*args)
    return outs[0], list(outs[1:])


ATTN_ROW_GROUPS = 2
DIFF_ATTN_TQ = 512
MLA_ATTN_TQ = 1024


def _row_groups(rows):
    g = ATTN_ROW_GROUPS
    while rows % (16 * g):
        g //= 2
    return g


def _nt_dot(a, b):
    return lax.dot_general(a, b, (((1,), (1,)), ((), ())), preferred_element_type=F32)


def _softmax_pv(scores, v, biases, m_sc, acc_sc):
    width = v.shape[0]
    v_ext = jnp.concatenate([v, jnp.ones(v.shape, v.dtype)], axis=1)
    r0 = 0
    for s, bias in zip(scores, biases):
        rows = slice(r0, r0 + s.shape[0])
        r0 += s.shape[0]
        if bias is not None:
            s = s + bias
        m_prev = m_sc[rows, :]
        m_new = jnp.maximum(m_prev, jnp.max(s, axis=1, keepdims=True))
        alpha = jnp.exp2(m_prev - m_new)
        m_rep = jnp.tile(m_new, (1, width // LANE)) if width >= LANE else m_new[:, :width]
        p = jnp.exp2(s - m_rep)
        pv = jnp.dot(p.astype(BF16), v_ext, preferred_element_type=F32)
        acc_sc[rows, :] = jnp.tile(alpha, (1, 2)) * acc_sc[rows, :] + pv
        m_sc[rows, :] = m_new


def _sweep_keys(q_groups, keys, v_ref, off_biases, diag_biases, m_sc, acc_sc, s_sc, *, q0, tq, tk):
    m_sc[...] = jnp.full(m_sc.shape, NEG, F32)
    acc_sc[...] = jnp.zeros(acc_sc.shape, F32)
    n_off = q0 // tk
    kd = pl.multiple_of(q0, tq)

    def scores(start, size):
        k = keys(start, size)
        return [_nt_dot(q, k) for q in q_groups()]

    if s_sc is None:
        def body(j, carry):
            ks = pl.multiple_of(j * tk, tk)
            _softmax_pv(scores(ks, tk), v_ref[pl.ds(ks, tk), :], off_biases(ks), m_sc, acc_sc)
            return carry

        lax.fori_loop(0, n_off, body, 0)
        _softmax_pv(scores(kd, tq), v_ref[pl.ds(kd, tq), :], diag_biases, m_sc, acc_sc)
        return

    assert tq == tk
    bounds = []
    r0 = 0
    for q in q_groups():
        bounds.append(slice(r0, r0 + q.shape[0]))
        r0 += q.shape[0]

    def stash(new_scores):
        for rows, s in zip(bounds, new_scores):
            s_sc[rows, :] = s

    def staged():
        return [s_sc[rows, :] for rows in bounds]

    stash(scores(0, tk))

    def body(j, carry):
        ks = pl.multiple_of(j * tk, tk)
        nxt = scores(pl.multiple_of(ks + tk, tk), tk)
        _softmax_pv(staged(), v_ref[pl.ds(ks, tk), :], off_biases(ks), m_sc, acc_sc)
        stash(nxt)
        return carry

    lax.fori_loop(0, n_off, body, 0)
    _softmax_pv(staged(), v_ref[pl.ds(kd, tq), :], diag_biases, m_sc, acc_sc)


def _chunk_mask_bias(q0, tq, slope):
    qpos = q0 + lax.broadcasted_iota(jnp.int32, (tq, tq), 0)
    kpos = q0 + lax.broadcasted_iota(jnp.int32, (tq, tq), 1)
    visible = (kpos >> 6) <= (qpos >> 6)
    if slope is None:
        return jnp.where(visible, 0.0, NEG)
    rel = (qpos - q0 - jnp.abs(qpos - kpos)).astype(F32)
    return jnp.where(visible, slope * rel, NEG)


def _diff_attn_head(qi, q_ref, k_ref, v_ref, slope, lam_ref, g_ref, o_ref,
                    qs_sc, m_sc, acc_sc, s_sc, *, tq, tk, q_off, lam_init):
    q0 = q_off + qi * tq
    q = q_ref[...]
    lane = lax.broadcasted_iota(jnp.int32, q.shape, 1)
    zero = jnp.zeros_like(q)
    qs_sc[0:tq, :] = jnp.where(lane < A_DH, q, zero)
    qs_sc[tq:2 * tq, :] = jnp.where(lane >= A_DH, q, zero)
    groups = _row_groups(2 * tq)
    gr = 2 * tq // groups

    def q_groups():
        return [qs_sc[g * gr:(g + 1) * gr, :] for g in range(groups)]

    def off_biases(ks):
        kpos = ks + lax.broadcasted_iota(jnp.int32, (1, tk), 1)
        return [slope * (kpos - q0).astype(F32)] * groups

    bias = _chunk_mask_bias(q0, tq, slope)
    if gr > tq:
        diag_biases = [jnp.concatenate([bias] * (gr // tq), axis=0)] * groups
    else:
        diag_biases = [bias[(g * gr) % tq:(g * gr) % tq + gr, :] for g in range(groups)]
    _sweep_keys(q_groups, lambda ks, size: k_ref[pl.ds(ks, size), :], v_ref, off_biases, diag_biases,
                m_sc, acc_sc, s_sc, q0=q0, tq=tq, tk=tk)

    lp = lam_ref[...]
    lam = (jnp.exp(jnp.sum(lp[0:1, :] * lp[1:2, :], axis=1, keepdims=True))
           - jnp.exp(jnp.sum(lp[2:3, :] * lp[3:4, :], axis=1, keepdims=True)) + lam_init)
    acc = acc_sc[...]
    o1 = acc[0:tq, 0:LANE] / acc[0:tq, LANE:2 * LANE]
    o2 = acc[tq:2 * tq, 0:LANE] / acc[tq:2 * tq, LANE:2 * LANE]
    o = _rms_rows(o1 - lam * o2, g_ref[...]) * (1.0 - lam_init)
    o_ref[...] = o.astype(o_ref.dtype)


def _head_cols(ref, h, width=LANE):
    return ref.at[:, h * width:(h + 1) * width]


def _diff_attn_kernel(q_ref, k_ref, v_ref, slope_ref, lam_ref, g_ref, o_ref,
                      qs_sc, m_sc, acc_sc, s_sc, **static):
    _diff_attn_head(pl.program_id(2), q_ref, k_ref, v_ref, slope_ref[...][:, 0:1], lam_ref, g_ref,
                    o_ref, qs_sc, m_sc, acc_sc, s_sc, **static)


def _diff_attn_heads_kernel(q_ref, kv_ref, slope_ref, lam_ref, g_ref, o_ref,
                            qs_sc, m_sc, acc_sc, **static):
    for h in range(A_HEADS):
        _diff_attn_head(pl.program_id(1), _head_cols(q_ref, h), _head_cols(kv_ref, h),
                        _head_cols(kv_ref, A_HEADS + h), slope_ref[h][:, 0:1], lam_ref, g_ref,
                        _head_cols(o_ref, h), qs_sc, m_sc, acc_sc, None, **static)


def _mla_attn_head(qi, q_ref, kn_ref, kr_ref, v_ref, o_ref, m_sc, acc_sc, s_sc, *, tq, tk, q_off):
    q0 = q_off + qi * tq
    groups = _row_groups(tq)
    gr = tq // groups

    def q_groups():
        return [q_ref[g * gr:(g + 1) * gr, :] for g in range(groups)]

    def keys(ks, size):
        return jnp.concatenate([kn_ref[pl.ds(ks, size), :], kr_ref[pl.ds(ks, size), :]], axis=1)

    mask = _chunk_mask_bias(q0, tq, None)
    diag_biases = [mask[g * gr:(g + 1) * gr, :] for g in range(groups)]
    _sweep_keys(q_groups, keys, v_ref, lambda ks: [None] * groups, diag_biases, m_sc, acc_sc,
                s_sc, q0=q0, tq=tq, tk=tk)
    acc = acc_sc[...]
    o_ref[...] = (acc[:, 0:LANE] / acc[:, LANE:2 * LANE]).astype(o_ref.dtype)


def _mla_attn_kernel(q_ref, kn_ref, kr_ref, v_ref, o_ref, m_sc, acc_sc, s_sc, **static):
    _mla_attn_head(pl.program_id(2), q_ref, kn_ref, kr_ref, v_ref, o_ref, m_sc, acc_sc, s_sc, **static)


def _mla_attn_heads_kernel(q_ref, kv_ref, kr_ref, o_ref, m_sc, acc_sc, **static):
    for h in range(B_HEADS):
        _mla_attn_head(pl.program_id(1), _head_cols(q_ref, h, 2 * LANE), _head_cols(kv_ref, h), kr_ref,
                       _head_cols(kv_ref, B_HEADS + h), _head_cols(o_ref, h), m_sc, acc_sc, None,
                       **static)


def _attn_tiles(lq, q_off, tq_max):
    sizes = tuple(t for t in (1024, 512, 256, 128, 64) if t <= tq_max)
    tq = _pick(lq, sizes)
    tk = tq if q_off == 0 else _pick(q_off, (512, 256, 128, 64))
    return tq, tk


def _diff_attn(q, kv, slopes, lam_p, subln, *, nb, lq, lk, q_off, q_row0, lam_init):
    tq, tk = _attn_tiles(lq, q_off, DIFF_ATTN_TQ)
    nq = lq // tq
    qb0 = q_row0 // tq
    static = dict(tq=tq, tk=tk, q_off=q_off, lam_init=lam_init)
    width = A_HEADS * LANE
    out_shape = jax.ShapeDtypeStruct((nb * lq, width), BF16)
    scratch = [pltpu.VMEM((2 * tq, LANE), BF16), pltpu.VMEM((2 * tq, LANE), F32),
               pltpu.VMEM((2 * tq, 2 * LANE), F32)]
    if tq != tk:
        return pl.pallas_call(
            functools.partial(_diff_attn_heads_kernel, **static), grid=(nb, nq),
            in_specs=[
                pl.BlockSpec((tq, width), lambda b, i: (qb0 + b * nq + i, 0)),
                pl.BlockSpec((lk, 2 * width), lambda b, i: (b, 0)),
                pl.BlockSpec((A_HEADS, 1, LANE), lambda b, i: (0, 0, 0)),
                pl.BlockSpec((4, A_DH), lambda b, i: (0, 0)),
                pl.BlockSpec((1, LANE), lambda b, i: (0, 0)),
            ],
            out_specs=pl.BlockSpec((tq, width), lambda b, i: (b * nq + i, 0)),
            out_shape=out_shape, scratch_shapes=scratch,
            compiler_params=_cparams(("parallel", "arbitrary")),
        )(q, kv, slopes, lam_p, subln)
    return pl.pallas_call(
        functools.partial(_diff_attn_kernel, **static), grid=(nb, A_HEADS, nq),
        in_specs=[
            pl.BlockSpec((tq, LANE), lambda b, h, i: (qb0 + b * nq + i, h)),
            pl.BlockSpec((lk, LANE), lambda b, h, i: (b, h)),
            pl.BlockSpec((lk, LANE), lambda b, h, i: (b, A_HEADS + h)),
            pl.BlockSpec((None, 1, LANE), lambda b, h, i: (h, 0, 0)),
            pl.BlockSpec((4, A_DH), lambda b, h, i: (0, 0)),
            pl.BlockSpec((1, LANE), lambda b, h, i: (0, 0)),
        ],
        out_specs=pl.BlockSpec((tq, LANE), lambda b, h, i: (b * nq + i, h)),
        out_shape=out_shape, scratch_shapes=scratch + [pltpu.VMEM((2 * tq, tk), F32)],
        compiler_params=_cparams(("parallel", "parallel", "arbitrary")),
    )(q, kv, kv, slopes, lam_p, subln)


def _mla_attn(q, kv, kr, *, nb, lq, lk, q_off, q_row0, kv_row0):
    tq, tk = _attn_tiles(lq, q_off, MLA_ATTN_TQ)
    nq = lq // tq
    qb0 = q_row0 // tq
    kb0 = kv_row0 // lk
    static = dict(tq=tq, tk=tk, q_off=q_off)
    width = B_HEADS * LANE
    out_shape = jax.ShapeDtypeStruct((nb * lq, width), BF16)
    scratch = [pltpu.VMEM((tq, LANE), F32), pltpu.VMEM((tq, 2 * LANE), F32)]
    if tq != tk:
        return pl.pallas_call(
            functools.partial(_mla_attn_heads_kernel, **static), grid=(nb, nq),
            in_specs=[
                pl.BlockSpec((tq, 2 * width), lambda b, i: (qb0 + b * nq + i, 0)),
                pl.BlockSpec((lk, 2 * width), lambda b, i: (kb0 + b, 0)),
                pl.BlockSpec((lk, LANE), lambda b, i: (kb0 + b, 0)),
            ],
            out_specs=pl.BlockSpec((tq, width), lambda b, i: (b * nq + i, 0)),
            out_shape=out_shape, scratch_shapes=scratch,
            compiler_params=_cparams(("parallel", "arbitrary")),
        )(q, kv, kr)
    return pl.pallas_call(
        functools.partial(_mla_attn_kernel, **static), grid=(nb, B_HEADS, nq),
        in_specs=[
            pl.BlockSpec((tq, 2 * LANE), lambda b, h, i: (qb0 + b * nq + i, h)),
            pl.BlockSpec((lk, LANE), lambda b, h, i: (kb0 + b, h)),
            pl.BlockSpec((lk, LANE), lambda b, h, i: (kb0 + b, 0)),
            pl.BlockSpec((lk, LANE), lambda b, h, i: (kb0 + b, B_HEADS + h)),
        ],
        out_specs=pl.BlockSpec((tq, LANE), lambda b, h, i: (b * nq + i, h)),
        out_shape=out_shape, scratch_shapes=scratch + [pltpu.VMEM((tq, tk), F32)],
        compiler_params=_cparams(("parallel", "parallel", "arbitrary")),
    )(q, kv, kr, kv)


def _spatial_gate_kernel(u_ref, v_ref, w_ref, b_ref, o_ref, *, n_chunks):
    gw = u_ref.shape[1] // C_GROUPS
    row = lax.broadcasted_iota(jnp.int32, (C_BLOCK, C_BLOCK), 0)
    col = lax.broadcasted_iota(jnp.int32, (C_BLOCK, C_BLOCK), 1)
    for g in range(C_GROUPS):
        w = jnp.where(row >= col, w_ref[g], jnp.zeros((C_BLOCK, C_BLOCK), BF16))
        bias = b_ref[g]
        for c in range(n_chunks):
            rows = slice(c * C_BLOCK, (c + 1) * C_BLOCK)
            cols = slice(g * gw, (g + 1) * gw)
            vv = v_ref[rows, cols].astype(BF16)
            sg = jnp.dot(w, vv, preferred_element_type=F32) + bias
            o_ref[rows, cols] = (u_ref[rows, cols].astype(F32) * sg).astype(o_ref.dtype)


def _spatial_gate(u, v, w_sel, b_sel, n_prompt_rows):
    m, width = u.shape
    tm = _pick(m, (1024, 512, 256, 128))
    tm = math.gcd(tm, n_prompt_rows)
    npt = n_prompt_rows // tm
    sel = lambda i: jnp.where(i >= npt, 1, 0)
    kern = functools.partial(_spatial_gate_kernel, n_chunks=tm // C_BLOCK)
    return pl.pallas_call(
        kern, grid=(m // tm,),
        in_specs=[
            pl.BlockSpec((tm, width), lambda i: (i, 0)),
            pl.BlockSpec((tm, width), lambda i: (i, 0)),
            pl.BlockSpec((None, C_GROUPS, C_BLOCK, C_BLOCK), lambda i: (sel(i), 0, 0, 0)),
            pl.BlockSpec((None, C_GROUPS, C_BLOCK, 1), lambda i: (sel(i), 0, 0, 0)),
        ],
        out_specs=pl.BlockSpec((tm, width), lambda i: (i, 0)),
        out_shape=jax.ShapeDtypeStruct((m, width), BF16),
        compiler_params=_cparams(("parallel",)),
    )(u, v, w_sel, b_sel)


def _pool_kernel(x_ref, prev_ref, hist_ref, w_ref, sc_ref, o_ref, xp_sc, *, tm, past):
    i = pl.program_id(1)
    xp_sc[0:D_HALO, :] = jnp.where(i == 0, hist_ref[...], prev_ref[...])
    xp_sc[D_HALO:D_HALO + tm, :] = x_ref[...]
    gw = x_ref.shape[1] // len(D_WINDOWS)
    pos = past + i * tm + lax.broadcasted_iota(jnp.int32, (tm, 1), 0)
    for gi, win in enumerate(D_WINDOWS):
        cols = slice(gi * gw, (gi + 1) * gw)
        tot = xp_sc[D_HALO:D_HALO + tm, cols]
        for j in range(1, win):
            tot = tot + xp_sc[D_HALO - j:D_HALO - j + tm, cols]
        cnt = jnp.minimum(pos + 1, win).astype(F32)
        pooled = tot / cnt - xp_sc[D_HALO:D_HALO + tm, cols]
        y = jnp.dot(pooled.astype(BF16), w_ref[gi], preferred_element_type=F32)
        o_ref[:, cols] = (y * sc_ref[:, cols]).astype(o_ref.dtype)


def _pool(d_in, hist, w_grp, scale, *, nb, lq, row0, past):
    width = d_in.shape[1]
    tm = _pick(lq, (512, 256, 128, 64))
    nt = lq // tm
    rb0 = row0 // tm
    hb0 = row0 // D_HALO
    per = tm // D_HALO
    kern = functools.partial(_pool_kernel, tm=tm, past=past)
    return pl.pallas_call(
        kern, grid=(nb, nt),
        in_specs=[
            pl.BlockSpec((tm, width), lambda b, i: (rb0 + b * nt + i, 0)),
            pl.BlockSpec((D_HALO, width),
                         lambda b, i: (jnp.maximum(hb0 + (b * nt + i) * per - 1, 0), 0)),
            pl.BlockSpec((None, D_HALO, width), lambda b, i: (b, 0, 0)),
            pl.BlockSpec(w_grp.shape, lambda b, i: (0, 0, 0)),
            pl.BlockSpec((1, width), lambda b, i: (0, 0)),
        ],
        out_specs=pl.BlockSpec((tm, width), lambda b, i: (b * nt + i, 0)),
        out_shape=jax.ShapeDtypeStruct((nb * lq, width), BF16),
        scratch_shapes=[pltpu.VMEM((D_HALO + tm, width), F32)],
        compiler_params=_cparams(("parallel", "arbitrary")),
    )(d_in, d_in, hist, w_grp, scale)


def _merge_kernel(x_ref, wg0, wg1, wg2, wg3, b0, b1, b2, b3, wb_ref, o_ref):
    x = x_ref[...]
    y = None
    for n, (wg, br) in enumerate(((wg0, b0), (wg1, b1), (wg2, b2), (wg3, b3))):
        gate = jax.nn.sigmoid(jnp.dot(x, wg[...], preferred_element_type=F32))
        t = gate * jnp.dot(br[...], wb_ref[n], preferred_element_type=F32)
        y = t if y is None else y + t
    o_ref[...] = y.astype(o_ref.dtype)


def _merge(xb, w_gate, branches, w_branch):
    m, d = xb.shape
    bw = branches[0].shape[1]
    tm = _pick(m, (512, 256, 128))
    tn = 512
    nj = d // tn
    in_specs = [pl.BlockSpec((tm, d), lambda i, j: (i, 0))]
    in_specs += [pl.BlockSpec((d, tn), lambda i, j, n=n: (0, n * nj + j)) for n in range(N_BRANCH)]
    in_specs += [pl.BlockSpec((tm, bw), lambda i, j: (i, 0)) for _ in range(N_BRANCH)]
    in_specs += [pl.BlockSpec((N_BRANCH, bw, tn), lambda i, j: (0, 0, j))]
    return pl.pallas_call(
        _merge_kernel, grid=(m // tm, nj), in_specs=in_specs,
        out_specs=pl.BlockSpec((tm, tn), lambda i, j: (i, j)),
        out_shape=jax.ShapeDtypeStruct((m, d), BF16),
        compiler_params=_cparams(("parallel", "arbitrary")),
    )(xb, w_gate, w_gate, w_gate, w_gate, *branches, w_branch)


def _out_ln_kernel(y_ref, x_ref, w_ref, g_ref, b_ref, o_ref, ob_ref, *, alpha):
    m = jnp.dot(y_ref[...], w_ref[...], preferred_element_type=F32)
    z = _ln_rows(alpha * x_ref[...] + m, g_ref[...], b_ref[...])
    o_ref[...] = z
    ob_ref[...] = z.astype(ob_ref.dtype)


def _out_ln(y, x, w_out, g, b, alpha):
    m, d = x.shape
    tm = _pick(m, (256, 128))
    row = lambda i: (i, 0)
    fixed = lambda i: (0, 0)
    return pl.pallas_call(
        functools.partial(_out_ln_kernel, alpha=alpha), grid=(m // tm,),
        in_specs=[pl.BlockSpec((tm, d), row), pl.BlockSpec((tm, d), row), pl.BlockSpec((d, d), fixed),
                  pl.BlockSpec((1, d), fixed), pl.BlockSpec((1, d), fixed)],
        out_specs=[pl.BlockSpec((tm, d), row), pl.BlockSpec((tm, d), row)],
        out_shape=[jax.ShapeDtypeStruct((m, d), F32), jax.ShapeDtypeStruct((m, d), BF16)],
        compiler_params=_cparams(("parallel",)),
    )(y, x, w_out, g, b)


def _ffn_kernel(xb_ref, x_ref, wg_ref, wu_ref, wd_ref, g_ref, b_ref, o_ref, ob_ref, acc_sc, *, alpha):
    f = pl.program_id(1)

    @pl.when(f == 0)
    def _():
        acc_sc[...] = jnp.zeros(acc_sc.shape, F32)

    xb = xb_ref[...]
    hg = jnp.dot(xb, wg_ref[...], preferred_element_type=F32)
    hu = jnp.dot(xb, wu_ref[...], preferred_element_type=F32)
    h = hg * jax.nn.sigmoid(hg) * hu
    acc_sc[...] += jnp.dot(h.astype(BF16), wd_ref[...], preferred_element_type=F32)

    @pl.when(f == pl.num_programs(1) - 1)
    def _():
        z = _ln_rows(alpha * x_ref[...] + acc_sc[...], g_ref[...], b_ref[...])
        o_ref[...] = z
        ob_ref[...] = z.astype(ob_ref.dtype)


def _ffn(xb, x, wg, wu, wd, g, b, alpha):
    m, d = x.shape
    ff = wg.shape[1]
    tm = _pick(m, (512, 256, 128))
    tf = _pick(ff, (512, 256, 128))
    row = lambda i, f: (i, 0)
    fixed = lambda i, f: (0, 0)
    return pl.pallas_call(
        functools.partial(_ffn_kernel, alpha=alpha), grid=(m // tm, ff // tf),
        in_specs=[pl.BlockSpec((tm, d), row), pl.BlockSpec((tm, d), row),
                  pl.BlockSpec((d, tf), lambda i, f: (0, f)), pl.BlockSpec((d, tf), lambda i, f: (0, f)),
                  pl.BlockSpec((tf, d), lambda i, f: (f, 0)),
                  pl.BlockSpec((1, d), fixed), pl.BlockSpec((1, d), fixed)],
        out_specs=[pl.BlockSpec((tm, d), row), pl.BlockSpec((tm, d), row)],
        out_shape=[jax.ShapeDtypeStruct((m, d), F32), jax.ShapeDtypeStruct((m, d), BF16)],
        scratch_shapes=[pltpu.VMEM((tm, d), F32)],
        compiler_params=_cparams(("parallel", "arbitrary")),
    )(xb, x, wg, wu, wd, g, b)


def _router_kernel(x_ref, w_ref, o_ref, sel_ref, *, n_experts):
    logits = jnp.dot(x_ref[...], w_ref[...], preferred_element_type=F32,
                     precision=lax.Precision.HIGHEST)
    lane = lax.broadcasted_iota(jnp.int32, logits.shape, 1).astype(F32)
    lg = jnp.where(lane < n_experts, logits, NEG)
    m1 = jnp.max(lg, axis=1, keepdims=True)
    i1 = jnp.min(jnp.where(lg == m1, lane, float(LANE)), axis=1, keepdims=True)
    lg2 = jnp.where(lane == i1, NEG, lg)
    m2 = jnp.max(lg2, axis=1, keepdims=True)
    i2 = jnp.min(jnp.where(lg2 == m2, lane, float(LANE)), axis=1, keepdims=True)
    ex = jnp.exp(m2 - m1)
    p1 = 1.0 / (1.0 + ex)
    p2 = ex / (1.0 + ex)
    o_ref[...] = (jnp.where(lane == 0.0, i1, 0.0) + jnp.where(lane == 1.0, i2, 0.0)
                  + jnp.where(lane == 2.0, p1, 0.0) + jnp.where(lane == 3.0, p2, 0.0))
    sel_ref[...] = jnp.where((lane == i1) | (lane == i2), 1.0, 0.0).astype(sel_ref.dtype)


def _router(x, w_router):
    m, d = x.shape
    ne = w_router.shape[1]
    w = jnp.pad(w_router, ((0, 0), (0, LANE - ne)))
    tm = _pick(m, (512, 256, 128))
    row = lambda i: (i, 0)
    return pl.pallas_call(
        functools.partial(_router_kernel, n_experts=ne), grid=(m // tm,),
        in_specs=[pl.BlockSpec((tm, d), row), pl.BlockSpec((d, LANE), lambda i: (0, 0))],
        out_specs=[pl.BlockSpec((tm, LANE), row), pl.BlockSpec((tm, LANE), row)],
        out_shape=[jax.ShapeDtypeStruct((m, LANE), F32), jax.ShapeDtypeStruct((m, LANE), BF16)],
        compiler_params=_cparams(("parallel",)),
    )(x, w)


def _slot_kernel(sel_ref, rec_ref, pos_ref, meta_ref, cnt_sc, off_sc, *, tile, n_experts):
    phase = pl.program_id(0)
    i = pl.program_id(1)
    sel = sel_ref[...]
    tm = sel.shape[0]
    lane = lax.broadcasted_iota(jnp.int32, (1, LANE), 1)

    @pl.when((phase == 0) & (i == 0))
    def _():
        cnt_sc[...] = jnp.zeros(cnt_sc.shape, F32)

    @pl.when(phase == 0)
    def _():
        cnt_sc[...] += jnp.sum(sel.astype(F32), axis=0, keepdims=True)

    @pl.when((phase == 1) & (i == 0))
    def _():
        counts = cnt_sc[...]
        padded = jnp.ceil(counts / tile) * tile
        starts = jnp.zeros((1, LANE), F32)
        for e in range(1, n_experts):
            before = jnp.sum(jnp.where(lane < e, padded, 0.0), axis=1, keepdims=True)
            starts = jnp.where(lane == e, before, starts)
        off_sc[...] = starts
        row = lax.broadcasted_iota(jnp.int32, meta_ref.shape, 0)
        meta_ref[...] = jnp.where(row == 0, counts, jnp.where(row == 1, starts, 0.0))
        cnt_sc[...] = jnp.zeros(cnt_sc.shape, F32)

    @pl.when(phase == 1)
    def _():
        r = lax.broadcasted_iota(jnp.int32, (tm, tm), 0)
        c = lax.broadcasted_iota(jnp.int32, (tm, tm), 1)
        tri = jnp.where(r >= c, 1.0, 0.0).astype(BF16)
        csum = jnp.dot(tri, sel, preferred_element_type=F32)
        slot = off_sc[...] + cnt_sc[...] + csum - sel.astype(F32)
        rec = rec_ref[...]
        lanef = lane.astype(F32)
        pos1 = jnp.sum(jnp.where(lanef == rec[:, 0:1], slot, 0.0), axis=1, keepdims=True)
        pos2 = jnp.sum(jnp.where(lanef == rec[:, 1:2], slot, 0.0), axis=1, keepdims=True)
        pos_ref[...] = jnp.where(lane == 0, pos1, 0.0) + jnp.where(lane == 1, pos2, 0.0)
        cnt_sc[...] += csum[tm - 1:tm, :]


def _slots(sel, rec, tile, n_experts):
    m = sel.shape[0]
    tm = _pick(m, (512, 256, 128))
    row = lambda p, i: (i, 0)
    return pl.pallas_call(
        functools.partial(_slot_kernel, tile=tile, n_experts=n_experts), grid=(2, m // tm),
        in_specs=[pl.BlockSpec((tm, LANE), row), pl.BlockSpec((tm, LANE), row)],
        out_specs=[pl.BlockSpec((tm, LANE), lambda p, i: (i * p, 0)),
                   pl.BlockSpec((8, LANE), lambda p, i: (0, 0))],
        out_shape=[jax.ShapeDtypeStruct((m, LANE), F32), jax.ShapeDtypeStruct((8, LANE), F32)],
        scratch_shapes=[pltpu.VMEM((1, LANE), F32), pltpu.VMEM((1, LANE), F32)],
        compiler_params=_cparams(("arbitrary", "arbitrary")),
    )(sel, rec)


def _row_copy(src, src_row, dst, dst_row, sem):
    return pltpu.make_async_copy(src.at[pl.ds(src_row, 1)], dst.at[pl.ds(dst_row, 1)], sem)


def _dispatch_kernel(pos1_ref, pos2_ref, pad_row_ref, pad_len_ref, x_ref, xs_hbm, zero_sc, sem, *, tm):
    step = pl.program_id(0)
    base = step * tm

    @pl.when(step == 0)
    def _():
        zero_sc[...] = jnp.zeros(zero_sc.shape, zero_sc.dtype)
        for e in range(pad_row_ref.shape[0]):
            def fill(r, carry, e=e):
                _row_copy(zero_sc, 0, xs_hbm, pad_row_ref[e] + r, sem).start()
                return carry

            def fill_wait(r, carry):
                _row_copy(zero_sc, 0, xs_hbm, 0, sem).wait()
                return carry

            lax.fori_loop(0, pad_len_ref[e], fill, 0)
            lax.fori_loop(0, pad_len_ref[e], fill_wait, 0)

    def issue(r, carry):
        t = base + r
        _row_copy(x_ref, r, xs_hbm, pos1_ref[t], sem).start()
        _row_copy(x_ref, r, xs_hbm, pos2_ref[t], sem).start()
        return carry

    lax.fori_loop(0, tm, issue, 0)

    def drain(r, carry):
        _row_copy(x_ref, 0, xs_hbm, 0, sem).wait()
        _row_copy(x_ref, 0, xs_hbm, 0, sem).wait()
        return carry

    lax.fori_loop(0, tm, drain, 0)


def _dispatch(x, pos1, pos2, pad_row, pad_len, n_rows):
    m, d = x.shape
    tm = _pick(m, (512, 256, 128))
    kern = functools.partial(_dispatch_kernel, tm=tm)
    return pl.pallas_call(
        kern,
        grid_spec=pltpu.PrefetchScalarGridSpec(
            num_scalar_prefetch=4, grid=(m // tm,),
            in_specs=[pl.BlockSpec((tm, d), lambda i, *_: (i, 0))],
            out_specs=pl.BlockSpec(memory_space=pl.ANY),
            scratch_shapes=[pltpu.VMEM((8, d), x.dtype), pltpu.SemaphoreType.DMA]),
        out_shape=jax.ShapeDtypeStruct((n_rows, d), x.dtype),
        compiler_params=_cparams(("arbitrary",)),
    )(pos1, pos2, pad_row, pad_len, x)


def _grouped_ffn_kernel(tile_e_ref, n_used_ref, x_ref, wg_ref, wu_ref, wd_ref, o_ref, acc_sc):
    del tile_e_ref
    i = pl.program_id(0)
    f = pl.program_id(1)
    used = i < n_used_ref[0]

    @pl.when(f == 0)
    def _():
        acc_sc[...] = jnp.zeros(acc_sc.shape, F32)

    @pl.when(used)
    def _():
        xb = x_ref[...].astype(BF16)
        hg = jnp.dot(xb, wg_ref[...], preferred_element_type=F32)
        hu = jnp.dot(xb, wu_ref[...], preferred_element_type=F32)
        h = hg * jax.nn.sigmoid(hg) * hu
        acc_sc[...] += jnp.dot(h.astype(BF16), wd_ref[...], preferred_element_type=F32)

    @pl.when(f == pl.num_programs(1) - 1)
    def _():
        o_ref[...] = acc_sc[...]


def _grouped_ffn(xs, tile_e, n_used, wg, wu, wd, tile):
    p, d = xs.shape
    ff = wg.shape[2]
    tf = _pick(ff, (1024, 512, 256, 128))
    row = lambda i, f, te, nu: (i, 0)
    return pl.pallas_call(
        _grouped_ffn_kernel,
        grid_spec=pltpu.PrefetchScalarGridSpec(
            num_scalar_prefetch=2, grid=(p // tile, ff // tf),
            in_specs=[pl.BlockSpec((tile, d), lambda i, f, te, nu: (jnp.minimum(i, nu[0] - 1), 0)),
                      pl.BlockSpec((None, d, tf), lambda i, f, te, nu: (te[i], 0, f)),
                      pl.BlockSpec((None, d, tf), lambda i, f, te, nu: (te[i], 0, f)),
                      pl.BlockSpec((None, tf, d), lambda i, f, te, nu: (te[i], f, 0))],
            out_specs=pl.BlockSpec((tile, d), row),
            scratch_shapes=[pltpu.VMEM((tile, d), F32)]),
        out_shape=jax.ShapeDtypeStruct((p, d), F32),
        compiler_params=_cparams(("parallel", "arbitrary")),
    )(tile_e, n_used, xs, wg, wu, wd)


def _combine_kernel(pos1_ref, pos2_ref, x_ref, rec_ref, g_ref, b_ref, ys_hbm, o_ref, ob_ref,
                    y1_sc, y2_sc, sem, *, alpha, tm):
    step = pl.program_id(0)
    slot = step & 1

    def gather(tile, into):
        def issue(r, carry):
            t = tile * tm + r
            _row_copy(ys_hbm, pos1_ref[t], y1_sc.at[into], r, sem.at[into]).start()
            _row_copy(ys_hbm, pos2_ref[t], y2_sc.at[into], r, sem.at[into]).start()
            return carry

        lax.fori_loop(0, tm, issue, 0)

    @pl.when(step == 0)
    def _():
        gather(0, 0)

    @pl.when(step + 1 < pl.num_programs(0))
    def _():
        gather(step + 1, 1 - slot)

    def drain(r, carry):
        _row_copy(ys_hbm, 0, y1_sc.at[slot], 0, sem.at[slot]).wait()
        _row_copy(ys_hbm, 0, y2_sc.at[slot], 0, sem.at[slot]).wait()
        return carry

    lax.fori_loop(0, tm, drain, 0)
    rec = rec_ref[...]
    f = rec[:, 2:3] * y1_sc[slot] + rec[:, 3:4] * y2_sc[slot]
    z = _ln_rows(alpha * x_ref[...] + f, g_ref[...], b_ref[...])
    o_ref[...] = z
    ob_ref[...] = z.astype(ob_ref.dtype)


def _combine(x, rec, ys, pos1, pos2, g, b, alpha):
    m, d = x.shape
    tm = _pick(m, (256, 128))
    row = lambda i, p1, p2: (i, 0)
    fixed = lambda i, p1, p2: (0, 0)
    return pl.pallas_call(
        functools.partial(_combine_kernel, alpha=alpha, tm=tm),
        grid_spec=pltpu.PrefetchScalarGridSpec(
            num_scalar_prefetch=2, grid=(m // tm,),
            in_specs=[pl.BlockSpec((tm, d), row), pl.BlockSpec((tm, LANE), row),
                      pl.BlockSpec((1, d), fixed), pl.BlockSpec((1, d), fixed),
                      pl.BlockSpec(memory_space=pl.ANY)],
            out_specs=[pl.BlockSpec((tm, d), row), pl.BlockSpec((tm, d), row)],
            scratch_shapes=[pltpu.VMEM((2, tm, d), F32), pltpu.VMEM((2, tm, d), F32),
                            pltpu.SemaphoreType.DMA((2,))]),
        out_shape=[jax.ShapeDtypeStruct((m, d), F32), jax.ShapeDtypeStruct((m, d), BF16)],
        compiler_params=_cparams(("arbitrary",)),
    )(pos1, pos2, x, rec, g, b, ys)


MOE_TILE = 512


def _routed_moe(x, w_router, wg, wu, wd, g, b, alpha):
    m, d = x.shape
    ne = w_router.shape[1]
    rec, sel = _router(x, w_router)
    pos, meta = _slots(sel, rec, MOE_TILE, ne)
    pos1 = pos[:, 0].astype(jnp.int32)
    pos2 = pos[:, 1].astype(jnp.int32)
    n_tiles = (2 * m) // MOE_TILE + ne
    counts, starts = meta[0, :ne], meta[1, :ne]
    ends = starts + jnp.ceil(counts / MOE_TILE) * MOE_TILE
    tile_start = jnp.arange(n_tiles, dtype=F32) * MOE_TILE
    tile_e = jnp.minimum(jnp.sum(ends[None, :] <= tile_start[:, None], axis=1), ne - 1).astype(jnp.int32)
    n_used = (ends[ne - 1:ne] / MOE_TILE).astype(jnp.int32)
    pad_row = (starts + counts).astype(jnp.int32)
    pad_len = (ends - starts - counts).astype(jnp.int32)
    xs = _dispatch(x, pos1, pos2, pad_row, pad_len, n_tiles * MOE_TILE)
    ys = _grouped_ffn(xs, tile_e, n_used, wg, wu, wd, MOE_TILE)
    return _combine(x, rec, ys, pos1, pos2, g, b, alpha)


def _epi_scale(scale):
    return lambda accs, tabs, vecs: [accs[0] * scale]


def _epi_f32_and_bf16(accs, tabs, vecs):
    return [accs[0], accs[0]]


def _epi_rms(accs, tabs, vecs):
    r = _rms_rows(accs[0], vecs[0])
    return [r, r]


def _epi_rope(accs, tabs, vecs):
    return [accs[0] * tabs[0] + accs[1] * tabs[1]]


def _epi_gelu(accs, tabs, vecs):
    return [jax.nn.gelu(accs[0])]


def _epi_gelu_ln(accs, tabs, vecs):
    return [_ln_rows(jax.nn.gelu(accs[0]), vecs[0], vecs[1])]


def _epi_plain(accs, tabs, vecs):
    return [accs[0]]


def _epi_q_rope(scale):
    return lambda accs, tabs, vecs: [(accs[0] * tabs[0] + accs[1] * tabs[1]) * scale]


def _rope_tables(pos):
    half = B_ROPE // 2
    inv = ROPE_THETA ** (-jnp.arange(half, dtype=F32) / half)
    ang = pos.astype(F32)[:, None] * inv[None, :]
    cos, sin = jnp.cos(ang), jnp.sin(ang)
    cos2 = jnp.concatenate([cos, cos], axis=1)
    sin2 = jnp.concatenate([-sin, sin], axis=1)
    return cos2, sin2


def _swap_halves(w):
    half = w.shape[-1] // 2
    return jnp.concatenate([w[..., half:], w[..., :half]], axis=-1)


def kernel(x_prompt, x_sample, cache_a_k, cache_a_v, cache_b_latent, cache_b_krope, state_d_pool, w_in, a_lambda, a_subln, b_q_norm, b_w_uq, b_kv_norm, b_w_ukv, c_ln_g, c_ln_b, c_w_s, c_b_s, d_w_grp, d_scale, w_branch, w_out, ln1_g, ln1_b, ln2_g, ln2_b, f_w_gate, f_w_up, f_w_down, m_router, m_w_gate, m_w_up, m_w_down):
    depth = w_in.shape[0]
    nbp, lp, d_model = x_prompt.shape
    nbs, ls, _ = x_sample.shape
    past = cache_a_k.shape[2]
    lks = past + ls
    tp, ts = nbp * lp, nbs * ls
    aw = A_HEADS * 2 * A_DH
    q_rank = b_q_norm.shape[1]
    kv_rank = b_kv_norm.shape[1]
    c_width = c_ln_g.shape[1]
    d_width = d_scale.shape[1]
    alpha = (2.0 * depth) ** 0.25
    sizes = (aw, aw, aw, q_rank, kv_rank, B_ROPE, c_width, c_width, d_width) + (d_model,) * N_BRANCH
    offs = [0]
    for s in sizes:
        offs.append(offs[-1] + s)

    x = jnp.concatenate([x_prompt.reshape(tp, d_model), x_sample.reshape(ts, d_model)], axis=0)
    xb = x.astype(BF16)

    pos = jnp.concatenate([jnp.tile(jnp.arange(lp, dtype=jnp.int32), nbp),
                           jnp.tile(past + jnp.arange(ls, dtype=jnp.int32), nbs)])
    cos2, sin2 = _rope_tables(pos)
    rows = pos.shape[0]
    ones = jnp.ones((rows, B_NOPE), F32)
    zeros_n = jnp.zeros((rows, B_NOPE), F32)
    zeros_r = jnp.zeros((rows, 2 * LANE - B_NOPE - B_ROPE), F32)
    q_cos = jnp.concatenate([ones, cos2, zeros_r], axis=1)
    q_sin = jnp.concatenate([zeros_n, sin2, zeros_r], axis=1)

    slopes = jnp.exp2(-8.0 * jnp.arange(1, A_HEADS + 1, dtype=F32) / A_HEADS) * LOG2E
    slopes = jnp.broadcast_to(slopes[:, None, None], (A_HEADS, 1, LANE))

    states_p, states_s = [], []
    kv_state = lat_state = None
    for l in range(depth):
        wl = w_in[l]
        seg = lambda k: wl[:, offs[k]:offs[k + 1]].astype(BF16)
        lam_init = 0.8 - 0.6 * math.exp(-0.3 * l)

        (q_a,) = _fused_mm(_epi_scale(A_DH ** -0.5 * LOG2E), xb, [seg(0)], [BF16], tn=1024)
        w_kv = jnp.concatenate([seg(1), seg(2)], axis=1)
        kv_a_b, kv_state = _state_proj(xb, w_kv, None, tn=aw, layer=l, depth=depth, n_prompt_rows=tp,
                                       prev=kv_state)
        subln = a_subln[l].reshape(1, 2 * A_DH)
        o_a_p = _diff_attn(q_a, kv_a_b, slopes, a_lambda[l], subln, nb=nbp, lq=lp, lk=lp, q_off=0,
                           q_row0=0, lam_init=lam_init)
        cache_kv = jnp.concatenate([cache_a_k[l].reshape(nbs, past, aw),
                                    cache_a_v[l].reshape(nbs, past, aw)], axis=2).astype(BF16)
        kv_s = jnp.concatenate([cache_kv, kv_a_b[tp:].reshape(nbs, ls, 2 * aw)], axis=1)
        o_a_s = _diff_attn(q_a, kv_s.reshape(nbs * lks, 2 * aw), slopes, a_lambda[l], subln, nb=nbs,
                           lq=ls, lk=lks, q_off=past, q_row0=tp, lam_init=lam_init)
        o_a = jnp.concatenate([o_a_p, o_a_s], axis=0)

        (cq,) = _fused_mm(_epi_rms, xb, [seg(3)], [BF16], tn=q_rank,
                          vecs=[b_q_norm[l].reshape(1, q_rank)])
        lat_b, lat_state = _state_proj(xb, seg(4), b_kv_norm[l].reshape(1, kv_rank), tn=kv_rank, layer=l,
                                       depth=depth, n_prompt_rows=tp, prev=lat_state)
        w_kr = wl[:, offs[5]:offs[6]]
        (kr,) = _fused_mm(_epi_rope, xb, [w_kr.astype(BF16), _swap_halves(w_kr).astype(BF16)], [F32],
                          tn=B_ROPE, tabs=[cos2, sin2])
        w_uq = b_w_uq[l].reshape(q_rank, B_HEADS, B_NOPE + B_ROPE)
        w_uq_n, w_uq_r = w_uq[..., :B_NOPE], w_uq[..., B_NOPE:]
        zpad = jnp.zeros((q_rank, B_HEADS, 2 * LANE - B_NOPE - B_ROPE), F32)
        w_q1 = jnp.concatenate([w_uq_n, w_uq_r, zpad], axis=-1).reshape(q_rank, -1).astype(BF16)
        w_q2 = jnp.concatenate([jnp.zeros_like(w_uq_n), _swap_halves(w_uq_r), zpad],
                               axis=-1).reshape(q_rank, -1).astype(BF16)
        (q_b,) = _fused_mm(_epi_q_rope((B_NOPE + B_ROPE) ** -0.5 * LOG2E), cq, [w_q1, w_q2], [BF16],
                           tn=2 * LANE, tabs=[q_cos, q_sin])
        w_ukv = b_w_ukv[l].reshape(kv_rank, B_HEADS, B_NOPE + B_VDIM)
        w_up = jnp.concatenate([w_ukv[..., :B_NOPE].reshape(kv_rank, -1),
                                w_ukv[..., B_NOPE:].reshape(kv_rank, -1)], axis=1).astype(BF16)
        lat_s = jnp.concatenate([cache_b_latent[l].astype(BF16), lat_b[tp:].reshape(nbs, ls, kv_rank)],
                                axis=1).reshape(nbs * lks, kv_rank)
        lat_cat = jnp.concatenate([lat_b[:tp], lat_s], axis=0)
        (kv_b,) = _fused_mm(_epi_plain, lat_cat, [w_up], [BF16], tn=1024)
        kr_s = jnp.concatenate([cache_b_krope[l], kr[tp:].reshape(nbs, ls, B_ROPE)],
                               axis=1).reshape(nbs * lks, B_ROPE)
        kr_cat = jnp.pad(jnp.concatenate([kr[:tp], kr_s], axis=0),
                         ((0, 0), (0, LANE - B_ROPE))).astype(BF16)
        o_b_p = _mla_attn(q_b, kv_b, kr_cat, nb=nbp, lq=lp, lk=lp, q_off=0, q_row0=0, kv_row0=0)
        if tp % lks:
            kv_b_s, kr_b_s, s_row0 = kv_b[tp:], kr_cat[tp:], 0
        else:
            kv_b_s, kr_b_s, s_row0 = kv_b, kr_cat, tp
        o_b_s = _mla_attn(q_b, kv_b_s, kr_b_s, nb=nbs, lq=ls, lk=lks, q_off=past, q_row0=tp,
                          kv_row0=s_row0)
        o_b = jnp.concatenate([o_b_p, o_b_s], axis=0)

        (u_c,) = _fused_mm(_epi_gelu, xb, [seg(6)], [BF16], tn=1024)
        (v_c,) = _fused_mm(_epi_gelu_ln, xb, [seg(7)], [F32], tn=c_width,
                           vecs=[c_ln_g[l].reshape(1, c_width), c_ln_b[l].reshape(1, c_width)])
        w_s, b_s = c_w_s[l], c_b_s[l]
        reps = C_BLOCK // ls if ls < C_BLOCK else 1
        w_tl = w_s[:, :C_BLOCK // reps, :C_BLOCK // reps]
        w_smp = jnp.zeros_like(w_s)
        for r in range(reps):
            sl = slice(r * (C_BLOCK // reps), (r + 1) * (C_BLOCK // reps))
            w_smp = w_smp.at[:, sl, sl].set(w_tl)
        b_smp = jnp.tile(b_s[:, :C_BLOCK // reps], (1, reps))
        w_sel = jnp.stack([w_s, w_smp]).astype(BF16)
        b_sel = jnp.stack([b_s, b_smp])[..., None]
        o_c = _spatial_gate(u_c, v_c, w_sel, b_sel, tp)

        (d_in,) = _fused_mm(_epi_plain, xb, [seg(8)], [F32], tn=1024)
        w_grp = d_w_grp[l].astype(BF16)
        scale = d_scale[l].reshape(1, d_width)
        hist_p = jnp.zeros((nbp, D_HALO, d_width), F32)
        hist_s = jnp.pad(state_d_pool[l], ((0, 0), (D_HALO - D_HIST, 0), (0, 0)))
        o_d_p = _pool(d_in, hist_p, w_grp, scale, nb=nbp, lq=lp, row0=0, past=0)
        o_d_s = _pool(d_in, hist_s, w_grp, scale, nb=nbs, lq=ls, row0=tp, past=past)
        o_d = jnp.concatenate([o_d_p, o_d_s], axis=0)

        w_gate = wl[:, offs[9]:offs[13]].astype(BF16)
        y = _merge(xb, w_gate, [o_a, o_b, o_c, o_d], w_branch[l].astype(BF16))
        x, xb = _out_ln(y, x, w_out[l].astype(BF16), ln1_g[l].reshape(1, -1), ln1_b[l].reshape(1, -1),
                        alpha)

        i = l // 2
        g2, b2 = ln2_g[l].reshape(1, -1), ln2_b[l].reshape(1, -1)
        if l % 2 == 0:
            x, xb = _ffn(xb, x, f_w_gate[i].astype(BF16), f_w_up[i].astype(BF16),
                         f_w_down[i].astype(BF16), g2, b2, alpha)
        else:
            x, xb = _routed_moe(x, m_router[i], m_w_gate[i].astype(BF16), m_w_up[i].astype(BF16),
                                m_w_down[i].astype(BF16), g2, b2, alpha)

        states_p.append((kr[:tp].reshape(nbp, lp, B_ROPE),
                         d_in[:tp].reshape(nbp, lp, d_width)[:, lp - D_HIST:]))
        states_s.append((kr[tp:].reshape(nbs, ls, B_ROPE), v_c[tp:].reshape(nbs, ls, c_width),
                         jnp.concatenate([state_d_pool[l], d_in[tp:].reshape(nbs, ls, d_width)],
                                         axis=1)[:, -D_HIST:]))

    yp = x[:tp].reshape(nbp, lp, d_model)
    ys = x[tp:].reshape(nbs, ls, d_model)
    stack = lambda sts, k: jnp.stack([s[k] for s in sts])
    k_p, k_s, v_p, v_s = kv_state
    lat_p, lat_s = lat_state
    heads_p = (depth, nbp, lp, A_HEADS, 2 * A_DH)
    heads_s = (depth, nbs, ls, A_HEADS, 2 * A_DH)
    return (yp, ys, k_p.reshape(heads_p), v_p.reshape(heads_p), lat_p.reshape(depth, nbp, lp, kv_rank),
            stack(states_p, 0), stack(states_p, 1), k_s.reshape(heads_s), v_s.reshape(heads_s),
            lat_s.reshape(depth, nbs, ls, kv_rank), stack(states_s, 0), stack(states_s, 1),
            stack(states_s, 2))
```

```python
import functools
import math

import jax
import jax.numpy as jnp
from jax import lax
from jax.experimental import pallas as pl
from jax.experimental.pallas import tpu as pltpu

CHUNK = 64
A_HEADS = 8
A_DH = 64
B_HEADS = 8
B_NOPE = 128
B_ROPE = 64
B_VDIM = 128
C_GROUPS = 4
C_BLOCK = 128
D_WINDOWS = (2, 4, 8, 16)
D_HIST = 15
D_HALO = 16
N_BRANCH = 4
ROPE_THETA = 10000.0
LN_EPS = 1e-5
RMS_EPS = 1e-6
LOG2E = 1.4426950408889634
NEG = -1e30
LANE = 128
VMEM_LIMIT = 56 * 1024 * 1024

F32 = jnp.float32
BF16 = jnp.bfloat16


def _pick(n, cands):
    for c in cands:
        if n % c == 0:
            return c
    raise ValueError(f"no tile for {n} in {cands}")


def _cparams(sem):
    return pltpu.CompilerParams(dimension_semantics=sem, vmem_limit_bytes=VMEM_LIMIT)


def _ln_rows(z, g, b):
    mu = jnp.mean(z, axis=-1, keepdims=True)
    zc = z - mu
    var = jnp.mean(zc * zc, axis=-1, keepdims=True)
    return zc * lax.rsqrt(var + LN_EPS) * g + b


def _rms_rows(z, g):
    return z * lax.rsqrt(jnp.mean(z * z, axis=-1, keepdims=True) + RMS_EPS) * g


def _fused_mm_kernel(*refs, fn, nw, ntab, nvec, precision):
    x = refs[0][...]
    w_refs = refs[1:1 + nw]
    tab_refs = refs[1 + nw:1 + nw + ntab]
    vec_refs = refs[1 + nw + ntab:1 + nw + ntab + nvec]
    out_refs = refs[1 + nw + ntab + nvec:]
    accs = [jnp.dot(x, w[...], preferred_element_type=F32, precision=precision) for w in w_refs]
    res = fn(accs, [t[...] for t in tab_refs], [v[...] for v in vec_refs])
    for o, r in zip(out_refs, res):
        o[...] = r.astype(o.dtype)


def _fused_mm(fn, x, ws, out_dtypes, *, tn, tm=None, tabs=(), tabs_follow_cols=False, vecs=(),
              precision=None):
    m, kdim = x.shape
    n = ws[0].shape[1]
    tm = tm or _pick(m, (1024, 512, 256, 128))
    grid = (m // tm, n // tn)
    in_specs = [pl.BlockSpec((tm, kdim), lambda i, j: (i, 0))]
    in_specs += [pl.BlockSpec((kdim, tn), lambda i, j: (0, j)) for _ in ws]
    for t in tabs:
        if tabs_follow_cols:
            in_specs.append(pl.BlockSpec((tm, tn), lambda i, j: (i, j)))
        else:
            in_specs.append(pl.BlockSpec((tm, t.shape[1]), lambda i, j: (i, 0)))
    in_specs += [pl.BlockSpec((1, tn), lambda i, j: (0, j)) for _ in vecs]
    out_specs = [pl.BlockSpec((tm, tn), lambda i, j: (i, j)) for _ in out_dtypes]
    out_shape = [jax.ShapeDtypeStruct((m, n), d) for d in out_dtypes]
    kern = functools.partial(_fused_mm_kernel, fn=fn, nw=len(ws), ntab=len(tabs), nvec=len(vecs),
                             precision=precision)
    return pl.pallas_call(
        kern, grid=grid, in_specs=in_specs, out_specs=out_specs, out_shape=out_shape,
        compiler_params=_cparams(("parallel", "arbitrary")),
    )(x, *ws, *tabs, *vecs)


def _state_proj_kernel(*refs, n_col, n_prompt_tiles, has_gain, has_prev):
    x_ref, w_ref = refs[0], refs[1]
    pos = 2
    g_ref = refs[pos] if has_gain else None
    pos += (1 if has_gain else 0) + (2 * n_col if has_prev else 0)
    joint_ref = refs[pos]
    state_refs = refs[pos + 1:]
    i = pl.program_id(0)
    j = pl.program_id(1)
    acc = jnp.dot(x_ref[...], w_ref[...], preferred_element_type=F32)
    if has_gain:
        acc = _rms_rows(acc, g_ref[...])
    joint_ref[...] = acc.astype(joint_ref.dtype)
    for c in range(n_col):
        @pl.when((j == c) & (i < n_prompt_tiles))
        def _(c=c):
            state_refs[2 * c][...] = acc

        @pl.when((j == c) & (i >= n_prompt_tiles))
        def _(c=c):
            state_refs[2 * c + 1][...] = acc


def _state_proj(x, w, gain, *, tn, layer, depth, n_prompt_rows, prev):
    m, kdim = x.shape
    n = w.shape[1]
    n_col = n // tn
    tm = math.gcd(_pick(m, (512, 256, 128)), n_prompt_rows)
    n_p = n_prompt_rows // tm
    n_s = (m - n_prompt_rows) // tm
    in_specs = [pl.BlockSpec((tm, kdim), lambda i, j: (i, 0)), pl.BlockSpec((kdim, tn), lambda i, j: (0, j))]
    args = [x, w]
    if gain is not None:
        in_specs.append(pl.BlockSpec((1, tn), lambda i, j: (0, j)))
        args.append(gain)
    aliases = {}
    if prev is not None:
        for k, buf in enumerate(prev):
            aliases[len(args)] = 1 + k
            in_specs.append(pl.BlockSpec(memory_space=pl.ANY))
            args.append(buf)
    out_specs = [pl.BlockSpec((tm, tn), lambda i, j: (i, j))]
    out_shape = [jax.ShapeDtypeStruct((m, n), BF16)]
    for _ in range(n_col):
        out_specs.append(pl.BlockSpec((tm, tn), lambda i, j: (layer * n_p + jnp.minimum(i, n_p - 1), 0)))
        out_shape.append(jax.ShapeDtypeStruct((depth * n_p * tm, tn), F32))
        out_specs.append(pl.BlockSpec((tm, tn), lambda i, j: (layer * n_s + jnp.maximum(i - n_p, 0), 0)))
        out_shape.append(jax.ShapeDtypeStruct((depth * n_s * tm, tn), F32))
    kern = functools.partial(_state_proj_kernel, n_col=n_col, n_prompt_tiles=n_p,
                             has_gain=gain is not None, has_prev=prev is not None)
    outs = pl.pallas_call(
        kern, grid=(m // tm, n_col), in_specs=in_specs, out_specs=out_specs, out_shape=out_shape,
        input_output_aliases=aliases, compiler_params=_cparams(("arbitrary", "arbitrary")),
    )(*args)
    return outs[0], list(outs[1:])


ATTN_ROW_GROUPS = 2
DIFF_ATTN_TQ = 512
MLA_ATTN_TQ = 1024


def _row_groups(rows):
    g = ATTN_ROW_GROUPS
    while rows % (16 * g):
        g //= 2
    return g


def _nt_dot(a, b):
    return lax.dot_general(a, b, (((1,), (1,)), ((), ())), preferred_element_type=F32)


def _softmax_pv(scores, v, biases, m_sc, acc_sc):
    width = v.shape[0]
    v_ext = jnp.concatenate([v, jnp.ones(v.shape, v.dtype)], axis=1)
    r0 = 0
    for s, bias in zip(scores, biases):
        rows = slice(r0, r0 + s.shape[0])
        r0 += s.shape[0]
        if bias is not None:
            s = s + bias
        m_prev = m_sc[rows, :]
        m_new = jnp.maximum(m_prev, jnp.max(s, axis=1, keepdims=True))
        alpha = jnp.exp2(m_prev - m_new)
        m_rep = jnp.tile(m_new, (1, width // LANE)) if width >= LANE else m_new[:, :width]
        p = jnp.exp2(s - m_rep)
        pv = jnp.dot(p.astype(BF16), v_ext, preferred_element_type=F32)
        acc_sc[rows, :] = jnp.tile(alpha, (1, 2)) * acc_sc[rows, :] + pv
        m_sc[rows, :] = m_new


def _sweep_keys(q_groups, keys, v_ref, off_biases, diag_biases, m_sc, acc_sc, s_sc, *, q0, tq, tk):
    m_sc[...] = jnp.full(m_sc.shape, NEG, F32)
    acc_sc[...] = jnp.zeros(acc_sc.shape, F32)
    n_off = q0 // tk
    kd = pl.multiple_of(q0, tq)

    def scores(start, size):
        k = keys(start, size)
        return [_nt_dot(q, k) for q in q_groups()]

    if s_sc is None:
        def body(j, carry):
            ks = pl.multiple_of(j * tk, tk)
            _softmax_pv(scores(ks, tk), v_ref[pl.ds(ks, tk), :], off_biases(ks), m_sc, acc_sc)
            return carry

        lax.fori_loop(0, n_off, body, 0)
        _softmax_pv(scores(kd, tq), v_ref[pl.ds(kd, tq), :], diag_biases, m_sc, acc_sc)
        return

    assert tq == tk
    bounds = []
    r0 = 0
    for q in q_groups():
        bounds.append(slice(r0, r0 + q.shape[0]))
        r0 += q.shape[0]

    def stash(new_scores):
        for rows, s in zip(bounds, new_scores):
            s_sc[rows, :] = s

    def staged():
        return [s_sc[rows, :] for rows in bounds]

    stash(scores(0, tk))

    def body(j, carry):
        ks = pl.multiple_of(j * tk, tk)
        nxt = scores(pl.multiple_of(ks + tk, tk), tk)
        _softmax_pv(staged(), v_ref[pl.ds(ks, tk), :], off_biases(ks), m_sc, acc_sc)
        stash(nxt)
        return carry

    lax.fori_loop(0, n_off, body, 0)
    _softmax_pv(staged(), v_ref[pl.ds(kd, tq), :], diag_biases, m_sc, acc_sc)


def _chunk_mask_bias(q0, tq, slope):
    qpos = q0 + lax.broadcasted_iota(jnp.int32, (tq, tq), 0)
    kpos = q0 + lax.broadcasted_iota(jnp.int32, (tq, tq), 1)
    visible = (kpos >> 6) <= (qpos >> 6)
    if slope is None:
        return jnp.where(visible, 0.0, NEG)
    rel = (qpos - q0 - jnp.abs(qpos - kpos)).astype(F32)
    return jnp.where(visible, slope * rel, NEG)


def _diff_attn_head(qi, q_ref, k_ref, v_ref, slope, lam_ref, g_ref, o_ref,
                    qs_sc, m_sc, acc_sc, s_sc, *, tq, tk, q_off, lam_init):
    q0 = q_off + qi * tq
    q = q_ref[...]
    lane = lax.broadcasted_iota(jnp.int32, q.shape, 1)
    zero = jnp.zeros_like(q)
    qs_sc[0:tq, :] = jnp.where(lane < A_DH, q, zero)
    qs_sc[tq:2 * tq, :] = jnp.where(lane >= A_DH, q, zero)
    groups = _row_groups(2 * tq)
    gr = 2 * tq // groups

    def q_groups():
        return [qs_sc[g * gr:(g + 1) * gr, :] for g in range(groups)]

    def off_biases(ks):
        kpos = ks + lax.broadcasted_iota(jnp.int32, (1, tk), 1)
        return [slope * (kpos - q0).astype(F32)] * groups

    bias = _chunk_mask_bias(q0, tq, slope)
    if gr > tq:
        diag_biases = [jnp.concatenate([bias] * (gr // tq), axis=0)] * groups
    else:
        diag_biases = [bias[(g * gr) % tq:(g * gr) % tq + gr, :] for g in range(groups)]
    _sweep_keys(q_groups, lambda ks, size: k_ref[pl.ds(ks, size), :], v_ref, off_biases, diag_biases,
                m_sc, acc_sc, s_sc, q0=q0, tq=tq, tk=tk)

    lp = lam_ref[...]
    lam = (jnp.exp(jnp.sum(lp[0:1, :] * lp[1:2, :], axis=1, keepdims=True))
           - jnp.exp(jnp.sum(lp[2:3, :] * lp[3:4, :], axis=1, keepdims=True)) + lam_init)
    acc = acc_sc[...]
    o1 = acc[0:tq, 0:LANE] / acc[0:tq, LANE:2 * LANE]
    o2 = acc[tq:2 * tq, 0:LANE] / acc[tq:2 * tq, LANE:2 * LANE]
    o = _rms_rows(o1 - lam * o2, g_ref[...]) * (1.0 - lam_init)
    o_ref[...] = o.astype(o_ref.dtype)


def _into_existing(kernel, in_specs, args, prev):
    if prev is None:
        return kernel, in_specs, args, {}

    def with_prev(prev_ref, *refs):
        del prev_ref
        kernel(*refs)

    return with_prev, [pl.BlockSpec(memory_space=pl.ANY)] + in_specs, [prev] + args, {0: 0}


def _head_cols(ref, h, width=LANE):
    return ref.at[:, h * width:(h + 1) * width]


def _diff_attn_kernel(q_ref, k_ref, v_ref, slope_ref, lam_ref, g_ref, o_ref,
                      qs_sc, m_sc, acc_sc, s_sc, **static):
    _diff_attn_head(pl.program_id(2), q_ref, k_ref, v_ref, slope_ref[...][:, 0:1], lam_ref, g_ref,
                    o_ref, qs_sc, m_sc, acc_sc, s_sc, **static)


def _diff_attn_heads_kernel(q_ref, kv_ref, slope_ref, lam_ref, g_ref, o_ref,
                            qs_sc, m_sc, acc_sc, **static):
    for h in range(A_HEADS):
        _diff_attn_head(pl.program_id(1), _head_cols(q_ref, h), _head_cols(kv_ref, h),
                        _head_cols(kv_ref, A_HEADS + h), slope_ref[h][:, 0:1], lam_ref, g_ref,
                        _head_cols(o_ref, h), qs_sc, m_sc, acc_sc, None, **static)


def _mla_attn_head(qi, q_ref, kn_ref, kr_ref, v_ref, o_ref, m_sc, acc_sc, s_sc, *, tq, tk, q_off):
    q0 = q_off + qi * tq
    groups = _row_groups(tq)
    gr = tq // groups

    def q_groups():
        return [q_ref[g * gr:(g + 1) * gr, :] for g in range(groups)]

    def keys(ks, size):
        return jnp.concatenate([kn_ref[pl.ds(ks, size), :], kr_ref[pl.ds(ks, size), :]], axis=1)

    mask = _chunk_mask_bias(q0, tq, None)
    diag_biases = [mask[g * gr:(g + 1) * gr, :] for g in range(groups)]
    _sweep_keys(q_groups, keys, v_ref, lambda ks: [None] * groups, diag_biases, m_sc, acc_sc,
                s_sc, q0=q0, tq=tq, tk=tk)
    acc = acc_sc[...]
    o_ref[...] = (acc[:, 0:LANE] / acc[:, LANE:2 * LANE]).astype(o_ref.dtype)


def _mla_attn_kernel(q_ref, kn_ref, kr_ref, v_ref, o_ref, m_sc, acc_sc, s_sc, **static):
    _mla_attn_head(pl.program_id(2), q_ref, kn_ref, kr_ref, v_ref, o_ref, m_sc, acc_sc, s_sc, **static)


def _mla_attn_heads_kernel(q_ref, kv_ref, kr_ref, o_ref, m_sc, acc_sc, **static):
    for h in range(B_HEADS):
        _mla_attn_head(pl.program_id(1), _head_cols(q_ref, h, 2 * LANE), _head_cols(kv_ref, h), kr_ref,
                       _head_cols(kv_ref, B_HEADS + h), _head_cols(o_ref, h), m_sc, acc_sc, None,
                       **static)


def _attn_tiles(lq, q_off, tq_max):
    sizes = tuple(t for t in (1024, 512, 256, 128, 64) if t <= tq_max)
    tq = _pick(lq, sizes)
    tk = tq if q_off == 0 else _pick(q_off, (512, 256, 128, 64))
    return tq, tk


def _diff_attn(q, kv, slopes, lam_p, subln, *, nb, lq, lk, q_off, q_row0, lam_init, out_rows, prev=None):
    tq, tk = _attn_tiles(lq, q_off, DIFF_ATTN_TQ)
    nq = lq // tq
    qb0 = q_row0 // tq
    static = dict(tq=tq, tk=tk, q_off=q_off, lam_init=lam_init)
    width = A_HEADS * LANE
    out_shape = jax.ShapeDtypeStruct((out_rows, width), BF16)
    scratch = [pltpu.VMEM((2 * tq, LANE), BF16), pltpu.VMEM((2 * tq, LANE), F32),
               pltpu.VMEM((2 * tq, 2 * LANE), F32)]
    if tq != tk:
        kern, in_specs, args, aliases = _into_existing(
            functools.partial(_diff_attn_heads_kernel, **static),
            [pl.BlockSpec((tq, width), lambda b, i: (qb0 + b * nq + i, 0)),
             pl.BlockSpec((lk, 2 * width), lambda b, i: (b, 0)),
             pl.BlockSpec((A_HEADS, 1, LANE), lambda b, i: (0, 0, 0)),
             pl.BlockSpec((4, A_DH), lambda b, i: (0, 0)),
             pl.BlockSpec((1, LANE), lambda b, i: (0, 0))],
            [q, kv, slopes, lam_p, subln], prev)
        return pl.pallas_call(
            kern, grid=(nb, nq), in_specs=in_specs,
            out_specs=pl.BlockSpec((tq, width), lambda b, i: (qb0 + b * nq + i, 0)),
            out_shape=out_shape, scratch_shapes=scratch, input_output_aliases=aliases,
            compiler_params=_cparams(("parallel", "arbitrary")),
        )(*args)
    kern, in_specs, args, aliases = _into_existing(
        functools.partial(_diff_attn_kernel, **static),
        [pl.BlockSpec((tq, LANE), lambda b, h, i: (qb0 + b * nq + i, h)),
         pl.BlockSpec((lk, LANE), lambda b, h, i: (b, h)),
         pl.BlockSpec((lk, LANE), lambda b, h, i: (b, A_HEADS + h)),
         pl.BlockSpec((None, 1, LANE), lambda b, h, i: (h, 0, 0)),
         pl.BlockSpec((4, A_DH), lambda b, h, i: (0, 0)),
         pl.BlockSpec((1, LANE), lambda b, h, i: (0, 0))],
        [q, kv, kv, slopes, lam_p, subln], prev)
    return pl.pallas_call(
        kern, grid=(nb, A_HEADS, nq), in_specs=in_specs,
        out_specs=pl.BlockSpec((tq, LANE), lambda b, h, i: (qb0 + b * nq + i, h)),
        out_shape=out_shape, scratch_shapes=scratch + [pltpu.VMEM((2 * tq, tk), F32)],
        input_output_aliases=aliases,
        compiler_params=_cparams(("parallel", "parallel", "arbitrary")),
    )(*args)


def _mla_attn(q, kv, kr, *, nb, lq, lk, q_off, q_row0, kv_row0, out_rows, prev=None):
    tq, tk = _attn_tiles(lq, q_off, MLA_ATTN_TQ)
    nq = lq // tq
    qb0 = q_row0 // tq
    kb0 = kv_row0 // lk
    static = dict(tq=tq, tk=tk, q_off=q_off)
    width = B_HEADS * LANE
    out_shape = jax.ShapeDtypeStruct((out_rows, width), BF16)
    scratch = [pltpu.VMEM((tq, LANE), F32), pltpu.VMEM((tq, 2 * LANE), F32)]
    if tq != tk:
        kern, in_specs, args, aliases = _into_existing(
            functools.partial(_mla_attn_heads_kernel, **static),
            [pl.BlockSpec((tq, 2 * width), lambda b, i: (qb0 + b * nq + i, 0)),
             pl.BlockSpec((lk, 2 * width), lambda b, i: (kb0 + b, 0)),
             pl.BlockSpec((lk, LANE), lambda b, i: (kb0 + b, 0))],
            [q, kv, kr], prev)
        return pl.pallas_call(
            kern, grid=(nb, nq), in_specs=in_specs,
            out_specs=pl.BlockSpec((tq, width), lambda b, i: (qb0 + b * nq + i, 0)),
            out_shape=out_shape, scratch_shapes=scratch, input_output_aliases=aliases,
            compiler_params=_cparams(("parallel", "arbitrary")),
        )(*args)
    kern, in_specs, args, aliases = _into_existing(
        functools.partial(_mla_attn_kernel, **static),
        [pl.BlockSpec((tq, 2 * LANE), lambda b, h, i: (qb0 + b * nq + i, h)),
         pl.BlockSpec((lk, LANE), lambda b, h, i: (kb0 + b, h)),
         pl.BlockSpec((lk, LANE), lambda b, h, i: (kb0 + b, 0)),
         pl.BlockSpec((lk, LANE), lambda b, h, i: (kb0 + b, B_HEADS + h))],
        [q, kv, kr, kv], prev)
    return pl.pallas_call(
        kern, grid=(nb, B_HEADS, nq), in_specs=in_specs,
        out_specs=pl.BlockSpec((tq, LANE), lambda b, h, i: (qb0 + b * nq + i, h)),
        out_shape=out_shape, scratch_shapes=scratch + [pltpu.VMEM((tq, tk), F32)],
        input_output_aliases=aliases,
        compiler_params=_cparams(("parallel", "parallel", "arbitrary")),
    )(*args)


def _spatial_gate_kernel(u_ref, v_ref, w_ref, b_ref, o_ref, *, n_chunks):
    gw = u_ref.shape[1] // C_GROUPS
    row = lax.broadcasted_iota(jnp.int32, (C_BLOCK, C_BLOCK), 0)
    col = lax.broadcasted_iota(jnp.int32, (C_BLOCK, C_BLOCK), 1)
    for g in range(C_GROUPS):
        w = jnp.where(row >= col, w_ref[g], jnp.zeros((C_BLOCK, C_BLOCK), BF16))
        bias = b_ref[g]
        for c in range(n_chunks):
            rows = slice(c * C_BLOCK, (c + 1) * C_BLOCK)
            cols = slice(g * gw, (g + 1) * gw)
            vv = v_ref[rows, cols].astype(BF16)
            sg = jnp.dot(w, vv, preferred_element_type=F32) + bias
            o_ref[rows, cols] = (u_ref[rows, cols].astype(F32) * sg).astype(o_ref.dtype)


def _spatial_gate(u, v, w_sel, b_sel, n_prompt_rows):
    m, width = u.shape
    tm = _pick(m, (1024, 512, 256, 128))
    tm = math.gcd(tm, n_prompt_rows)
    npt = n_prompt_rows // tm
    sel = lambda i: jnp.where(i >= npt, 1, 0)
    kern = functools.partial(_spatial_gate_kernel, n_chunks=tm // C_BLOCK)
    return pl.pallas_call(
        kern, grid=(m // tm,),
        in_specs=[
            pl.BlockSpec((tm, width), lambda i: (i, 0)),
            pl.BlockSpec((tm, width), lambda i: (i, 0)),
            pl.BlockSpec((None, C_GROUPS, C_BLOCK, C_BLOCK), lambda i: (sel(i), 0, 0, 0)),
            pl.BlockSpec((None, C_GROUPS, C_BLOCK, 1), lambda i: (sel(i), 0, 0, 0)),
        ],
        out_specs=pl.BlockSpec((tm, width), lambda i: (i, 0)),
        out_shape=jax.ShapeDtypeStruct((m, width), BF16),
        compiler_params=_cparams(("parallel",)),
    )(u, v, w_sel, b_sel)


def _pool_kernel(x_ref, prev_ref, hist_ref, w_ref, sc_ref, o_ref, xp_sc, *, tm, past):
    i = pl.program_id(1)
    xp_sc[0:D_HALO, :] = jnp.where(i == 0, hist_ref[...], prev_ref[...])
    xp_sc[D_HALO:D_HALO + tm, :] = x_ref[...]
    gw = x_ref.shape[1] // len(D_WINDOWS)
    pos = past + i * tm + lax.broadcasted_iota(jnp.int32, (tm, 1), 0)
    for gi, win in enumerate(D_WINDOWS):
        cols = slice(gi * gw, (gi + 1) * gw)
        tot = xp_sc[D_HALO:D_HALO + tm, cols]
        for j in range(1, win):
            tot = tot + xp_sc[D_HALO - j:D_HALO - j + tm, cols]
        cnt = jnp.minimum(pos + 1, win).astype(F32)
        pooled = tot / cnt - xp_sc[D_HALO:D_HALO + tm, cols]
        y = jnp.dot(pooled.astype(BF16), w_ref[gi], preferred_element_type=F32)
        o_ref[:, cols] = (y * sc_ref[:, cols]).astype(o_ref.dtype)


def _pool(d_in, hist, w_grp, scale, *, nb, lq, row0, past, prev=None):
    width = d_in.shape[1]
    tm = _pick(lq, (512, 256, 128, 64))
    nt = lq // tm
    rb0 = row0 // tm
    hb0 = row0 // D_HALO
    per = tm // D_HALO
    kern, in_specs, args, aliases = _into_existing(
        functools.partial(_pool_kernel, tm=tm, past=past),
        [pl.BlockSpec((tm, width), lambda b, i: (rb0 + b * nt + i, 0)),
         pl.BlockSpec((D_HALO, width), lambda b, i: (jnp.maximum(hb0 + (b * nt + i) * per - 1, 0), 0)),
         pl.BlockSpec((None, D_HALO, width), lambda b, i: (b, 0, 0)),
         pl.BlockSpec(w_grp.shape, lambda b, i: (0, 0, 0)),
         pl.BlockSpec((1, width), lambda b, i: (0, 0))],
        [d_in, d_in, hist, w_grp, scale], prev)
    return pl.pallas_call(
        kern, grid=(nb, nt), in_specs=in_specs,
        out_specs=pl.BlockSpec((tm, width), lambda b, i: (rb0 + b * nt + i, 0)),
        out_shape=jax.ShapeDtypeStruct((d_in.shape[0], width), BF16),
        scratch_shapes=[pltpu.VMEM((D_HALO + tm, width), F32)], input_output_aliases=aliases,
        compiler_params=_cparams(("parallel", "arbitrary")),
    )(*args)


def _merge_kernel(x_ref, wg0, wg1, wg2, wg3, b0, b1, b2, b3, wb_ref, o_ref):
    x = x_ref[...]
    y = None
    for n, (wg, br) in enumerate(((wg0, b0), (wg1, b1), (wg2, b2), (wg3, b3))):
        gate = jax.nn.sigmoid(jnp.dot(x, wg[...], preferred_element_type=F32))
        t = gate * jnp.dot(br[...], wb_ref[n], preferred_element_type=F32)
        y = t if y is None else y + t
    o_ref[...] = y.astype(o_ref.dtype)


def _merge(xb, w_gate, branches, w_branch):
    m, d = xb.shape
    bw = branches[0].shape[1]
    tm = _pick(m, (512, 256, 128))
    tn = 512
    nj = d // tn
    in_specs = [pl.BlockSpec((tm, d), lambda i, j: (i, 0))]
    in_specs += [pl.BlockSpec((d, tn), lambda i, j, n=n: (0, n * nj + j)) for n in range(N_BRANCH)]
    in_specs += [pl.BlockSpec((tm, bw), lambda i, j: (i, 0)) for _ in range(N_BRANCH)]
    in_specs += [pl.BlockSpec((N_BRANCH, bw, tn), lambda i, j: (0, 0, j))]
    return pl.pallas_call(
        _merge_kernel, grid=(m // tm, nj), in_specs=in_specs,
        out_specs=pl.BlockSpec((tm, tn), lambda i, j: (i, j)),
        out_shape=jax.ShapeDtypeStruct((m, d), BF16),
        compiler_params=_cparams(("parallel", "arbitrary")),
    )(xb, w_gate, w_gate, w_gate, w_gate, *branches, w_branch)


def _out_ln_kernel(y_ref, x_ref, w_ref, g_ref, b_ref, o_ref, ob_ref, *, alpha):
    m = jnp.dot(y_ref[...], w_ref[...], preferred_element_type=F32)
    z = _ln_rows(alpha * x_ref[...] + m, g_ref[...], b_ref[...])
    o_ref[...] = z
    ob_ref[...] = z.astype(ob_ref.dtype)


def _out_ln(y, x, w_out, g, b, alpha):
    m, d = x.shape
    tm = _pick(m, (256, 128))
    row = lambda i: (i, 0)
    fixed = lambda i: (0, 0)
    return pl.pallas_call(
        functools.partial(_out_ln_kernel, alpha=alpha), grid=(m // tm,),
        in_specs=[pl.BlockSpec((tm, d), row), pl.BlockSpec((tm, d), row), pl.BlockSpec((d, d), fixed),
                  pl.BlockSpec((1, d), fixed), pl.BlockSpec((1, d), fixed)],
        out_specs=[pl.BlockSpec((tm, d), row), pl.BlockSpec((tm, d), row)],
        out_shape=[jax.ShapeDtypeStruct((m, d), F32), jax.ShapeDtypeStruct((m, d), BF16)],
        compiler_params=_cparams(("parallel",)),
    )(y, x, w_out, g, b)


def _ffn_kernel(xb_ref, x_ref, wg_ref, wu_ref, wd_ref, g_ref, b_ref, o_ref, ob_ref, acc_sc, *, alpha):
    f = pl.program_id(1)

    @pl.when(f == 0)
    def _():
        acc_sc[...] = jnp.zeros(acc_sc.shape, F32)

    xb = xb_ref[...]
    hg = jnp.dot(xb, wg_ref[...], preferred_element_type=F32)
    hu = jnp.dot(xb, wu_ref[...], preferred_element_type=F32)
    h = hg * jax.nn.sigmoid(hg) * hu
    acc_sc[...] += jnp.dot(h.astype(BF16), wd_ref[...], preferred_element_type=F32)

    @pl.when(f == pl.num_programs(1) - 1)
    def _():
        z = _ln_rows(alpha * x_ref[...] + acc_sc[...], g_ref[...], b_ref[...])
        o_ref[...] = z
        ob_ref[...] = z.astype(ob_ref.dtype)


def _ffn(xb, x, wg, wu, wd, g, b, alpha):
    m, d = x.shape
    ff = wg.shape[1]
    tm = _pick(m, (512, 256, 128))
    tf = _pick(ff, (512, 256, 128))
    row = lambda i, f: (i, 0)
    fixed = lambda i, f: (0, 0)
    return pl.pallas_call(
        functools.partial(_ffn_kernel, alpha=alpha), grid=(m // tm, ff // tf),
        in_specs=[pl.BlockSpec((tm, d), row), pl.BlockSpec((tm, d), row),
                  pl.BlockSpec((d, tf), lambda i, f: (0, f)), pl.BlockSpec((d, tf), lambda i, f: (0, f)),
                  pl.BlockSpec((tf, d), lambda i, f: (f, 0)),
                  pl.BlockSpec((1, d), fixed), pl.BlockSpec((1, d), fixed)],
        out_specs=[pl.BlockSpec((tm, d), row), pl.BlockSpec((tm, d), row)],
        out_shape=[jax.ShapeDtypeStruct((m, d), F32), jax.ShapeDtypeStruct((m, d), BF16)],
        scratch_shapes=[pltpu.VMEM((tm, d), F32)],
        compiler_params=_cparams(("parallel", "arbitrary")),
    )(xb, x, wg, wu, wd, g, b)


def _router_kernel(x_ref, w_ref, o_ref, sel_ref, *, n_experts):
    logits = jnp.dot(x_ref[...], w_ref[...], preferred_element_type=F32,
                     precision=lax.Precision.HIGHEST)
    lane = lax.broadcasted_iota(jnp.int32, logits.shape, 1).astype(F32)
    lg = jnp.where(lane < n_experts, logits, NEG)
    m1 = jnp.max(lg, axis=1, keepdims=True)
    i1 = jnp.min(jnp.where(lg == m1, lane, float(LANE)), axis=1, keepdims=True)
    lg2 = jnp.where(lane == i1, NEG, lg)
    m2 = jnp.max(lg2, axis=1, keepdims=True)
    i2 = jnp.min(jnp.where(lg2 == m2, lane, float(LANE)), axis=1, keepdims=True)
    ex = jnp.exp(m2 - m1)
    p1 = 1.0 / (1.0 + ex)
    p2 = ex / (1.0 + ex)
    o_ref[...] = (jnp.where(lane == 0.0, i1, 0.0) + jnp.where(lane == 1.0, i2, 0.0)
                  + jnp.where(lane == 2.0, p1, 0.0) + jnp.where(lane == 3.0, p2, 0.0))
    sel_ref[...] = jnp.where((lane == i1) | (lane == i2), 1.0, 0.0).astype(sel_ref.dtype)


def _router(x, w_router):
    m, d = x.shape
    ne = w_router.shape[1]
    w = jnp.pad(w_router, ((0, 0), (0, LANE - ne)))
    tm = _pick(m, (512, 256, 128))
    row = lambda i: (i, 0)
    return pl.pallas_call(
        functools.partial(_router_kernel, n_experts=ne), grid=(m // tm,),
        in_specs=[pl.BlockSpec((tm, d), row), pl.BlockSpec((d, LANE), lambda i: (0, 0))],
        out_specs=[pl.BlockSpec((tm, LANE), row), pl.BlockSpec((tm, LANE), row)],
        out_shape=[jax.ShapeDtypeStruct((m, LANE), F32), jax.ShapeDtypeStruct((m, LANE), BF16)],
        compiler_params=_cparams(("parallel",)),
    )(x, w)


def _slot_kernel(sel_ref, rec_ref, pos_ref, meta_ref, cnt_sc, off_sc, *, tile, n_experts):
    phase = pl.program_id(0)
    i = pl.program_id(1)
    sel = sel_ref[...]
    tm = sel.shape[0]
    lane = lax.broadcasted_iota(jnp.int32, (1, LANE), 1)

    @pl.when((phase == 0) & (i == 0))
    def _():
        cnt_sc[...] = jnp.zeros(cnt_sc.shape, F32)

    @pl.when(phase == 0)
    def _():
        cnt_sc[...] += jnp.sum(sel.astype(F32), axis=0, keepdims=True)

    @pl.when((phase == 1) & (i == 0))
    def _():
        counts = cnt_sc[...]
        padded = jnp.ceil(counts / tile) * tile
        starts = jnp.zeros((1, LANE), F32)
        for e in range(1, n_experts):
            before = jnp.sum(jnp.where(lane < e, padded, 0.0), axis=1, keepdims=True)
            starts = jnp.where(lane == e, before, starts)
        off_sc[...] = starts
        row = lax.broadcasted_iota(jnp.int32, meta_ref.shape, 0)
        meta_ref[...] = jnp.where(row == 0, counts, jnp.where(row == 1, starts, 0.0))
        cnt_sc[...] = jnp.zeros(cnt_sc.shape, F32)

    @pl.when(phase == 1)
    def _():
        r = lax.broadcasted_iota(jnp.int32, (tm, tm), 0)
        c = lax.broadcasted_iota(jnp.int32, (tm, tm), 1)
        tri = jnp.where(r >= c, 1.0, 0.0).astype(BF16)
        csum = jnp.dot(tri, sel, preferred_element_type=F32)
        slot = off_sc[...] + cnt_sc[...] + csum - sel.astype(F32)
        rec = rec_ref[...]
        lanef = lane.astype(F32)
        pos1 = jnp.sum(jnp.where(lanef == rec[:, 0:1], slot, 0.0), axis=1, keepdims=True)
        pos2 = jnp.sum(jnp.where(lanef == rec[:, 1:2], slot, 0.0), axis=1, keepdims=True)
        pos_ref[...] = jnp.where(lane == 0, pos1, 0.0) + jnp.where(lane == 1, pos2, 0.0)
        cnt_sc[...] += csum[tm - 1:tm, :]


def _slots(sel, rec, tile, n_experts):
    m = sel.shape[0]
    tm = _pick(m, (512, 256, 128))
    row = lambda p, i: (i, 0)
    return pl.pallas_call(
        functools.partial(_slot_kernel, tile=tile, n_experts=n_experts), grid=(2, m // tm),
        in_specs=[pl.BlockSpec((tm, LANE), row), pl.BlockSpec((tm, LANE), row)],
        out_specs=[pl.BlockSpec((tm, LANE), lambda p, i: (i * p, 0)),
                   pl.BlockSpec((8, LANE), lambda p, i: (0, 0))],
        out_shape=[jax.ShapeDtypeStruct((m, LANE), F32), jax.ShapeDtypeStruct((8, LANE), F32)],
        scratch_shapes=[pltpu.VMEM((1, LANE), F32), pltpu.VMEM((1, LANE), F32)],
        compiler_params=_cparams(("arbitrary", "arbitrary")),
    )(sel, rec)


def _row_copy(src, src_row, dst, dst_row, sem):
    return pltpu.make_async_copy(src.at[pl.ds(src_row, 1)], dst.at[pl.ds(dst_row, 1)], sem)


def _dispatch_kernel(pos1_ref, pos2_ref, pad_row_ref, pad_len_ref, x_ref, xs_hbm, zero_sc, sem, *, tm):
    step = pl.program_id(0)
    base = step * tm

    @pl.when(step == 0)
    def _():
        zero_sc[...] = jnp.zeros(zero_sc.shape, zero_sc.dtype)
        for e in range(pad_row_ref.shape[0]):
            def fill(r, carry, e=e):
                _row_copy(zero_sc, 0, xs_hbm, pad_row_ref[e] + r, sem).start()
                return carry

            def fill_wait(r, carry):
                _row_copy(zero_sc, 0, xs_hbm, 0, sem).wait()
                return carry

            lax.fori_loop(0, pad_len_ref[e], fill, 0)
            lax.fori_loop(0, pad_len_ref[e], fill_wait, 0)

    def issue(r, carry):
        t = base + r
        _row_copy(x_ref, r, xs_hbm, pos1_ref[t], sem).start()
        _row_copy(x_ref, r, xs_hbm, pos2_ref[t], sem).start()
        return carry

    lax.fori_loop(0, tm, issue, 0)

    def drain(r, carry):
        _row_copy(x_ref, 0, xs_hbm, 0, sem).wait()
        _row_copy(x_ref, 0, xs_hbm, 0, sem).wait()
        return carry

    lax.fori_loop(0, tm, drain, 0)


def _dispatch(x, pos1, pos2, pad_row, pad_len, n_rows):
    m, d = x.shape
    tm = _pick(m, (512, 256, 128))
    kern = functools.partial(_dispatch_kernel, tm=tm)
    return pl.pallas_call(
        kern,
        grid_spec=pltpu.PrefetchScalarGridSpec(
            num_scalar_prefetch=4, grid=(m // tm,),
            in_specs=[pl.BlockSpec((tm, d), lambda i, *_: (i, 0))],
            out_specs=pl.BlockSpec(memory_space=pl.ANY),
            scratch_shapes=[pltpu.VMEM((8, d), x.dtype), pltpu.SemaphoreType.DMA]),
        out_shape=jax.ShapeDtypeStruct((n_rows, d), x.dtype),
        compiler_params=_cparams(("arbitrary",)),
    )(pos1, pos2, pad_row, pad_len, x)


def _grouped_ffn_kernel(tile_e_ref, n_used_ref, x_ref, wg_ref, wu_ref, wd_ref, o_ref, acc_sc):
    del tile_e_ref
    i = pl.program_id(0)
    f = pl.program_id(1)
    used = i < n_used_ref[0]

    @pl.when(f == 0)
    def _():
        acc_sc[...] = jnp.zeros(acc_sc.shape, F32)

    @pl.when(used)
    def _():
        xb = x_ref[...].astype(BF16)
        hg = jnp.dot(xb, wg_ref[...], preferred_element_type=F32)
        hu = jnp.dot(xb, wu_ref[...], preferred_element_type=F32)
        h = hg * jax.nn.sigmoid(hg) * hu
        acc_sc[...] += jnp.dot(h.astype(BF16), wd_ref[...], preferred_element_type=F32)

    @pl.when(f == pl.num_programs(1) - 1)
    def _():
        o_ref[...] = acc_sc[...]


def _grouped_ffn(xs, tile_e, n_used, wg, wu, wd, tile):
    p, d = xs.shape
    ff = wg.shape[2]
    tf = _pick(ff, (1024, 512, 256, 128))
    row = lambda i, f, te, nu: (i, 0)
    return pl.pallas_call(
        _grouped_ffn_kernel,
        grid_spec=pltpu.PrefetchScalarGridSpec(
            num_scalar_prefetch=2, grid=(p // tile, ff // tf),
            in_specs=[pl.BlockSpec((tile, d), lambda i, f, te, nu: (jnp.minimum(i, nu[0] - 1), 0)),
                      pl.BlockSpec((None, d, tf), lambda i, f, te, nu: (te[i], 0, f)),
                      pl.BlockSpec((None, d, tf), lambda i, f, te, nu: (te[i], 0, f)),
                      pl.BlockSpec((None, tf, d), lambda i, f, te, nu: (te[i], f, 0))],
            out_specs=pl.BlockSpec((tile, d), row),
            scratch_shapes=[pltpu.VMEM((tile, d), F32)]),
        out_shape=jax.ShapeDtypeStruct((p, d), F32),
        compiler_params=_cparams(("parallel", "arbitrary")),
    )(tile_e, n_used, xs, wg, wu, wd)


def _combine_kernel(pos1_ref, pos2_ref, x_ref, rec_ref, g_ref, b_ref, ys_hbm, o_ref, o2_ref,
                    y1_sc, y2_sc, sem, *, alpha, tm, n_prompt_tiles):
    step = pl.program_id(0)
    base = step * tm

    def issue(r, carry):
        t = base + r
        _row_copy(ys_hbm, pos1_ref[t], y1_sc, r, sem).start()
        _row_copy(ys_hbm, pos2_ref[t], y2_sc, r, sem).start()
        return carry

    lax.fori_loop(0, tm, issue, 0)

    def drain(r, carry):
        _row_copy(ys_hbm, 0, y1_sc, 0, sem).wait()
        _row_copy(ys_hbm, 0, y2_sc, 0, sem).wait()
        return carry

    lax.fori_loop(0, tm, drain, 0)
    rec = rec_ref[...]
    f = rec[:, 2:3] * y1_sc[...] + rec[:, 3:4] * y2_sc[...]
    z = _ln_rows(alpha * x_ref[...] + f, g_ref[...], b_ref[...])
    if n_prompt_tiles is None:
        o_ref[...] = z
        o2_ref[...] = z.astype(o2_ref.dtype)
    else:
        @pl.when(step < n_prompt_tiles)
        def _():
            o_ref[...] = z

        @pl.when(step >= n_prompt_tiles)
        def _():
            o2_ref[...] = z


def _combine(x, rec, ys, pos1, pos2, g, b, alpha, final_prompt_rows=None):
    m, d = x.shape
    tm = _pick(m, (256, 128))
    row = lambda i, p1, p2: (i, 0)
    fixed = lambda i, p1, p2: (0, 0)
    if final_prompt_rows is None:
        n_p = None
        out_specs = [pl.BlockSpec((tm, d), row), pl.BlockSpec((tm, d), row)]
        out_shape = [jax.ShapeDtypeStruct((m, d), F32), jax.ShapeDtypeStruct((m, d), BF16)]
    else:
        tm = math.gcd(tm, final_prompt_rows)
        n_p = final_prompt_rows // tm
        out_specs = [pl.BlockSpec((tm, d), lambda i, p1, p2: (jnp.minimum(i, n_p - 1), 0)),
                     pl.BlockSpec((tm, d), lambda i, p1, p2: (jnp.maximum(i - n_p, 0), 0))]
        out_shape = [jax.ShapeDtypeStruct((final_prompt_rows, d), F32),
                     jax.ShapeDtypeStruct((m - final_prompt_rows, d), F32)]
    return pl.pallas_call(
        functools.partial(_combine_kernel, alpha=alpha, tm=tm, n_prompt_tiles=n_p),
        grid_spec=pltpu.PrefetchScalarGridSpec(
            num_scalar_prefetch=2, grid=(m // tm,),
            in_specs=[pl.BlockSpec((tm, d), row), pl.BlockSpec((tm, LANE), row),
                      pl.BlockSpec((1, d), fixed), pl.BlockSpec((1, d), fixed),
                      pl.BlockSpec(memory_space=pl.ANY)],
            out_specs=out_specs,
            scratch_shapes=[pltpu.VMEM((tm, d), F32), pltpu.VMEM((tm, d), F32),
                            pltpu.SemaphoreType.DMA]),
        out_shape=out_shape,
        compiler_params=_cparams(("arbitrary",)),
    )(pos1, pos2, x, rec, g, b, ys)


MOE_TILE = 512


def _routed_moe(x, w_router, wg, wu, wd, g, b, alpha, final_prompt_rows=None):
    m, d = x.shape
    ne = w_router.shape[1]
    rec, sel = _router(x, w_router)
    pos, meta = _slots(sel, rec, MOE_TILE, ne)
    pos1 = pos[:, 0].astype(jnp.int32)
    pos2 = pos[:, 1].astype(jnp.int32)
    n_tiles = (2 * m) // MOE_TILE + ne
    counts, starts = meta[0, :ne], meta[1, :ne]
    ends = starts + jnp.ceil(counts / MOE_TILE) * MOE_TILE
    tile_start = jnp.arange(n_tiles, dtype=F32) * MOE_TILE
    tile_e = jnp.minimum(jnp.sum(ends[None, :] <= tile_start[:, None], axis=1), ne - 1).astype(jnp.int32)
    n_used = (ends[ne - 1:ne] / MOE_TILE).astype(jnp.int32)
    pad_row = (starts + counts).astype(jnp.int32)
    pad_len = (ends - starts - counts).astype(jnp.int32)
    xs = _dispatch(x, pos1, pos2, pad_row, pad_len, n_tiles * MOE_TILE)
    ys = _grouped_ffn(xs, tile_e, n_used, wg, wu, wd, MOE_TILE)
    return _combine(x, rec, ys, pos1, pos2, g, b, alpha, final_prompt_rows)


def _epi_scale(scale):
    return lambda accs, tabs, vecs: [accs[0] * scale]


def _epi_f32_and_bf16(accs, tabs, vecs):
    return [accs[0], accs[0]]


def _epi_rms(accs, tabs, vecs):
    r = _rms_rows(accs[0], vecs[0])
    return [r, r]


def _epi_rope(accs, tabs, vecs):
    return [accs[0] * tabs[0] + accs[1] * tabs[1]]


def _epi_gelu(accs, tabs, vecs):
    return [jax.nn.gelu(accs[0])]


def _epi_gelu_ln(accs, tabs, vecs):
    return [_ln_rows(jax.nn.gelu(accs[0]), vecs[0], vecs[1])]


def _epi_plain(accs, tabs, vecs):
    return [accs[0]]


def _epi_q_rope(scale):
    return lambda accs, tabs, vecs: [(accs[0] * tabs[0] + accs[1] * tabs[1]) * scale]


def _rope_tables(pos):
    half = B_ROPE // 2
    inv = ROPE_THETA ** (-jnp.arange(half, dtype=F32) / half)
    ang = pos.astype(F32)[:, None] * inv[None, :]
    cos, sin = jnp.cos(ang), jnp.sin(ang)
    cos2 = jnp.concatenate([cos, cos], axis=1)
    sin2 = jnp.concatenate([-sin, sin], axis=1)
    return cos2, sin2


def _swap_halves(w):
    half = w.shape[-1] // 2
    return jnp.concatenate([w[..., half:], w[..., :half]], axis=-1)


def kernel(x_prompt, x_sample, cache_a_k, cache_a_v, cache_b_latent, cache_b_krope, state_d_pool, w_in, a_lambda, a_subln, b_q_norm, b_w_uq, b_kv_norm, b_w_ukv, c_ln_g, c_ln_b, c_w_s, c_b_s, d_w_grp, d_scale, w_branch, w_out, ln1_g, ln1_b, ln2_g, ln2_b, f_w_gate, f_w_up, f_w_down, m_router, m_w_gate, m_w_up, m_w_down):
    depth = w_in.shape[0]
    nbp, lp, d_model = x_prompt.shape
    nbs, ls, _ = x_sample.shape
    past = cache_a_k.shape[2]
    lks = past + ls
    tp, ts = nbp * lp, nbs * ls
    aw = A_HEADS * 2 * A_DH
    q_rank = b_q_norm.shape[1]
    kv_rank = b_kv_norm.shape[1]
    c_width = c_ln_g.shape[1]
    d_width = d_scale.shape[1]
    alpha = (2.0 * depth) ** 0.25
    sizes = (aw, aw, aw, q_rank, kv_rank, B_ROPE, c_width, c_width, d_width) + (d_model,) * N_BRANCH
    offs = [0]
    for s in sizes:
        offs.append(offs[-1] + s)

    x = jnp.concatenate([x_prompt.reshape(tp, d_model), x_sample.reshape(ts, d_model)], axis=0)
    xb = x.astype(BF16)

    pos = jnp.concatenate([jnp.tile(jnp.arange(lp, dtype=jnp.int32), nbp),
                           jnp.tile(past + jnp.arange(ls, dtype=jnp.int32), nbs)])
    cos2, sin2 = _rope_tables(pos)
    rows = pos.shape[0]
    ones = jnp.ones((rows, B_NOPE), F32)
    zeros_n = jnp.zeros((rows, B_NOPE), F32)
    zeros_r = jnp.zeros((rows, 2 * LANE - B_NOPE - B_ROPE), F32)
    q_cos = jnp.tile(jnp.concatenate([ones, cos2, zeros_r], axis=1), (1, 2))
    q_sin = jnp.tile(jnp.concatenate([zeros_n, sin2, zeros_r], axis=1), (1, 2))

    slopes = jnp.exp2(-8.0 * jnp.arange(1, A_HEADS + 1, dtype=F32) / A_HEADS) * LOG2E
    slopes = jnp.broadcast_to(slopes[:, None, None], (A_HEADS, 1, LANE))

    states_p, states_s = [], []
    kv_state = lat_state = None
    for l in range(depth):
        wl = w_in[l]
        seg = lambda k: wl[:, offs[k]:offs[k + 1]].astype(BF16)
        lam_init = 0.8 - 0.6 * math.exp(-0.3 * l)

        (q_a,) = _fused_mm(_epi_scale(A_DH ** -0.5 * LOG2E), xb, [seg(0)], [BF16], tn=1024)
        w_kv = jnp.concatenate([seg(1), seg(2)], axis=1)
        kv_a_b, kv_state = _state_proj(xb, w_kv, None, tn=aw, layer=l, depth=depth, n_prompt_rows=tp,
                                       prev=kv_state)
        subln = a_subln[l].reshape(1, 2 * A_DH)
        o_a = _diff_attn(q_a, kv_a_b, slopes, a_lambda[l], subln, nb=nbp, lq=lp, lk=lp, q_off=0,
                         q_row0=0, lam_init=lam_init, out_rows=rows)
        cache_kv = jnp.concatenate([cache_a_k[l].reshape(nbs, past, aw),
                                    cache_a_v[l].reshape(nbs, past, aw)], axis=2).astype(BF16)
        kv_s = jnp.concatenate([cache_kv, kv_a_b[tp:].reshape(nbs, ls, 2 * aw)], axis=1)
        o_a = _diff_attn(q_a, kv_s.reshape(nbs * lks, 2 * aw), slopes, a_lambda[l], subln, nb=nbs,
                         lq=ls, lk=lks, q_off=past, q_row0=tp, lam_init=lam_init, out_rows=rows, prev=o_a)

        (cq,) = _fused_mm(_epi_rms, xb, [seg(3)], [BF16], tn=q_rank,
                          vecs=[b_q_norm[l].reshape(1, q_rank)])
        lat_b, lat_state = _state_proj(xb, seg(4), b_kv_norm[l].reshape(1, kv_rank), tn=kv_rank, layer=l,
                                       depth=depth, n_prompt_rows=tp, prev=lat_state)
        w_kr = wl[:, offs[5]:offs[6]]
        (kr,) = _fused_mm(_epi_rope, xb, [w_kr.astype(BF16), _swap_halves(w_kr).astype(BF16)], [F32],
                          tn=B_ROPE, tabs=[cos2, sin2])
        w_uq = b_w_uq[l].reshape(q_rank, B_HEADS, B_NOPE + B_ROPE)
        w_uq_n, w_uq_r = w_uq[..., :B_NOPE], w_uq[..., B_NOPE:]
        zpad = jnp.zeros((q_rank, B_HEADS, 2 * LANE - B_NOPE - B_ROPE), F32)
        w_q1 = jnp.concatenate([w_uq_n, w_uq_r, zpad], axis=-1).reshape(q_rank, -1).astype(BF16)
        w_q2 = jnp.concatenate([jnp.zeros_like(w_uq_n), _swap_halves(w_uq_r), zpad],
                               axis=-1).reshape(q_rank, -1).astype(BF16)
        (q_b,) = _fused_mm(_epi_q_rope((B_NOPE + B_ROPE) ** -0.5 * LOG2E), cq, [w_q1, w_q2], [BF16],
                           tn=4 * LANE, tabs=[q_cos, q_sin])
        w_ukv = b_w_ukv[l].reshape(kv_rank, B_HEADS, B_NOPE + B_VDIM)
        w_up = jnp.concatenate([w_ukv[..., :B_NOPE].reshape(kv_rank, -1),
                                w_ukv[..., B_NOPE:].reshape(kv_rank, -1)], axis=1).astype(BF16)
        lat_s = jnp.concatenate([cache_b_latent[l].astype(BF16), lat_b[tp:].reshape(nbs, ls, kv_rank)],
                                axis=1).reshape(nbs * lks, kv_rank)
        lat_cat = jnp.concatenate([lat_b[:tp], lat_s], axis=0)
        (kv_b,) = _fused_mm(_epi_plain, lat_cat, [w_up], [BF16], tn=1024)
        kr_s = jnp.concatenate([cache_b_krope[l], kr[tp:].reshape(nbs, ls, B_ROPE)],
                               axis=1).reshape(nbs * lks, B_ROPE)
        kr_cat = jnp.pad(jnp.concatenate([kr[:tp], kr_s], axis=0),
                         ((0, 0), (0, LANE - B_ROPE))).astype(BF16)
        o_b = _mla_attn(q_b, kv_b, kr_cat, nb=nbp, lq=lp, lk=lp, q_off=0, q_row0=0, kv_row0=0,
                        out_rows=rows)
        if tp % lks:
            kv_b_s, kr_b_s, s_row0 = kv_b[tp:], kr_cat[tp:], 0
        else:
            kv_b_s, kr_b_s, s_row0 = kv_b, kr_cat, tp
        o_b = _mla_attn(q_b, kv_b_s, kr_b_s, nb=nbs, lq=ls, lk=lks, q_off=past, q_row0=tp,
                        kv_row0=s_row0, out_rows=rows, prev=o_b)

        (u_c,) = _fused_mm(_epi_gelu, xb, [seg(6)], [BF16], tn=1024)
        (v_c,) = _fused_mm(_epi_gelu_ln, xb, [seg(7)], [F32], tn=c_width,
                           vecs=[c_ln_g[l].reshape(1, c_width), c_ln_b[l].reshape(1, c_width)])
        w_s, b_s = c_w_s[l], c_b_s[l]
        reps = C_BLOCK // ls if ls < C_BLOCK else 1
        w_tl = w_s[:, :C_BLOCK // reps, :C_BLOCK // reps]
        w_smp = jnp.zeros_like(w_s)
        for r in range(reps):
            sl = slice(r * (C_BLOCK // reps), (r + 1) * (C_BLOCK // reps))
            w_smp = w_smp.at[:, sl, sl].set(w_tl)
        b_smp = jnp.tile(b_s[:, :C_BLOCK // reps], (1, reps))
        w_sel = jnp.stack([w_s, w_smp]).astype(BF16)
        b_sel = jnp.stack([b_s, b_smp])[..., None]
        o_c = _spatial_gate(u_c, v_c, w_sel, b_sel, tp)

        (d_in,) = _fused_mm(_epi_plain, xb, [seg(8)], [F32], tn=1024)
        w_grp = d_w_grp[l].astype(BF16)
        scale = d_scale[l].reshape(1, d_width)
        hist_p = jnp.zeros((nbp, D_HALO, d_width), F32)
        hist_s = jnp.pad(state_d_pool[l], ((0, 0), (D_HALO - D_HIST, 0), (0, 0)))
        o_d = _pool(d_in, hist_p, w_grp, scale, nb=nbp, lq=lp, row0=0, past=0)
        o_d = _pool(d_in, hist_s, w_grp, scale, nb=nbs, lq=ls, row0=tp, past=past, prev=o_d)

        w_gate = wl[:, offs[9]:offs[13]].astype(BF16)
        y = _merge(xb, w_gate, [o_a, o_b, o_c, o_d], w_branch[l].astype(BF16))
        x, xb = _out_ln(y, x, w_out[l].astype(BF16), ln1_g[l].reshape(1, -1), ln1_b[l].reshape(1, -1),
                        alpha)

        i = l // 2
        g2, b2 = ln2_g[l].reshape(1, -1), ln2_b[l].reshape(1, -1)
        if l % 2 == 0:
            x, xb = _ffn(xb, x, f_w_gate[i].astype(BF16), f_w_up[i].astype(BF16),
                         f_w_down[i].astype(BF16), g2, b2, alpha)
        else:
            x, xb = _routed_moe(x, m_router[i], m_w_gate[i].astype(BF16), m_w_up[i].astype(BF16),
                                m_w_down[i].astype(BF16), g2, b2, alpha,
                                final_prompt_rows=tp if l == depth - 1 else None)

        states_p.append((kr[:tp].reshape(nbp, lp, B_ROPE),
                         d_in[:tp].reshape(nbp, lp, d_width)[:, lp - D_HIST:]))
        states_s.append((kr[tp:].reshape(nbs, ls, B_ROPE), v_c[tp:].reshape(nbs, ls, c_width),
                         jnp.concatenate([state_d_pool[l], d_in[tp:].reshape(nbs, ls, d_width)],
                                         axis=1)[:, -D_HIST:]))

    if depth % 2 == 0:
        yp, ys = x.reshape(nbp, lp, d_model), xb.reshape(nbs, ls, d_model)
    else:
        yp, ys = x[:tp].reshape(nbp, lp, d_model), x[tp:].reshape(nbs, ls, d_model)
    stack = lambda sts, k: jnp.stack([s[k] for s in sts])
    k_p, k_s, v_p, v_s = kv_state
    lat_p, lat_s = lat_state
    heads_p = (depth, nbp, lp, A_HEADS, 2 * A_DH)
    heads_s = (depth, nbs, ls, A_HEADS, 2 * A_DH)
    return (yp, ys, k_p.reshape(heads_p), v_p.reshape(heads_p), lat_p.reshape(depth, nbp, lp, kv_rank),
            stack(states_p, 0), stack(states_p, 1), k_s.reshape(heads_s), v_s.reshape(heads_s),
            lat_s.reshape(depth, nbs, ls, kv_rank), stack(states_s, 0), stack(states_s, 1),
            stack(states_s, 2))
```

```python
import functools
import math

import jax
import jax.numpy as jnp
from jax import lax
from jax.experimental import pallas as pl
from jax.experimental.pallas import tpu as pltpu

CHUNK = 64
A_HEADS = 8
A_DH = 64
B_HEADS = 8
B_NOPE = 128
B_ROPE = 64
B_VDIM = 128
C_GROUPS = 4
C_BLOCK = 128
D_WINDOWS = (2, 4, 8, 16)
D_HIST = 15
D_HALO = 16
N_BRANCH = 4
ROPE_THETA = 10000.0
LN_EPS = 1e-5
RMS_EPS = 1e-6
LOG2E = 1.4426950408889634
NEG = -1e30
LANE = 128
VMEM_LIMIT = 56 * 1024 * 1024

F32 = jnp.float32
BF16 = jnp.bfloat16


def _pick(n, cands):
    for c in cands:
        if n % c == 0:
            return c
    raise ValueError(f"no tile for {n} in {cands}")


def _cparams(sem):
    return pltpu.CompilerParams(dimension_semantics=sem, vmem_limit_bytes=VMEM_LIMIT)


def _ln_rows(z, g, b):
    mu = jnp.mean(z, axis=-1, keepdims=True)
    zc = z - mu
    var = jnp.mean(zc * zc, axis=-1, keepdims=True)
    return zc * lax.rsqrt(var + LN_EPS) * g + b


def _rms_rows(z, g):
    return z * lax.rsqrt(jnp.mean(z * z, axis=-1, keepdims=True) + RMS_EPS) * g


def _fused_mm_kernel(*refs, fn, nw, ntab, nvec, precision):
    x = refs[0][...]
    w_refs = refs[1:1 + nw]
    tab_refs = refs[1 + nw:1 + nw + ntab]
    vec_refs = refs[1 + nw + ntab:1 + nw + ntab + nvec]
    out_refs = refs[1 + nw + ntab + nvec:]
    accs = [jnp.dot(x, w[...], preferred_element_type=F32, precision=precision) for w in w_refs]
    res = fn(accs, [t[...] for t in tab_refs], [v[...] for v in vec_refs])
    for o, r in zip(out_refs, res):
        o[...] = r.astype(o.dtype)


def _fused_mm(fn, x, ws, out_dtypes, *, tn, tm=None, tabs=(), tabs_follow_cols=False, vecs=(),
              precision=None):
    m, kdim = x.shape
    n = ws[0].shape[1]
    tm = tm or _pick(m, (1024, 512, 256, 128))
    grid = (m // tm, n // tn)
    in_specs = [pl.BlockSpec((tm, kdim), lambda i, j: (i, 0))]
    in_specs += [pl.BlockSpec((kdim, tn), lambda i, j: (0, j)) for _ in ws]
    for t in tabs:
        if tabs_follow_cols:
            in_specs.append(pl.BlockSpec((tm, tn), lambda i, j: (i, j)))
        else:
            in_specs.append(pl.BlockSpec((tm, t.shape[1]), lambda i, j: (i, 0)))
    in_specs += [pl.BlockSpec((1, tn), lambda i, j: (0, j)) for _ in vecs]
    out_specs = [pl.BlockSpec((tm, tn), lambda i, j: (i, j)) for _ in out_dtypes]
    out_shape = [jax.ShapeDtypeStruct((m, n), d) for d in out_dtypes]
    kern = functools.partial(_fused_mm_kernel, fn=fn, nw=len(ws), ntab=len(tabs), nvec=len(vecs),
                             precision=precision)
    return pl.pallas_call(
        kern, grid=grid, in_specs=in_specs, out_specs=out_specs, out_shape=out_shape,
        compiler_params=_cparams(("parallel", "arbitrary")),
    )(x, *ws, *tabs, *vecs)


def _state_proj_kernel(*refs, n_col, n_prompt_tiles, has_gain, has_prev):
    x_ref, w_ref = refs[0], refs[1]
    pos = 2
    g_ref = refs[pos] if has_gain else None
    pos += (1 if has_gain else 0) + (2 * n_col if has_prev else 0)
    joint_ref = refs[pos]
    state_refs = refs[pos + 1:]
    i = pl.program_id(0)
    j = pl.program_id(1)
    acc = jnp.dot(x_ref[...], w_ref[...], preferred_element_type=F32)
    if has_gain:
        acc = _rms_rows(acc, g_ref[...])
    joint_ref[...] = acc.astype(joint_ref.dtype)
    for c in range(n_col):
        @pl.when((j == c) & (i < n_prompt_tiles))
        def _(c=c):
            state_refs[2 * c][...] = acc

        @pl.when((j == c) & (i >= n_prompt_tiles))
        def _(c=c):
            state_refs[2 * c + 1][...] = acc


def _state_proj(x, w, gain, *, tn, layer, depth, n_prompt_rows, prev):
    m, kdim = x.shape
    n = w.shape[1]
    n_col = n // tn
    tm = math.gcd(_pick(m, (512, 256, 128)), n_prompt_rows)
    n_p = n_prompt_rows // tm
    n_s = (m - n_prompt_rows) // tm
    in_specs = [pl.BlockSpec((tm, kdim), lambda i, j: (i, 0)), pl.BlockSpec((kdim, tn), lambda i, j: (0, j))]
    args = [x, w]
    if gain is not None:
        in_specs.append(pl.BlockSpec((1, tn), lambda i, j: (0, j)))
        args.append(gain)
    aliases = {}
    if prev is not None:
        for k, buf in enumerate(prev):
            aliases[len(args)] = 1 + k
            in_specs.append(pl.BlockSpec(memory_space=pl.ANY))
            args.append(buf)
    out_specs = [pl.BlockSpec((tm, tn), lambda i, j: (i, j))]
    out_shape = [jax.ShapeDtypeStruct((m, n), BF16)]
    for _ in range(n_col):
        out_specs.append(pl.BlockSpec((tm, tn), lambda i, j: (layer * n_p + jnp.minimum(i, n_p - 1), 0)))
        out_shape.append(jax.ShapeDtypeStruct((depth * n_p * tm, tn), F32))
        out_specs.append(pl.BlockSpec((tm, tn), lambda i, j: (layer * n_s + jnp.maximum(i - n_p, 0), 0)))
        out_shape.append(jax.ShapeDtypeStruct((depth * n_s * tm, tn), F32))
    kern = functools.partial(_state_proj_kernel, n_col=n_col, n_prompt_tiles=n_p,
                             has_gain=gain is not None, has_prev=prev is not None)
    outs = pl.pallas_call(
        kern, grid=(m // tm, n_col), in_specs=in_specs, out_specs=out_specs, out_shape=out_shape,
        input_output_aliases=aliases, compiler_params=_cparams(("arbitrary", "arbitrary")),
    )(*args)
    return outs[0], list(outs[1:])


ATTN_ROW_GROUPS = 2
DIFF_ATTN_TQ = 512
DIFF_ATTN_HEADS = 2
MLA_ATTN_TQ = 1024


def _row_groups(rows):
    g = ATTN_ROW_GROUPS
    while rows % (16 * g):
        g //= 2
    return g


def _nt_dot(a, b):
    return lax.dot_general(a, b, (((1,), (1,)), ((), ())), preferred_element_type=F32)


def _softmax_pv(scores, values, biases, m_sc, acc_sc):
    r0 = 0
    for s, v, bias in zip(scores, values, biases):
        width = v.shape[0]
        v_ext = jnp.concatenate([v, jnp.ones(v.shape, v.dtype)], axis=1)
        rows = slice(r0, r0 + s.shape[0])
        r0 += s.shape[0]
        if bias is not None:
            s = s + bias
        m_prev = m_sc[rows, :]
        m_new = jnp.maximum(m_prev, jnp.max(s, axis=1, keepdims=True))
        alpha = jnp.exp2(m_prev - m_new)
        m_rep = jnp.tile(m_new, (1, width // LANE)) if width >= LANE else m_new[:, :width]
        p = jnp.exp2(s - m_rep)
        pv = jnp.dot(p.astype(BF16), v_ext, preferred_element_type=F32)
        acc_sc[rows, :] = jnp.tile(alpha, (1, 2)) * acc_sc[rows, :] + pv
        m_sc[rows, :] = m_new


def _sweep_keys(q_groups, keys, values, off_biases, diag_biases, m_sc, acc_sc, s_sc, *, q0, tq, tk):
    m_sc[...] = jnp.full(m_sc.shape, NEG, F32)
    acc_sc[...] = jnp.zeros(acc_sc.shape, F32)
    n_off = q0 // tk
    kd = pl.multiple_of(q0, tq)

    def scores(start, size):
        return [_nt_dot(q, k) for q, k in zip(q_groups(), keys(start, size))]

    if s_sc is None:
        def body(j, carry):
            ks = pl.multiple_of(j * tk, tk)
            _softmax_pv(scores(ks, tk), values(ks, tk), off_biases(ks), m_sc, acc_sc)
            return carry

        lax.fori_loop(0, n_off, body, 0)
        _softmax_pv(scores(kd, tq), values(kd, tq), diag_biases, m_sc, acc_sc)
        return

    assert tq == tk
    bounds = []
    r0 = 0
    for q in q_groups():
        bounds.append(slice(r0, r0 + q.shape[0]))
        r0 += q.shape[0]

    def stash(new_scores):
        for rows, s in zip(bounds, new_scores):
            s_sc[rows, :] = s

    def staged():
        return [s_sc[rows, :] for rows in bounds]

    stash(scores(0, tk))

    def body(j, carry):
        ks = pl.multiple_of(j * tk, tk)
        nxt = scores(pl.multiple_of(ks + tk, tk), tk)
        _softmax_pv(staged(), values(ks, tk), off_biases(ks), m_sc, acc_sc)
        stash(nxt)
        return carry

    lax.fori_loop(0, n_off, body, 0)
    _softmax_pv(staged(), values(kd, tq), diag_biases, m_sc, acc_sc)


def _chunk_mask_bias(q0, tq, slope):
    qpos = q0 + lax.broadcasted_iota(jnp.int32, (tq, tq), 0)
    kpos = q0 + lax.broadcasted_iota(jnp.int32, (tq, tq), 1)
    visible = (kpos >> 6) <= (qpos >> 6)
    if slope is None:
        return jnp.where(visible, 0.0, NEG)
    rel = (qpos - q0 - jnp.abs(qpos - kpos)).astype(F32)
    return jnp.where(visible, slope * rel, NEG)


def _diff_attn_head(qi, q_ref, k_ref, v_ref, slopes, lam_ref, g_ref, o_ref,
                    qs_sc, m_sc, acc_sc, s_sc, *, tq, tk, q_off, lam_init):
    heads = len(slopes)
    groups = 2 * heads
    q0 = q_off + qi * tq
    for h in range(heads):
        q = q_ref[:, h * LANE:(h + 1) * LANE]
        lane = lax.broadcasted_iota(jnp.int32, q.shape, 1)
        zero = jnp.zeros_like(q)
        qs_sc[2 * h * tq:(2 * h + 1) * tq, :] = jnp.where(lane < A_DH, q, zero)
        qs_sc[(2 * h + 1) * tq:(2 * h + 2) * tq, :] = jnp.where(lane >= A_DH, q, zero)

    def head_cols(ref, ks, size):
        return [ref[pl.ds(ks, size), (g // 2) * LANE:(g // 2 + 1) * LANE] for g in range(groups)]

    def q_groups():
        return [qs_sc[g * tq:(g + 1) * tq, :] for g in range(groups)]

    def off_biases(ks):
        rel = (ks + lax.broadcasted_iota(jnp.int32, (1, tk), 1) - q0).astype(F32)
        return [slopes[g // 2] * rel for g in range(groups)]

    diag = [_chunk_mask_bias(q0, tq, slope) for slope in slopes]
    _sweep_keys(q_groups, functools.partial(head_cols, k_ref), functools.partial(head_cols, v_ref),
                off_biases, [diag[g // 2] for g in range(groups)], m_sc, acc_sc, s_sc,
                q0=q0, tq=tq, tk=tk)

    lp = lam_ref[...]
    lam = (jnp.exp(jnp.sum(lp[0:1, :] * lp[1:2, :], axis=1, keepdims=True))
           - jnp.exp(jnp.sum(lp[2:3, :] * lp[3:4, :], axis=1, keepdims=True)) + lam_init)
    for h in range(heads):
        acc1 = acc_sc[2 * h * tq:(2 * h + 1) * tq, :]
        acc2 = acc_sc[(2 * h + 1) * tq:(2 * h + 2) * tq, :]
        o = acc1[:, 0:LANE] / acc1[:, LANE:2 * LANE] - lam * (acc2[:, 0:LANE] / acc2[:, LANE:2 * LANE])
        o = _rms_rows(o, g_ref[...]) * (1.0 - lam_init)
        o_ref[:, h * LANE:(h + 1) * LANE] = o.astype(o_ref.dtype)


def _into_existing(kernel, in_specs, args, prev):
    if prev is None:
        return kernel, in_specs, args, {}

    def with_prev(prev_ref, *refs):
        del prev_ref
        kernel(*refs)

    return with_prev, [pl.BlockSpec(memory_space=pl.ANY)] + in_specs, [prev] + args, {0: 0}


def _head_cols(ref, h, width=LANE):
    return ref.at[:, h * width:(h + 1) * width]


def _diff_attn_kernel(q_ref, k_ref, v_ref, slope_ref, lam_ref, g_ref, o_ref,
                      qs_sc, m_sc, acc_sc, s_sc, **static):
    slopes = [slope_ref[h][:, 0:1] for h in range(slope_ref.shape[0])]
    _diff_attn_head(pl.program_id(2), q_ref, k_ref, v_ref, slopes, lam_ref, g_ref,
                    o_ref, qs_sc, m_sc, acc_sc, s_sc, **static)


def _diff_attn_heads_kernel(q_ref, kv_ref, slope_ref, lam_ref, g_ref, o_ref,
                            qs_sc, m_sc, acc_sc, **static):
    for h in range(A_HEADS):
        _diff_attn_head(pl.program_id(1), _head_cols(q_ref, h), _head_cols(kv_ref, h),
                        _head_cols(kv_ref, A_HEADS + h), [slope_ref[h][:, 0:1]], lam_ref, g_ref,
                        _head_cols(o_ref, h), qs_sc, m_sc, acc_sc, None, **static)


def _mla_attn_head(qi, q_ref, kn_ref, kr_ref, v_ref, o_ref, m_sc, acc_sc, s_sc, *, tq, tk, q_off):
    q0 = q_off + qi * tq
    groups = _row_groups(tq)
    gr = tq // groups

    def q_groups():
        return [q_ref[g * gr:(g + 1) * gr, :] for g in range(groups)]

    def keys(ks, size):
        k = jnp.concatenate([kn_ref[pl.ds(ks, size), :], kr_ref[pl.ds(ks, size), :]], axis=1)
        return [k] * groups

    def values(ks, size):
        return [v_ref[pl.ds(ks, size), :]] * groups

    mask = _chunk_mask_bias(q0, tq, None)
    diag_biases = [mask[g * gr:(g + 1) * gr, :] for g in range(groups)]
    _sweep_keys(q_groups, keys, values, lambda ks: [None] * groups, diag_biases, m_sc, acc_sc,
                s_sc, q0=q0, tq=tq, tk=tk)
    acc = acc_sc[...]
    o_ref[...] = (acc[:, 0:LANE] / acc[:, LANE:2 * LANE]).astype(o_ref.dtype)


def _mla_attn_kernel(q_ref, kn_ref, kr_ref, v_ref, o_ref, m_sc, acc_sc, s_sc, **static):
    _mla_attn_head(pl.program_id(2), q_ref, kn_ref, kr_ref, v_ref, o_ref, m_sc, acc_sc, s_sc, **static)


def _mla_attn_heads_kernel(q_ref, kv_ref, kr_ref, o_ref, m_sc, acc_sc, **static):
    for h in range(B_HEADS):
        _mla_attn_head(pl.program_id(1), _head_cols(q_ref, h, 2 * LANE), _head_cols(kv_ref, h), kr_ref,
                       _head_cols(kv_ref, B_HEADS + h), _head_cols(o_ref, h), m_sc, acc_sc, None,
                       **static)


def _attn_tiles(lq, q_off, tq_max):
    sizes = tuple(t for t in (1024, 512, 256, 128, 64) if t <= tq_max)
    tq = _pick(lq, sizes)
    tk = tq if q_off == 0 else _pick(q_off, (512, 256, 128, 64))
    return tq, tk


def _diff_attn(q, kv, slopes, lam_p, subln, *, nb, lq, lk, q_off, q_row0, lam_init, out_rows, prev=None):
    tq, tk = _attn_tiles(lq, q_off, DIFF_ATTN_TQ)
    nq = lq // tq
    qb0 = q_row0 // tq
    static = dict(tq=tq, tk=tk, q_off=q_off, lam_init=lam_init)
    width = A_HEADS * LANE
    out_shape = jax.ShapeDtypeStruct((out_rows, width), BF16)
    scratch = [pltpu.VMEM((2 * tq, LANE), BF16), pltpu.VMEM((2 * tq, LANE), F32),
               pltpu.VMEM((2 * tq, 2 * LANE), F32)]
    if tq != tk:
        kern, in_specs, args, aliases = _into_existing(
            functools.partial(_diff_attn_heads_kernel, **static),
            [pl.BlockSpec((tq, width), lambda b, i: (qb0 + b * nq + i, 0)),
             pl.BlockSpec((lk, 2 * width), lambda b, i: (b, 0)),
             pl.BlockSpec((A_HEADS, 1, LANE), lambda b, i: (0, 0, 0)),
             pl.BlockSpec((4, A_DH), lambda b, i: (0, 0)),
             pl.BlockSpec((1, LANE), lambda b, i: (0, 0))],
            [q, kv, slopes, lam_p, subln], prev)
        return pl.pallas_call(
            kern, grid=(nb, nq), in_specs=in_specs,
            out_specs=pl.BlockSpec((tq, width), lambda b, i: (qb0 + b * nq + i, 0)),
            out_shape=out_shape, scratch_shapes=scratch, input_output_aliases=aliases,
            compiler_params=_cparams(("parallel", "arbitrary")),
        )(*args)
    hp = DIFF_ATTN_HEADS
    hw = hp * LANE
    ng = A_HEADS // hp
    kern, in_specs, args, aliases = _into_existing(
        functools.partial(_diff_attn_kernel, **static),
        [pl.BlockSpec((tq, hw), lambda b, h, i: (qb0 + b * nq + i, h)),
         pl.BlockSpec((lk, hw), lambda b, h, i: (b, h)),
         pl.BlockSpec((lk, hw), lambda b, h, i: (b, ng + h)),
         pl.BlockSpec((hp, 1, LANE), lambda b, h, i: (h, 0, 0)),
         pl.BlockSpec((4, A_DH), lambda b, h, i: (0, 0)),
         pl.BlockSpec((1, LANE), lambda b, h, i: (0, 0))],
        [q, kv, kv, slopes, lam_p, subln], prev)
    rows = 2 * hp * tq
    return pl.pallas_call(
        kern, grid=(nb, ng, nq), in_specs=in_specs,
        out_specs=pl.BlockSpec((tq, hw), lambda b, h, i: (qb0 + b * nq + i, h)),
        out_shape=out_shape,
        scratch_shapes=[pltpu.VMEM((rows, LANE), BF16), pltpu.VMEM((rows, LANE), F32),
                        pltpu.VMEM((rows, 2 * LANE), F32), pltpu.VMEM((rows, tk), F32)],
        input_output_aliases=aliases,
        compiler_params=_cparams(("parallel", "parallel", "arbitrary")),
    )(*args)


def _mla_attn(q, kv, kr, *, nb, lq, lk, q_off, q_row0, kv_row0, out_rows, prev=None):
    tq, tk = _attn_tiles(lq, q_off, MLA_ATTN_TQ)
    nq = lq // tq
    qb0 = q_row0 // tq
    kb0 = kv_row0 // lk
    static = dict(tq=tq, tk=tk, q_off=q_off)
    width = B_HEADS * LANE
    out_shape = jax.ShapeDtypeStruct((out_rows, width), BF16)
    scratch = [pltpu.VMEM((tq, LANE), F32), pltpu.VMEM((tq, 2 * LANE), F32)]
    if tq != tk:
        kern, in_specs, args, aliases = _into_existing(
            functools.partial(_mla_attn_heads_kernel, **static),
            [pl.BlockSpec((tq, 2 * width), lambda b, i: (qb0 + b * nq + i, 0)),
             pl.BlockSpec((lk, 2 * width), lambda b, i: (kb0 + b, 0)),
             pl.BlockSpec((lk, LANE), lambda b, i: (kb0 + b, 0))],
            [q, kv, kr], prev)
        return pl.pallas_call(
            kern, grid=(nb, nq), in_specs=in_specs,
            out_specs=pl.BlockSpec((tq, width), lambda b, i: (qb0 + b * nq + i, 0)),
            out_shape=out_shape, scratch_shapes=scratch, input_output_aliases=aliases,
            compiler_params=_cparams(("parallel", "arbitrary")),
        )(*args)
    kern, in_specs, args, aliases = _into_existing(
        functools.partial(_mla_attn_kernel, **static),
        [pl.BlockSpec((tq, 2 * LANE), lambda b, h, i: (qb0 + b * nq + i, h)),
         pl.BlockSpec((lk, LANE), lambda b, h, i: (kb0 + b, h)),
         pl.BlockSpec((lk, LANE), lambda b, h, i: (kb0 + b, 0)),
         pl.BlockSpec((lk, LANE), lambda b, h, i: (kb0 + b, B_HEADS + h))],
        [q, kv, kr, kv], prev)
    return pl.pallas_call(
        kern, grid=(nb, B_HEADS, nq), in_specs=in_specs,
        out_specs=pl.BlockSpec((tq, LANE), lambda b, h, i: (qb0 + b * nq + i, h)),
        out_shape=out_shape, scratch_shapes=scratch + [pltpu.VMEM((tq, tk), F32)],
        input_output_aliases=aliases,
        compiler_params=_cparams(("parallel", "parallel", "arbitrary")),
    )(*args)


def _spatial_gate_kernel(u_ref, v_ref, w_ref, b_ref, o_ref, *, n_chunks):
    gw = u_ref.shape[1] // C_GROUPS
    row = lax.broadcasted_iota(jnp.int32, (C_BLOCK, C_BLOCK), 0)
    col = lax.broadcasted_iota(jnp.int32, (C_BLOCK, C_BLOCK), 1)
    for g in range(C_GROUPS):
        w = jnp.where(row >= col, w_ref[g], jnp.zeros((C_BLOCK, C_BLOCK), BF16))
        bias = b_ref[g]
        for c in range(n_chunks):
            rows = slice(c * C_BLOCK, (c + 1) * C_BLOCK)
            cols = slice(g * gw, (g + 1) * gw)
            vv = v_ref[rows, cols].astype(BF16)
            sg = jnp.dot(w, vv, preferred_element_type=F32) + bias
            o_ref[rows, cols] = (u_ref[rows, cols].astype(F32) * sg).astype(o_ref.dtype)


def _spatial_gate(u, v, w_sel, b_sel, n_prompt_rows):
    m, width = u.shape
    tm = _pick(m, (1024, 512, 256, 128))
    tm = math.gcd(tm, n_prompt_rows)
    npt = n_prompt_rows // tm
    sel = lambda i: jnp.where(i >= npt, 1, 0)
    kern = functools.partial(_spatial_gate_kernel, n_chunks=tm // C_BLOCK)
    return pl.pallas_call(
        kern, grid=(m // tm,),
        in_specs=[
            pl.BlockSpec((tm, width), lambda i: (i, 0)),
            pl.BlockSpec((tm, width), lambda i: (i, 0)),
            pl.BlockSpec((None, C_GROUPS, C_BLOCK, C_BLOCK), lambda i: (sel(i), 0, 0, 0)),
            pl.BlockSpec((None, C_GROUPS, C_BLOCK, 1), lambda i: (sel(i), 0, 0, 0)),
        ],
        out_specs=pl.BlockSpec((tm, width), lambda i: (i, 0)),
        out_shape=jax.ShapeDtypeStruct((m, width), BF16),
        compiler_params=_cparams(("parallel",)),
    )(u, v, w_sel, b_sel)


def _pool_kernel(x_ref, prev_ref, hist_ref, w_ref, sc_ref, o_ref, xp_sc, *, tm, past):
    i = pl.program_id(1)
    xp_sc[0:D_HALO, :] = jnp.where(i == 0, hist_ref[...], prev_ref[...])
    xp_sc[D_HALO:D_HALO + tm, :] = x_ref[...]
    gw = x_ref.shape[1] // len(D_WINDOWS)
    pos = past + i * tm + lax.broadcasted_iota(jnp.int32, (tm, 1), 0)
    for gi, win in enumerate(D_WINDOWS):
        cols = slice(gi * gw, (gi + 1) * gw)
        tot = xp_sc[D_HALO:D_HALO + tm, cols]
        for j in range(1, win):
            tot = tot + xp_sc[D_HALO - j:D_HALO - j + tm, cols]
        cnt = jnp.minimum(pos + 1, win).astype(F32)
        pooled = tot / cnt - xp_sc[D_HALO:D_HALO + tm, cols]
        y = jnp.dot(pooled.astype(BF16), w_ref[gi], preferred_element_type=F32)
        o_ref[:, cols] = (y * sc_ref[:, cols]).astype(o_ref.dtype)


def _pool(d_in, hist, w_grp, scale, *, nb, lq, row0, past, prev=None):
    width = d_in.shape[1]
    tm = _pick(lq, (512, 256, 128, 64))
    nt = lq // tm
    rb0 = row0 // tm
    hb0 = row0 // D_HALO
    per = tm // D_HALO
    kern, in_specs, args, aliases = _into_existing(
        functools.partial(_pool_kernel, tm=tm, past=past),
        [pl.BlockSpec((tm, width), lambda b, i: (rb0 + b * nt + i, 0)),
         pl.BlockSpec((D_HALO, width), lambda b, i: (jnp.maximum(hb0 + (b * nt + i) * per - 1, 0), 0)),
         pl.BlockSpec((None, D_HALO, width), lambda b, i: (b, 0, 0)),
         pl.BlockSpec(w_grp.shape, lambda b, i: (0, 0, 0)),
         pl.BlockSpec((1, width), lambda b, i: (0, 0))],
        [d_in, d_in, hist, w_grp, scale], prev)
    return pl.pallas_call(
        kern, grid=(nb, nt), in_specs=in_specs,
        out_specs=pl.BlockSpec((tm, width), lambda b, i: (rb0 + b * nt + i, 0)),
        out_shape=jax.ShapeDtypeStruct((d_in.shape[0], width), BF16),
        scratch_shapes=[pltpu.VMEM((D_HALO + tm, width), F32)], input_output_aliases=aliases,
        compiler_params=_cparams(("parallel", "arbitrary")),
    )(*args)


def _merge_kernel(x_ref, wg0, wg1, wg2, wg3, b0, b1, b2, b3, wb_ref, o_ref):
    x = x_ref[...]
    y = None
    for n, (wg, br) in enumerate(((wg0, b0), (wg1, b1), (wg2, b2), (wg3, b3))):
        gate = jax.nn.sigmoid(jnp.dot(x, wg[...], preferred_element_type=F32))
        t = gate * jnp.dot(br[...], wb_ref[n], preferred_element_type=F32)
        y = t if y is None else y + t
    o_ref[...] = y.astype(o_ref.dtype)


def _merge(xb, w_gate, branches, w_branch):
    m, d = xb.shape
    bw = branches[0].shape[1]
    tm = _pick(m, (512, 256, 128))
    tn = 512
    nj = d // tn
    in_specs = [pl.BlockSpec((tm, d), lambda i, j: (i, 0))]
    in_specs += [pl.BlockSpec((d, tn), lambda i, j, n=n: (0, n * nj + j)) for n in range(N_BRANCH)]
    in_specs += [pl.BlockSpec((tm, bw), lambda i, j: (i, 0)) for _ in range(N_BRANCH)]
    in_specs += [pl.BlockSpec((N_BRANCH, bw, tn), lambda i, j: (0, 0, j))]
    return pl.pallas_call(
        _merge_kernel, grid=(m // tm, nj), in_specs=in_specs,
        out_specs=pl.BlockSpec((tm, tn), lambda i, j: (i, j)),
        out_shape=jax.ShapeDtypeStruct((m, d), BF16),
        compiler_params=_cparams(("parallel", "arbitrary")),
    )(xb, w_gate, w_gate, w_gate, w_gate, *branches, w_branch)


def _out_ln_kernel(y_ref, x_ref, w_ref, g_ref, b_ref, o_ref, ob_ref, *, alpha):
    m = jnp.dot(y_ref[...], w_ref[...], preferred_element_type=F32)
    z = _ln_rows(alpha * x_ref[...] + m, g_ref[...], b_ref[...])
    o_ref[...] = z
    ob_ref[...] = z.astype(ob_ref.dtype)


def _out_ln(y, x, w_out, g, b, alpha):
    m, d = x.shape
    tm = _pick(m, (256, 128))
    row = lambda i: (i, 0)
    fixed = lambda i: (0, 0)
    return pl.pallas_call(
        functools.partial(_out_ln_kernel, alpha=alpha), grid=(m // tm,),
        in_specs=[pl.BlockSpec((tm, d), row), pl.BlockSpec((tm, d), row), pl.BlockSpec((d, d), fixed),
                  pl.BlockSpec((1, d), fixed), pl.BlockSpec((1, d), fixed)],
        out_specs=[pl.BlockSpec((tm, d), row), pl.BlockSpec((tm, d), row)],
        out_shape=[jax.ShapeDtypeStruct((m, d), F32), jax.ShapeDtypeStruct((m, d), BF16)],
        compiler_params=_cparams(("parallel",)),
    )(y, x, w_out, g, b)


def _ffn_kernel(xb_ref, x_ref, wg_ref, wu_ref, wd_ref, g_ref, b_ref, o_ref, ob_ref, acc_sc, *, alpha):
    f = pl.program_id(1)

    @pl.when(f == 0)
    def _():
        acc_sc[...] = jnp.zeros(acc_sc.shape, F32)

    xb = xb_ref[...]
    hg = jnp.dot(xb, wg_ref[...], preferred_element_type=F32)
    hu = jnp.dot(xb, wu_ref[...], preferred_element_type=F32)
    h = hg * jax.nn.sigmoid(hg) * hu
    acc_sc[...] += jnp.dot(h.astype(BF16), wd_ref[...], preferred_element_type=F32)

    @pl.when(f == pl.num_programs(1) - 1)
    def _():
        z = _ln_rows(alpha * x_ref[...] + acc_sc[...], g_ref[...], b_ref[...])
        o_ref[...] = z
        ob_ref[...] = z.astype(ob_ref.dtype)


def _ffn(xb, x, wg, wu, wd, g, b, alpha):
    m, d = x.shape
    ff = wg.shape[1]
    tm = _pick(m, (512, 256, 128))
    tf = _pick(ff, (512, 256, 128))
    row = lambda i, f: (i, 0)
    fixed = lambda i, f: (0, 0)
    return pl.pallas_call(
        functools.partial(_ffn_kernel, alpha=alpha), grid=(m // tm, ff // tf),
        in_specs=[pl.BlockSpec((tm, d), row), pl.BlockSpec((tm, d), row),
                  pl.BlockSpec((d, tf), lambda i, f: (0, f)), pl.BlockSpec((d, tf), lambda i, f: (0, f)),
                  pl.BlockSpec((tf, d), lambda i, f: (f, 0)),
                  pl.BlockSpec((1, d), fixed), pl.BlockSpec((1, d), fixed)],
        out_specs=[pl.BlockSpec((tm, d), row), pl.BlockSpec((tm, d), row)],
        out_shape=[jax.ShapeDtypeStruct((m, d), F32), jax.ShapeDtypeStruct((m, d), BF16)],
        scratch_shapes=[pltpu.VMEM((tm, d), F32)],
        compiler_params=_cparams(("parallel", "arbitrary")),
    )(xb, x, wg, wu, wd, g, b)


def _router_kernel(x_ref, w_ref, o_ref, sel_ref, *, n_experts):
    logits = jnp.dot(x_ref[...], w_ref[...], preferred_element_type=F32,
                     precision=lax.Precision.HIGHEST)
    lane = lax.broadcasted_iota(jnp.int32, logits.shape, 1).astype(F32)
    lg = jnp.where(lane < n_experts, logits, NEG)
    m1 = jnp.max(lg, axis=1, keepdims=True)
    i1 = jnp.min(jnp.where(lg == m1, lane, float(LANE)), axis=1, keepdims=True)
    lg2 = jnp.where(lane == i1, NEG, lg)
    m2 = jnp.max(lg2, axis=1, keepdims=True)
    i2 = jnp.min(jnp.where(lg2 == m2, lane, float(LANE)), axis=1, keepdims=True)
    ex = jnp.exp(m2 - m1)
    p1 = 1.0 / (1.0 + ex)
    p2 = ex / (1.0 + ex)
    o_ref[...] = (jnp.where(lane == 0.0, i1, 0.0) + jnp.where(lane == 1.0, i2, 0.0)
                  + jnp.where(lane == 2.0, p1, 0.0) + jnp.where(lane == 3.0, p2, 0.0))
    sel_ref[...] = jnp.where((lane == i1) | (lane == i2), 1.0, 0.0).astype(sel_ref.dtype)


def _router(x, w_router):
    m, d = x.shape
    ne = w_router.shape[1]
    w = jnp.pad(w_router, ((0, 0), (0, LANE - ne)))
    tm = _pick(m, (512, 256, 128))
    row = lambda i: (i, 0)
    return pl.pallas_call(
        functools.partial(_router_kernel, n_experts=ne), grid=(m // tm,),
        in_specs=[pl.BlockSpec((tm, d), row), pl.BlockSpec((d, LANE), lambda i: (0, 0))],
        out_specs=[pl.BlockSpec((tm, LANE), row), pl.BlockSpec((tm, LANE), row)],
        out_shape=[jax.ShapeDtypeStruct((m, LANE), F32), jax.ShapeDtypeStruct((m, LANE), BF16)],
        compiler_params=_cparams(("parallel",)),
    )(x, w)


def _slot_kernel(sel_ref, rec_ref, pos_ref, meta_ref, cnt_sc, off_sc, *, tile, n_experts):
    phase = pl.program_id(0)
    i = pl.program_id(1)
    sel = sel_ref[...]
    tm = sel.shape[0]
    lane = lax.broadcasted_iota(jnp.int32, (1, LANE), 1)

    @pl.when((phase == 0) & (i == 0))
    def _():
        cnt_sc[...] = jnp.zeros(cnt_sc.shape, F32)

    @pl.when(phase == 0)
    def _():
        cnt_sc[...] += jnp.sum(sel.astype(F32), axis=0, keepdims=True)

    @pl.when((phase == 1) & (i == 0))
    def _():
        counts = cnt_sc[...]
        padded = jnp.ceil(counts / tile) * tile
        starts = jnp.zeros((1, LANE), F32)
        for e in range(1, n_experts):
            before = jnp.sum(jnp.where(lane < e, padded, 0.0), axis=1, keepdims=True)
            starts = jnp.where(lane == e, before, starts)
        off_sc[...] = starts
        row = lax.broadcasted_iota(jnp.int32, meta_ref.shape, 0)
        meta_ref[...] = jnp.where(row == 0, counts, jnp.where(row == 1, starts, 0.0))
        cnt_sc[...] = jnp.zeros(cnt_sc.shape, F32)

    @pl.when(phase == 1)
    def _():
        r = lax.broadcasted_iota(jnp.int32, (tm, tm), 0)
        c = lax.broadcasted_iota(jnp.int32, (tm, tm), 1)
        tri = jnp.where(r >= c, 1.0, 0.0).astype(BF16)
        csum = jnp.dot(tri, sel, preferred_element_type=F32)
        slot = off_sc[...] + cnt_sc[...] + csum - sel.astype(F32)
        rec = rec_ref[...]
        lanef = lane.astype(F32)
        pos1 = jnp.sum(jnp.where(lanef == rec[:, 0:1], slot, 0.0), axis=1, keepdims=True)
        pos2 = jnp.sum(jnp.where(lanef == rec[:, 1:2], slot, 0.0), axis=1, keepdims=True)
        pos_ref[...] = jnp.where(lane == 0, pos1, 0.0) + jnp.where(lane == 1, pos2, 0.0)
        cnt_sc[...] += csum[tm - 1:tm, :]


def _slots(sel, rec, tile, n_experts):
    m = sel.shape[0]
    tm = _pick(m, (512, 256, 128))
    row = lambda p, i: (i, 0)
    return pl.pallas_call(
        functools.partial(_slot_kernel, tile=tile, n_experts=n_experts), grid=(2, m // tm),
        in_specs=[pl.BlockSpec((tm, LANE), row), pl.BlockSpec((tm, LANE), row)],
        out_specs=[pl.BlockSpec((tm, LANE), lambda p, i: (i * p, 0)),
                   pl.BlockSpec((8, LANE), lambda p, i: (0, 0))],
        out_shape=[jax.ShapeDtypeStruct((m, LANE), F32), jax.ShapeDtypeStruct((8, LANE), F32)],
        scratch_shapes=[pltpu.VMEM((1, LANE), F32), pltpu.VMEM((1, LANE), F32)],
        compiler_params=_cparams(("arbitrary", "arbitrary")),
    )(sel, rec)


def _row_copy(src, src_row, dst, dst_row, sem):
    return pltpu.make_async_copy(src.at[pl.ds(src_row, 1)], dst.at[pl.ds(dst_row, 1)], sem)


def _dispatch_kernel(pos1_ref, pos2_ref, pad_row_ref, pad_len_ref, x_ref, xs_hbm, zero_sc, sem, *, tm):
    step = pl.program_id(0)
    base = step * tm

    @pl.when(step == 0)
    def _():
        zero_sc[...] = jnp.zeros(zero_sc.shape, zero_sc.dtype)
        for e in range(pad_row_ref.shape[0]):
            def fill(r, carry, e=e):
                _row_copy(zero_sc, 0, xs_hbm, pad_row_ref[e] + r, sem).start()
                return carry

            def fill_wait(r, carry):
                _row_copy(zero_sc, 0, xs_hbm, 0, sem).wait()
                return carry

            lax.fori_loop(0, pad_len_ref[e], fill, 0)
            lax.fori_loop(0, pad_len_ref[e], fill_wait, 0)

    def issue(r, carry):
        t = base + r
        _row_copy(x_ref, r, xs_hbm, pos1_ref[t], sem).start()
        _row_copy(x_ref, r, xs_hbm, pos2_ref[t], sem).start()
        return carry

    lax.fori_loop(0, tm, issue, 0)

    def drain(r, carry):
        _row_copy(x_ref, 0, xs_hbm, 0, sem).wait()
        _row_copy(x_ref, 0, xs_hbm, 0, sem).wait()
        return carry

    lax.fori_loop(0, tm, drain, 0)


def _dispatch(x, pos1, pos2, pad_row, pad_len, n_rows):
    m, d = x.shape
    tm = _pick(m, (512, 256, 128))
    kern = functools.partial(_dispatch_kernel, tm=tm)
    return pl.pallas_call(
        kern,
        grid_spec=pltpu.PrefetchScalarGridSpec(
            num_scalar_prefetch=4, grid=(m // tm,),
            in_specs=[pl.BlockSpec((tm, d), lambda i, *_: (i, 0))],
            out_specs=pl.BlockSpec(memory_space=pl.ANY),
            scratch_shapes=[pltpu.VMEM((8, d), x.dtype), pltpu.SemaphoreType.DMA]),
        out_shape=jax.ShapeDtypeStruct((n_rows, d), x.dtype),
        compiler_params=_cparams(("arbitrary",)),
    )(pos1, pos2, pad_row, pad_len, x)


def _grouped_ffn_kernel(tile_e_ref, n_used_ref, x_ref, wg_ref, wu_ref, wd_ref, o_ref, acc_sc):
    del tile_e_ref
    i = pl.program_id(0)
    f = pl.program_id(1)
    used = i < n_used_ref[0]

    @pl.when(f == 0)
    def _():
        acc_sc[...] = jnp.zeros(acc_sc.shape, F32)

    @pl.when(used)
    def _():
        xb = x_ref[...].astype(BF16)
        hg = jnp.dot(xb, wg_ref[...], preferred_element_type=F32)
        hu = jnp.dot(xb, wu_ref[...], preferred_element_type=F32)
        h = hg * jax.nn.sigmoid(hg) * hu
        acc_sc[...] += jnp.dot(h.astype(BF16), wd_ref[...], preferred_element_type=F32)

    @pl.when(f == pl.num_programs(1) - 1)
    def _():
        o_ref[...] = acc_sc[...]


def _grouped_ffn(xs, tile_e, n_used, wg, wu, wd, tile):
    p, d = xs.shape
    ff = wg.shape[2]
    tf = _pick(ff, (1024, 512, 256, 128))
    row = lambda i, f, te, nu: (i, 0)
    return pl.pallas_call(
        _grouped_ffn_kernel,
        grid_spec=pltpu.PrefetchScalarGridSpec(
            num_scalar_prefetch=2, grid=(p // tile, ff // tf),
            in_specs=[pl.BlockSpec((tile, d), lambda i, f, te, nu: (jnp.minimum(i, nu[0] - 1), 0)),
                      pl.BlockSpec((None, d, tf), lambda i, f, te, nu: (te[i], 0, f)),
                      pl.BlockSpec((None, d, tf), lambda i, f, te, nu: (te[i], 0, f)),
                      pl.BlockSpec((None, tf, d), lambda i, f, te, nu: (te[i], f, 0))],
            out_specs=pl.BlockSpec((tile, d), row),
            scratch_shapes=[pltpu.VMEM((tile, d), F32)]),
        out_shape=jax.ShapeDtypeStruct((p, d), F32),
        compiler_params=_cparams(("parallel", "arbitrary")),
    )(tile_e, n_used, xs, wg, wu, wd)


def _combine_kernel(pos1_ref, pos2_ref, x_ref, rec_ref, g_ref, b_ref, ys_hbm, o_ref, o2_ref,
                    y1_sc, y2_sc, sem, *, alpha, tm, n_prompt_tiles):
    step = pl.program_id(0)
    base = step * tm

    def issue(r, carry):
        t = base + r
        _row_copy(ys_hbm, pos1_ref[t], y1_sc, r, sem).start()
        _row_copy(ys_hbm, pos2_ref[t], y2_sc, r, sem).start()
        return carry

    lax.fori_loop(0, tm, issue, 0)

    def drain(r, carry):
        _row_copy(ys_hbm, 0, y1_sc, 0, sem).wait()
        _row_copy(ys_hbm, 0, y2_sc, 0, sem).wait()
        return carry

    lax.fori_loop(0, tm, drain, 0)
    rec = rec_ref[...]
    f = rec[:, 2:3] * y1_sc[...] + rec[:, 3:4] * y2_sc[...]
    z = _ln_rows(alpha * x_ref[...] + f, g_ref[...], b_ref[...])
    if n_prompt_tiles is None:
        o_ref[...] = z
        o2_ref[...] = z.astype(o2_ref.dtype)
    else:
        @pl.when(step < n_prompt_tiles)
        def _():
            o_ref[...] = z

        @pl.when(step >= n_prompt_tiles)
        def _():
            o2_ref[...] = z


def _combine(x, rec, ys, pos1, pos2, g, b, alpha, final_prompt_rows=None):
    m, d = x.shape
    tm = _pick(m, (256, 128))
    row = lambda i, p1, p2: (i, 0)
    fixed = lambda i, p1, p2: (0, 0)
    if final_prompt_rows is None:
        n_p = None
        out_specs = [pl.BlockSpec((tm, d), row), pl.BlockSpec((tm, d), row)]
        out_shape = [jax.ShapeDtypeStruct((m, d), F32), jax.ShapeDtypeStruct((m, d), BF16)]
    else:
        tm = math.gcd(tm, final_prompt_rows)
        n_p = final_prompt_rows // tm
        out_specs = [pl.BlockSpec((tm, d), lambda i, p1, p2: (jnp.minimum(i, n_p - 1), 0)),
                     pl.BlockSpec((tm, d), lambda i, p1, p2: (jnp.maximum(i - n_p, 0), 0))]
        out_shape = [jax.ShapeDtypeStruct((final_prompt_rows, d), F32),
                     jax.ShapeDtypeStruct((m - final_prompt_rows, d), F32)]
    return pl.pallas_call(
        functools.partial(_combine_kernel, alpha=alpha, tm=tm, n_prompt_tiles=n_p),
        grid_spec=pltpu.PrefetchScalarGridSpec(
            num_scalar_prefetch=2, grid=(m // tm,),
            in_specs=[pl.BlockSpec((tm, d), row), pl.BlockSpec((tm, LANE), row),
                      pl.BlockSpec((1, d), fixed), pl.BlockSpec((1, d), fixed),
                      pl.BlockSpec(memory_space=pl.ANY)],
            out_specs=out_specs,
            scratch_shapes=[pltpu.VMEM((tm, d), F32), pltpu.VMEM((tm, d), F32),
                            pltpu.SemaphoreType.DMA]),
        out_shape=out_shape,
        compiler_params=_cparams(("arbitrary",)),
    )(pos1, pos2, x, rec, g, b, ys)


MOE_TILE = 512


def _routed_moe(x, w_router, wg, wu, wd, g, b, alpha, final_prompt_rows=None):
    m, d = x.shape
    ne = w_router.shape[1]
    rec, sel = _router(x, w_router)
    pos, meta = _slots(sel, rec, MOE_TILE, ne)
    pos1 = pos[:, 0].astype(jnp.int32)
    pos2 = pos[:, 1].astype(jnp.int32)
    n_tiles = (2 * m) // MOE_TILE + ne
    counts, starts = meta[0, :ne], meta[1, :ne]
    ends = starts + jnp.ceil(counts / MOE_TILE) * MOE_TILE
    tile_start = jnp.arange(n_tiles, dtype=F32) * MOE_TILE
    tile_e = jnp.minimum(jnp.sum(ends[None, :] <= tile_start[:, None], axis=1), ne - 1).astype(jnp.int32)
    n_used = (ends[ne - 1:ne] / MOE_TILE).astype(jnp.int32)
    pad_row = (starts + counts).astype(jnp.int32)
    pad_len = (ends - starts - counts).astype(jnp.int32)
    xs = _dispatch(x, pos1, pos2, pad_row, pad_len, n_tiles * MOE_TILE)
    ys = _grouped_ffn(xs, tile_e, n_used, wg, wu, wd, MOE_TILE)
    return _combine(x, rec, ys, pos1, pos2, g, b, alpha, final_prompt_rows)


def _epi_scale(scale):
    return lambda accs, tabs, vecs: [accs[0] * scale]


def _epi_f32_and_bf16(accs, tabs, vecs):
    return [accs[0], accs[0]]


def _epi_rms(accs, tabs, vecs):
    r = _rms_rows(accs[0], vecs[0])
    return [r, r]


def _epi_rope(accs, tabs, vecs):
    return [accs[0] * tabs[0] + accs[1] * tabs[1]]


def _epi_gelu(accs, tabs, vecs):
    return [jax.nn.gelu(accs[0])]


def _epi_gelu_ln(accs, tabs, vecs):
    return [_ln_rows(jax.nn.gelu(accs[0]), vecs[0], vecs[1])]


def _epi_plain(accs, tabs, vecs):
    return [accs[0]]


def _epi_q_rope(scale):
    return lambda accs, tabs, vecs: [(accs[0] * tabs[0] + accs[1] * tabs[1]) * scale]


def _rope_tables(pos):
    half = B_ROPE // 2
    inv = ROPE_THETA ** (-jnp.arange(half, dtype=F32) / half)
    ang = pos.astype(F32)[:, None] * inv[None, :]
    cos, sin = jnp.cos(ang), jnp.sin(ang)
    cos2 = jnp.concatenate([cos, cos], axis=1)
    sin2 = jnp.concatenate([-sin, sin], axis=1)
    return cos2, sin2


def _swap_halves(w):
    half = w.shape[-1] // 2
    return jnp.concatenate([w[..., half:], w[..., :half]], axis=-1)


def kernel(x_prompt, x_sample, cache_a_k, cache_a_v, cache_b_latent, cache_b_krope, state_d_pool, w_in, a_lambda, a_subln, b_q_norm, b_w_uq, b_kv_norm, b_w_ukv, c_ln_g, c_ln_b, c_w_s, c_b_s, d_w_grp, d_scale, w_branch, w_out, ln1_g, ln1_b, ln2_g, ln2_b, f_w_gate, f_w_up, f_w_down, m_router, m_w_gate, m_w_up, m_w_down):
    depth = w_in.shape[0]
    nbp, lp, d_model = x_prompt.shape
    nbs, ls, _ = x_sample.shape
    past = cache_a_k.shape[2]
    lks = past + ls
    tp, ts = nbp * lp, nbs * ls
    aw = A_HEADS * 2 * A_DH
    q_rank = b_q_norm.shape[1]
    kv_rank = b_kv_norm.shape[1]
    c_width = c_ln_g.shape[1]
    d_width = d_scale.shape[1]
    alpha = (2.0 * depth) ** 0.25
    sizes = (aw, aw, aw, q_rank, kv_rank, B_ROPE, c_width, c_width, d_width) + (d_model,) * N_BRANCH
    offs = [0]
    for s in sizes:
        offs.append(offs[-1] + s)

    x = jnp.concatenate([x_prompt.reshape(tp, d_model), x_sample.reshape(ts, d_model)], axis=0)
    xb = x.astype(BF16)

    pos = jnp.concatenate([jnp.tile(jnp.arange(lp, dtype=jnp.int32), nbp),
                           jnp.tile(past + jnp.arange(ls, dtype=jnp.int32), nbs)])
    cos2, sin2 = _rope_tables(pos)
    rows = pos.shape[0]
    ones = jnp.ones((rows, B_NOPE), F32)
    zeros_n = jnp.zeros((rows, B_NOPE), F32)
    zeros_r = jnp.zeros((rows, 2 * LANE - B_NOPE - B_ROPE), F32)
    q_cos = jnp.tile(jnp.concatenate([ones, cos2, zeros_r], axis=1), (1, 2))
    q_sin = jnp.tile(jnp.concatenate([zeros_n, sin2, zeros_r], axis=1), (1, 2))

    slopes = jnp.exp2(-8.0 * jnp.arange(1, A_HEADS + 1, dtype=F32) / A_HEADS) * LOG2E
    slopes = jnp.broadcast_to(slopes[:, None, None], (A_HEADS, 1, LANE))

    states_p, states_s = [], []
    kv_state = lat_state = None
    for l in range(depth):
        wl = w_in[l]
        seg = lambda k: wl[:, offs[k]:offs[k + 1]].astype(BF16)
        lam_init = 0.8 - 0.6 * math.exp(-0.3 * l)

        (q_a,) = _fused_mm(_epi_scale(A_DH ** -0.5 * LOG2E), xb, [seg(0)], [BF16], tn=1024)
        w_kv = jnp.concatenate([seg(1), seg(2)], axis=1)
        kv_a_b, kv_state = _state_proj(xb, w_kv, None, tn=aw, layer=l, depth=depth, n_prompt_rows=tp,
                                       prev=kv_state)
        subln = a_subln[l].reshape(1, 2 * A_DH)
        o_a = _diff_attn(q_a, kv_a_b, slopes, a_lambda[l], subln, nb=nbp, lq=lp, lk=lp, q_off=0,
                         q_row0=0, lam_init=lam_init, out_rows=rows)
        cache_kv = jnp.concatenate([cache_a_k[l].reshape(nbs, past, aw),
                                    cache_a_v[l].reshape(nbs, past, aw)], axis=2).astype(BF16)
        kv_s = jnp.concatenate([cache_kv, kv_a_b[tp:].reshape(nbs, ls, 2 * aw)], axis=1)
        o_a = _diff_attn(q_a, kv_s.reshape(nbs * lks, 2 * aw), slopes, a_lambda[l], subln, nb=nbs,
                         lq=ls, lk=lks, q_off=past, q_row0=tp, lam_init=lam_init, out_rows=rows, prev=o_a)

        (cq,) = _fused_mm(_epi_rms, xb, [seg(3)], [BF16], tn=q_rank,
                          vecs=[b_q_norm[l].reshape(1, q_rank)])
        lat_b, lat_state = _state_proj(xb, seg(4), b_kv_norm[l].reshape(1, kv_rank), tn=kv_rank, layer=l,
                                       depth=depth, n_prompt_rows=tp, prev=lat_state)
        w_kr = wl[:, offs[5]:offs[6]]
        (kr,) = _fused_mm(_epi_rope, xb, [w_kr.astype(BF16), _swap_halves(w_kr).astype(BF16)], [F32],
                          tn=B_ROPE, tabs=[cos2, sin2])
        w_uq = b_w_uq[l].reshape(q_rank, B_HEADS, B_NOPE + B_ROPE)
        w_uq_n, w_uq_r = w_uq[..., :B_NOPE], w_uq[..., B_NOPE:]
        zpad = jnp.zeros((q_rank, B_HEADS, 2 * LANE - B_NOPE - B_ROPE), F32)
        w_q1 = jnp.concatenate([w_uq_n, w_uq_r, zpad], axis=-1).reshape(q_rank, -1).astype(BF16)
        w_q2 = jnp.concatenate([jnp.zeros_like(w_uq_n), _swap_halves(w_uq_r), zpad],
                               axis=-1).reshape(q_rank, -1).astype(BF16)
        (q_b,) = _fused_mm(_epi_q_rope((B_NOPE + B_ROPE) ** -0.5 * LOG2E), cq, [w_q1, w_q2], [BF16],
                           tn=4 * LANE, tabs=[q_cos, q_sin])
        w_ukv = b_w_ukv[l].reshape(kv_rank, B_HEADS, B_NOPE + B_VDIM)
        w_up = jnp.concatenate([w_ukv[..., :B_NOPE].reshape(kv_rank, -1),
                                w_ukv[..., B_NOPE:].reshape(kv_rank, -1)], axis=1).astype(BF16)
        lat_s = jnp.concatenate([cache_b_latent[l].astype(BF16), lat_b[tp:].reshape(nbs, ls, kv_rank)],
                                axis=1).reshape(nbs * lks, kv_rank)
        lat_cat = jnp.concatenate([lat_b[:tp], lat_s], axis=0)
        (kv_b,) = _fused_mm(_epi_plain, lat_cat, [w_up], [BF16], tn=1024)
        kr_s = jnp.concatenate([cache_b_krope[l], kr[tp:].reshape(nbs, ls, B_ROPE)],
                               axis=1).reshape(nbs * lks, B_ROPE)
        kr_cat = jnp.pad(jnp.concatenate([kr[:tp], kr_s], axis=0),
                         ((0, 0), (0, LANE - B_ROPE))).astype(BF16)
        o_b = _mla_attn(q_b, kv_b, kr_cat, nb=nbp, lq=lp, lk=lp, q_off=0, q_row0=0, kv_row0=0,
                        out_rows=rows)
        if tp % lks:
            kv_b_s, kr_b_s, s_row0 = kv_b[tp:], kr_cat[tp:], 0
        else:
            kv_b_s, kr_b_s, s_row0 = kv_b, kr_cat, tp
        o_b = _mla_attn(q_b, kv_b_s, kr_b_s, nb=nbs, lq=ls, lk=lks, q_off=past, q_row0=tp,
                        kv_row0=s_row0, out_rows=rows, prev=o_b)

        (u_c,) = _fused_mm(_epi_gelu, xb, [seg(6)], [BF16], tn=1024)
        (v_c,) = _fused_mm(_epi_gelu_ln, xb, [seg(7)], [F32], tn=c_width,
                           vecs=[c_ln_g[l].reshape(1, c_width), c_ln_b[l].reshape(1, c_width)])
        w_s, b_s = c_w_s[l], c_b_s[l]
        reps = C_BLOCK // ls if ls < C_BLOCK else 1
        w_tl = w_s[:, :C_BLOCK // reps, :C_BLOCK // reps]
        w_smp = jnp.zeros_like(w_s)
        for r in range(reps):
            sl = slice(r * (C_BLOCK // reps), (r + 1) * (C_BLOCK // reps))
            w_smp = w_smp.at[:, sl, sl].set(w_tl)
        b_smp = jnp.tile(b_s[:, :C_BLOCK // reps], (1, reps))
        w_sel = jnp.stack([w_s, w_smp]).astype(BF16)
        b_sel = jnp.stack([b_s, b_smp])[..., None]
        o_c = _spatial_gate(u_c, v_c, w_sel, b_sel, tp)

        (d_in,) = _fused_mm(_epi_plain, xb, [seg(8)], [F32], tn=1024)
        w_grp = d_w_grp[l].astype(BF16)
        scale = d_scale[l].reshape(1, d_width)
        hist_p = jnp.zeros((nbp, D_HALO, d_width), F32)
        hist_s = jnp.pad(state_d_pool[l], ((0, 0), (D_HALO - D_HIST, 0), (0, 0)))
        o_d = _pool(d_in, hist_p, w_grp, scale, nb=nbp, lq=lp, row0=0, past=0)
        o_d = _pool(d_in, hist_s, w_grp, scale, nb=nbs, lq=ls, row0=tp, past=past, prev=o_d)

        w_gate = wl[:, offs[9]:offs[13]].astype(BF16)
        y = _merge(xb, w_gate, [o_a, o_b, o_c, o_d], w_branch[l].astype(BF16))
        x, xb = _out_ln(y, x, w_out[l].astype(BF16), ln1_g[l].reshape(1, -1), ln1_b[l].reshape(1, -1),
                        alpha)

        i = l // 2
        g2, b2 = ln2_g[l].reshape(1, -1), ln2_b[l].reshape(1, -1)
        if l % 2 == 0:
            x, xb = _ffn(xb, x, f_w_gate[i].astype(BF16), f_w_up[i].astype(BF16),
                         f_w_down[i].astype(BF16), g2, b2, alpha)
        else:
            x, xb = _routed_moe(x, m_router[i], m_w_gate[i].astype(BF16), m_w_up[i].astype(BF16),
                                m_w_down[i].astype(BF16), g2, b2, alpha,
                                final_prompt_rows=tp if l == depth - 1 else None)

        states_p.append((kr[:tp].reshape(nbp, lp, B_ROPE),
                         d_in[:tp].reshape(nbp, lp, d_width)[:, lp - D_HIST:]))
        states_s.append((kr[tp:].reshape(nbs, ls, B_ROPE), v_c[tp:].reshape(nbs, ls, c_width),
                         jnp.concatenate([state_d_pool[l], d_in[tp:].reshape(nbs, ls, d_width)],
                                         axis=1)[:, -D_HIST:]))

    if depth % 2 == 0:
        yp, ys = x.reshape(nbp, lp, d_model), xb.reshape(nbs, ls, d_model)
    else:
        yp, ys = x[:tp].reshape(nbp, lp, d_model), x[tp:].reshape(nbs, ls, d_model)
    stack = lambda sts, k: jnp.stack([s[k] for s in sts])
    k_p, k_s, v_p, v_s = kv_state
    lat_p, lat_s = lat_state
    heads_p = (depth, nbp, lp, A_HEADS, 2 * A_DH)
    heads_s = (depth, nbs, ls, A_HEADS, 2 * A_DH)
    return (yp, ys, k_p.reshape(heads_p), v_p.reshape(heads_p), lat_p.reshape(depth, nbp, lp, kv_rank),
            stack(states_p, 0), stack(states_p, 1), k_s.reshape(heads_s), v_s.reshape(heads_s),
            lat_s.reshape(depth, nbs, ls, kv_rank), stack(states_s, 0), stack(states_s, 1),
            stack(states_s, 2))
```

```python
import functools
import math

import jax
import jax.numpy as jnp
from jax import lax
from jax.experimental import pallas as pl
from jax.experimental.pallas import tpu as pltpu

CHUNK = 64
A_HEADS = 8
A_DH = 64
B_HEADS = 8
B_NOPE = 128
B_ROPE = 64
B_VDIM = 128
C_GROUPS = 4
C_BLOCK = 128
D_WINDOWS = (2, 4, 8, 16)
D_HIST = 15
D_HALO = 16
N_BRANCH = 4
ROPE_THETA = 10000.0
LN_EPS = 1e-5
RMS_EPS = 1e-6
LOG2E = 1.4426950408889634
NEG = -1e30
LANE = 128
VMEM_LIMIT = 56 * 1024 * 1024

F32 = jnp.float32
BF16 = jnp.bfloat16


def _pick(n, cands):
    for c in cands:
        if n % c == 0:
            return c
    raise ValueError(f"no tile for {n} in {cands}")


def _cparams(sem):
    return pltpu.CompilerParams(dimension_semantics=sem, vmem_limit_bytes=VMEM_LIMIT)


def _ln_rows(z, g, b):
    mu = jnp.mean(z, axis=-1, keepdims=True)
    zc = z - mu
    var = jnp.mean(zc * zc, axis=-1, keepdims=True)
    return zc * lax.rsqrt(var + LN_EPS) * g + b


def _rms_rows(z, g):
    return z * lax.rsqrt(jnp.mean(z * z, axis=-1, keepdims=True) + RMS_EPS) * g


def _fused_mm_kernel(*refs, fn, nw, ntab, nvec, precision):
    x = refs[0][...]
    w_refs = refs[1:1 + nw]
    tab_refs = refs[1 + nw:1 + nw + ntab]
    vec_refs = refs[1 + nw + ntab:1 + nw + ntab + nvec]
    out_refs = refs[1 + nw + ntab + nvec:]
    accs = [jnp.dot(x, w[...], preferred_element_type=F32, precision=precision) for w in w_refs]
    res = fn(accs, [t[...] for t in tab_refs], [v[...] for v in vec_refs])
    for o, r in zip(out_refs, res):
        o[...] = r.astype(o.dtype)


def _fused_mm(fn, x, ws, out_dtypes, *, tn, tm=None, tabs=(), tabs_follow_cols=False, vecs=(),
              precision=None):
    m, kdim = x.shape
    n = ws[0].shape[1]
    tm = tm or _pick(m, (1024, 512, 256, 128))
    grid = (m // tm, n // tn)
    in_specs = [pl.BlockSpec((tm, kdim), lambda i, j: (i, 0))]
    in_specs += [pl.BlockSpec((kdim, tn), lambda i, j: (0, j)) for _ in ws]
    for t in tabs:
        if tabs_follow_cols:
            in_specs.append(pl.BlockSpec((tm, tn), lambda i, j: (i, j)))
        else:
            in_specs.append(pl.BlockSpec((tm, t.shape[1]), lambda i, j: (i, 0)))
    in_specs += [pl.BlockSpec((1, tn), lambda i, j: (0, j)) for _ in vecs]
    out_specs = [pl.BlockSpec((tm, tn), lambda i, j: (i, j)) for _ in out_dtypes]
    out_shape = [jax.ShapeDtypeStruct((m, n), d) for d in out_dtypes]
    kern = functools.partial(_fused_mm_kernel, fn=fn, nw=len(ws), ntab=len(tabs), nvec=len(vecs),
                             precision=precision)
    return pl.pallas_call(
        kern, grid=grid, in_specs=in_specs, out_specs=out_specs, out_shape=out_shape,
        compiler_params=_cparams(("parallel", "arbitrary")),
    )(x, *ws, *tabs, *vecs)


def _state_proj_kernel(*refs, n_col, n_prompt_tiles, has_gain, has_prev):
    x_ref, w_ref = refs[0], refs[1]
    pos = 2
    g_ref = refs[pos] if has_gain else None
    pos += (1 if has_gain else 0) + (2 * n_col if has_prev else 0)
    joint_ref = refs[pos]
    state_refs = refs[pos + 1:]
    i = pl.program_id(0)
    j = pl.program_id(1)
    acc = jnp.dot(x_ref[...], w_ref[...], preferred_element_type=F32)
    if has_gain:
        acc = _rms_rows(acc, g_ref[...])
    joint_ref[...] = acc.astype(joint_ref.dtype)
    for c in range(n_col):
        @pl.when((j == c) & (i < n_prompt_tiles))
        def _(c=c):
            state_refs[2 * c][...] = acc

        @pl.when((j == c) & (i >= n_prompt_tiles))
        def _(c=c):
            state_refs[2 * c + 1][...] = acc


def _state_proj(x, w, gain, *, tn, layer, depth, n_prompt_rows, prev):
    m, kdim = x.shape
    n = w.shape[1]
    n_col = n // tn
    tm = math.gcd(_pick(m, (512, 256, 128)), n_prompt_rows)
    n_p = n_prompt_rows // tm
    n_s = (m - n_prompt_rows) // tm
    in_specs = [pl.BlockSpec((tm, kdim), lambda i, j: (i, 0)), pl.BlockSpec((kdim, tn), lambda i, j: (0, j))]
    args = [x, w]
    if gain is not None:
        in_specs.append(pl.BlockSpec((1, tn), lambda i, j: (0, j)))
        args.append(gain)
    if prev is None:
        prev = []
        for _ in range(n_col):
            prev += [jnp.zeros((depth * n_p * tm, tn), F32), jnp.zeros((depth * n_s * tm, tn), F32)]
    aliases = {}
    for k, buf in enumerate(prev):
        aliases[len(args)] = 1 + k
        in_specs.append(pl.BlockSpec(memory_space=pl.ANY))
        args.append(buf)
    out_specs = [pl.BlockSpec((tm, tn), lambda i, j: (i, j))]
    out_shape = [jax.ShapeDtypeStruct((m, n), BF16)]
    for _ in range(n_col):
        out_specs.append(pl.BlockSpec((tm, tn), lambda i, j: (layer * n_p + jnp.minimum(i, n_p - 1), 0)))
        out_shape.append(jax.ShapeDtypeStruct((depth * n_p * tm, tn), F32))
        out_specs.append(pl.BlockSpec((tm, tn), lambda i, j: (layer * n_s + jnp.maximum(i - n_p, 0), 0)))
        out_shape.append(jax.ShapeDtypeStruct((depth * n_s * tm, tn), F32))
    kern = functools.partial(_state_proj_kernel, n_col=n_col, n_prompt_tiles=n_p,
                             has_gain=gain is not None, has_prev=True)
    outs = pl.pallas_call(
        kern, grid=(m // tm, n_col), in_specs=in_specs, out_specs=out_specs, out_shape=out_shape,
        input_output_aliases=aliases, compiler_params=_cparams(("arbitrary", "arbitrary")),
    )(*args)
    return outs[0], list(outs[1:])


ATTN_ROW_GROUPS = 2
DIFF_ATTN_TQ = 512
DIFF_ATTN_HEADS = 2
MLA_ATTN_TQ = 1024


def _row_groups(rows):
    g = ATTN_ROW_GROUPS
    while rows % (16 * g):
        g //= 2
    return g


def _nt_dot(a, b):
    return lax.dot_general(a, b, (((1,), (1,)), ((), ())), preferred_element_type=F32)


def _softmax_pv(scores, values, biases, m_sc, acc_sc):
    r0 = 0
    for s, v, bias in zip(scores, values, biases):
        width = v.shape[0]
        v_ext = jnp.concatenate([v, jnp.ones(v.shape, v.dtype)], axis=1)
        rows = slice(r0, r0 + s.shape[0])
        r0 += s.shape[0]
        if bias is not None:
            s = s + bias
        m_prev = m_sc[rows, :]
        m_new = jnp.maximum(m_prev, jnp.max(s, axis=1, keepdims=True))
        alpha = jnp.exp2(m_prev - m_new)
        m_rep = jnp.tile(m_new, (1, width // LANE)) if width >= LANE else m_new[:, :width]
        p = jnp.exp2(s - m_rep)
        pv = jnp.dot(p.astype(BF16), v_ext, preferred_element_type=F32)
        acc_sc[rows, :] = jnp.tile(alpha, (1, 2)) * acc_sc[rows, :] + pv
        m_sc[rows, :] = m_new


def _sweep_keys(q_groups, keys, values, off_biases, diag_biases, m_sc, acc_sc, s_sc, *, q0, tq, tk):
    m_sc[...] = jnp.full(m_sc.shape, NEG, F32)
    acc_sc[...] = jnp.zeros(acc_sc.shape, F32)
    n_off = q0 // tk
    kd = pl.multiple_of(q0, tq)

    def scores(start, size):
        return [_nt_dot(q, k) for q, k in zip(q_groups(), keys(start, size))]

    if s_sc is None:
        def body(j, carry):
            ks = pl.multiple_of(j * tk, tk)
            _softmax_pv(scores(ks, tk), values(ks, tk), off_biases(ks), m_sc, acc_sc)
            return carry

        lax.fori_loop(0, n_off, body, 0)
        _softmax_pv(scores(kd, tq), values(kd, tq), diag_biases, m_sc, acc_sc)
        return

    assert tq == tk
    bounds = []
    r0 = 0
    for q in q_groups():
        bounds.append(slice(r0, r0 + q.shape[0]))
        r0 += q.shape[0]

    def stash(new_scores):
        for rows, s in zip(bounds, new_scores):
            s_sc[rows, :] = s

    def staged():
        return [s_sc[rows, :] for rows in bounds]

    stash(scores(0, tk))

    def body(j, carry):
        ks = pl.multiple_of(j * tk, tk)
        nxt = scores(pl.multiple_of(ks + tk, tk), tk)
        _softmax_pv(staged(), values(ks, tk), off_biases(ks), m_sc, acc_sc)
        stash(nxt)
        return carry

    lax.fori_loop(0, n_off, body, 0)
    _softmax_pv(staged(), values(kd, tq), diag_biases, m_sc, acc_sc)


def _chunk_mask_bias(q0, tq, slope):
    qpos = q0 + lax.broadcasted_iota(jnp.int32, (tq, tq), 0)
    kpos = q0 + lax.broadcasted_iota(jnp.int32, (tq, tq), 1)
    visible = (kpos >> 6) <= (qpos >> 6)
    if slope is None:
        return jnp.where(visible, 0.0, NEG)
    rel = (qpos - q0 - jnp.abs(qpos - kpos)).astype(F32)
    return jnp.where(visible, slope * rel, NEG)


def _diff_attn_head(qi, q_ref, k_ref, v_ref, slopes, lam_ref, g_ref, o_ref,
                    qs_sc, m_sc, acc_sc, s_sc, *, tq, tk, q_off, lam_init):
    heads = len(slopes)
    groups = 2 * heads
    q0 = q_off + qi * tq
    for h in range(heads):
        q = q_ref[:, h * LANE:(h + 1) * LANE]
        lane = lax.broadcasted_iota(jnp.int32, q.shape, 1)
        zero = jnp.zeros_like(q)
        qs_sc[2 * h * tq:(2 * h + 1) * tq, :] = jnp.where(lane < A_DH, q, zero)
        qs_sc[(2 * h + 1) * tq:(2 * h + 2) * tq, :] = jnp.where(lane >= A_DH, q, zero)

    def head_cols(ref, ks, size):
        return [ref[pl.ds(ks, size), (g // 2) * LANE:(g // 2 + 1) * LANE] for g in range(groups)]

    def q_groups():
        return [qs_sc[g * tq:(g + 1) * tq, :] for g in range(groups)]

    def off_biases(ks):
        rel = (ks + lax.broadcasted_iota(jnp.int32, (1, tk), 1) - q0).astype(F32)
        return [slopes[g // 2] * rel for g in range(groups)]

    diag = [_chunk_mask_bias(q0, tq, slope) for slope in slopes]
    _sweep_keys(q_groups, functools.partial(head_cols, k_ref), functools.partial(head_cols, v_ref),
                off_biases, [diag[g // 2] for g in range(groups)], m_sc, acc_sc, s_sc,
                q0=q0, tq=tq, tk=tk)

    lp = lam_ref[...]
    lam = (jnp.exp(jnp.sum(lp[0:1, :] * lp[1:2, :], axis=1, keepdims=True))
           - jnp.exp(jnp.sum(lp[2:3, :] * lp[3:4, :], axis=1, keepdims=True)) + lam_init)
    for h in range(heads):
        acc1 = acc_sc[2 * h * tq:(2 * h + 1) * tq, :]
        acc2 = acc_sc[(2 * h + 1) * tq:(2 * h + 2) * tq, :]
        o = acc1[:, 0:LANE] / acc1[:, LANE:2 * LANE] - lam * (acc2[:, 0:LANE] / acc2[:, LANE:2 * LANE])
        o = _rms_rows(o, g_ref[...]) * (1.0 - lam_init)
        o_ref[:, h * LANE:(h + 1) * LANE] = o.astype(o_ref.dtype)


def _into_existing(kernel, in_specs, args, prev):
    if prev is None:
        return kernel, in_specs, args, {}

    def with_prev(prev_ref, *refs):
        del prev_ref
        kernel(*refs)

    return with_prev, [pl.BlockSpec(memory_space=pl.ANY)] + in_specs, [prev] + args, {0: 0}


def _head_cols(ref, h, width=LANE):
    return ref.at[:, h * width:(h + 1) * width]


def _diff_attn_kernel(q_ref, k_ref, v_ref, slope_ref, lam_ref, g_ref, o_ref,
                      qs_sc, m_sc, acc_sc, s_sc, **static):
    slopes = [slope_ref[h][:, 0:1] for h in range(slope_ref.shape[0])]
    _diff_attn_head(pl.program_id(2), q_ref, k_ref, v_ref, slopes, lam_ref, g_ref,
                    o_ref, qs_sc, m_sc, acc_sc, s_sc, **static)


def _diff_attn_heads_kernel(q_ref, kv_ref, slope_ref, lam_ref, g_ref, o_ref,
                            qs_sc, m_sc, acc_sc, **static):
    for h in range(A_HEADS):
        _diff_attn_head(pl.program_id(1), _head_cols(q_ref, h), _head_cols(kv_ref, h),
                        _head_cols(kv_ref, A_HEADS + h), [slope_ref[h][:, 0:1]], lam_ref, g_ref,
                        _head_cols(o_ref, h), qs_sc, m_sc, acc_sc, None, **static)


def _mla_attn_head(qi, q_ref, kn_ref, kr_ref, v_ref, o_ref, m_sc, acc_sc, s_sc, *, tq, tk, q_off):
    q0 = q_off + qi * tq
    groups = _row_groups(tq)
    gr = tq // groups

    def q_groups():
        return [q_ref[g * gr:(g + 1) * gr, :] for g in range(groups)]

    def keys(ks, size):
        k = jnp.concatenate([kn_ref[pl.ds(ks, size), :], kr_ref[pl.ds(ks, size), :]], axis=1)
        return [k] * groups

    def values(ks, size):
        return [v_ref[pl.ds(ks, size), :]] * groups

    mask = _chunk_mask_bias(q0, tq, None)
    diag_biases = [mask[g * gr:(g + 1) * gr, :] for g in range(groups)]
    _sweep_keys(q_groups, keys, values, lambda ks: [None] * groups, diag_biases, m_sc, acc_sc,
                s_sc, q0=q0, tq=tq, tk=tk)
    acc = acc_sc[...]
    o_ref[...] = (acc[:, 0:LANE] / acc[:, LANE:2 * LANE]).astype(o_ref.dtype)


def _mla_attn_kernel(q_ref, kn_ref, kr_ref, v_ref, o_ref, m_sc, acc_sc, s_sc, **static):
    _mla_attn_head(pl.program_id(2), q_ref, kn_ref, kr_ref, v_ref, o_ref, m_sc, acc_sc, s_sc, **static)


def _mla_attn_heads_kernel(q_ref, kv_ref, kr_ref, o_ref, m_sc, acc_sc, **static):
    for h in range(B_HEADS):
        _mla_attn_head(pl.program_id(1), _head_cols(q_ref, h, 2 * LANE), _head_cols(kv_ref, h), kr_ref,
                       _head_cols(kv_ref, B_HEADS + h), _head_cols(o_ref, h), m_sc, acc_sc, None,
                       **static)


def _attn_tiles(lq, q_off, tq_max):
    sizes = tuple(t for t in (1024, 512, 256, 128, 64) if t <= tq_max)
    tq = _pick(lq, sizes)
    tk = tq if q_off == 0 else _pick(q_off, (512, 256, 128, 64))
    return tq, tk


def _diff_attn(q, kv, slopes, lam_p, subln, *, nb, lq, lk, q_off, q_row0, lam_init, out_rows, prev=None):
    tq, tk = _attn_tiles(lq, q_off, DIFF_ATTN_TQ)
    nq = lq // tq
    qb0 = q_row0 // tq
    static = dict(tq=tq, tk=tk, q_off=q_off, lam_init=lam_init)
    width = A_HEADS * LANE
    out_shape = jax.ShapeDtypeStruct((out_rows, width), BF16)
    scratch = [pltpu.VMEM((2 * tq, LANE), BF16), pltpu.VMEM((2 * tq, LANE), F32),
               pltpu.VMEM((2 * tq, 2 * LANE), F32)]
    if tq != tk:
        kern, in_specs, args, aliases = _into_existing(
            functools.partial(_diff_attn_heads_kernel, **static),
            [pl.BlockSpec((tq, width), lambda b, i: (qb0 + b * nq + i, 0)),
             pl.BlockSpec((lk, 2 * width), lambda b, i: (b, 0)),
             pl.BlockSpec((A_HEADS, 1, LANE), lambda b, i: (0, 0, 0)),
             pl.BlockSpec((4, A_DH), lambda b, i: (0, 0)),
             pl.BlockSpec((1, LANE), lambda b, i: (0, 0))],
            [q, kv, slopes, lam_p, subln], prev)
        return pl.pallas_call(
            kern, grid=(nb, nq), in_specs=in_specs,
            out_specs=pl.BlockSpec((tq, width), lambda b, i: (qb0 + b * nq + i, 0)),
            out_shape=out_shape, scratch_shapes=scratch, input_output_aliases=aliases,
            compiler_params=_cparams(("parallel", "arbitrary")),
        )(*args)
    hp = DIFF_ATTN_HEADS
    hw = hp * LANE
    ng = A_HEADS // hp
    kern, in_specs, args, aliases = _into_existing(
        functools.partial(_diff_attn_kernel, **static),
        [pl.BlockSpec((tq, hw), lambda b, h, i: (qb0 + b * nq + i, h)),
         pl.BlockSpec((lk, hw), lambda b, h, i: (b, h)),
         pl.BlockSpec((lk, hw), lambda b, h, i: (b, ng + h)),
         pl.BlockSpec((hp, 1, LANE), lambda b, h, i: (h, 0, 0)),
         pl.BlockSpec((4, A_DH), lambda b, h, i: (0, 0)),
         pl.BlockSpec((1, LANE), lambda b, h, i: (0, 0))],
        [q, kv, kv, slopes, lam_p, subln], prev)
    rows = 2 * hp * tq
    return pl.pallas_call(
        kern, grid=(nb, ng, nq), in_specs=in_specs,
        out_specs=pl.BlockSpec((tq, hw), lambda b, h, i: (qb0 + b * nq + i, h)),
        out_shape=out_shape,
        scratch_shapes=[pltpu.VMEM((rows, LANE), BF16), pltpu.VMEM((rows, LANE), F32),
                        pltpu.VMEM((rows, 2 * LANE), F32), pltpu.VMEM((rows, tk), F32)],
        input_output_aliases=aliases,
        compiler_params=_cparams(("parallel", "parallel", "arbitrary")),
    )(*args)


def _mla_attn(q, kv, kr, *, nb, lq, lk, q_off, q_row0, kv_row0, out_rows, prev=None):
    tq, tk = _attn_tiles(lq, q_off, MLA_ATTN_TQ)
    nq = lq // tq
    qb0 = q_row0 // tq
    kb0 = kv_row0 // lk
    static = dict(tq=tq, tk=tk, q_off=q_off)
    width = B_HEADS * LANE
    out_shape = jax.ShapeDtypeStruct((out_rows, width), BF16)
    scratch = [pltpu.VMEM((tq, LANE), F32), pltpu.VMEM((tq, 2 * LANE), F32)]
    if tq != tk:
        kern, in_specs, args, aliases = _into_existing(
            functools.partial(_mla_attn_heads_kernel, **static),
            [pl.BlockSpec((tq, 2 * width), lambda b, i: (qb0 + b * nq + i, 0)),
             pl.BlockSpec((lk, 2 * width), lambda b, i: (kb0 + b, 0)),
             pl.BlockSpec((lk, LANE), lambda b, i: (kb0 + b, 0))],
            [q, kv, kr], prev)
        return pl.pallas_call(
            kern, grid=(nb, nq), in_specs=in_specs,
            out_specs=pl.BlockSpec((tq, width), lambda b, i: (qb0 + b * nq + i, 0)),
            out_shape=out_shape, scratch_shapes=scratch, input_output_aliases=aliases,
            compiler_params=_cparams(("parallel", "arbitrary")),
        )(*args)
    kern, in_specs, args, aliases = _into_existing(
        functools.partial(_mla_attn_kernel, **static),
        [pl.BlockSpec((tq, 2 * LANE), lambda b, h, i: (qb0 + b * nq + i, h)),
         pl.BlockSpec((lk, LANE), lambda b, h, i: (kb0 + b, h)),
         pl.BlockSpec((lk, LANE), lambda b, h, i: (kb0 + b, 0)),
         pl.BlockSpec((lk, LANE), lambda b, h, i: (kb0 + b, B_HEADS + h))],
        [q, kv, kr, kv], prev)
    return pl.pallas_call(
        kern, grid=(nb, B_HEADS, nq), in_specs=in_specs,
        out_specs=pl.BlockSpec((tq, LANE), lambda b, h, i: (qb0 + b * nq + i, h)),
        out_shape=out_shape, scratch_shapes=scratch + [pltpu.VMEM((tq, tk), F32)],
        input_output_aliases=aliases,
        compiler_params=_cparams(("parallel", "parallel", "arbitrary")),
    )(*args)


def _spatial_gate_kernel(u_ref, v_ref, w_ref, b_ref, o_ref, *, n_chunks):
    gw = u_ref.shape[1] // C_GROUPS
    row = lax.broadcasted_iota(jnp.int32, (C_BLOCK, C_BLOCK), 0)
    col = lax.broadcasted_iota(jnp.int32, (C_BLOCK, C_BLOCK), 1)
    for g in range(C_GROUPS):
        w = jnp.where(row >= col, w_ref[g], jnp.zeros((C_BLOCK, C_BLOCK), BF16))
        bias = b_ref[g]
        for c in range(n_chunks):
            rows = slice(c * C_BLOCK, (c + 1) * C_BLOCK)
            cols = slice(g * gw, (g + 1) * gw)
            vv = v_ref[rows, cols].astype(BF16)
            sg = jnp.dot(w, vv, preferred_element_type=F32) + bias
            o_ref[rows, cols] = (u_ref[rows, cols].astype(F32) * sg).astype(o_ref.dtype)


def _spatial_gate(u, v, w_sel, b_sel, n_prompt_rows):
    m, width = u.shape
    tm = _pick(m, (1024, 512, 256, 128))
    tm = math.gcd(tm, n_prompt_rows)
    npt = n_prompt_rows // tm
    sel = lambda i: jnp.where(i >= npt, 1, 0)
    kern = functools.partial(_spatial_gate_kernel, n_chunks=tm // C_BLOCK)
    return pl.pallas_call(
        kern, grid=(m // tm,),
        in_specs=[
            pl.BlockSpec((tm, width), lambda i: (i, 0)),
            pl.BlockSpec((tm, width), lambda i: (i, 0)),
            pl.BlockSpec((None, C_GROUPS, C_BLOCK, C_BLOCK), lambda i: (sel(i), 0, 0, 0)),
            pl.BlockSpec((None, C_GROUPS, C_BLOCK, 1), lambda i: (sel(i), 0, 0, 0)),
        ],
        out_specs=pl.BlockSpec((tm, width), lambda i: (i, 0)),
        out_shape=jax.ShapeDtypeStruct((m, width), BF16),
        compiler_params=_cparams(("parallel",)),
    )(u, v, w_sel, b_sel)


def _pool_kernel(x_ref, prev_ref, hist_ref, w_ref, sc_ref, o_ref, xp_sc, *, tm, past):
    i = pl.program_id(1)
    xp_sc[0:D_HALO, :] = jnp.where(i == 0, hist_ref[...], prev_ref[...])
    xp_sc[D_HALO:D_HALO + tm, :] = x_ref[...]
    gw = x_ref.shape[1] // len(D_WINDOWS)
    pos = past + i * tm + lax.broadcasted_iota(jnp.int32, (tm, 1), 0)
    for gi, win in enumerate(D_WINDOWS):
        cols = slice(gi * gw, (gi + 1) * gw)
        tot = xp_sc[D_HALO:D_HALO + tm, cols]
        for j in range(1, win):
            tot = tot + xp_sc[D_HALO - j:D_HALO - j + tm, cols]
        cnt = jnp.minimum(pos + 1, win).astype(F32)
        pooled = tot / cnt - xp_sc[D_HALO:D_HALO + tm, cols]
        y = jnp.dot(pooled.astype(BF16), w_ref[gi], preferred_element_type=F32)
        o_ref[:, cols] = (y * sc_ref[:, cols]).astype(o_ref.dtype)


def _pool(d_in, hist, w_grp, scale, *, nb, lq, row0, past, prev=None):
    width = d_in.shape[1]
    tm = _pick(lq, (512, 256, 128, 64))
    nt = lq // tm
    rb0 = row0 // tm
    hb0 = row0 // D_HALO
    per = tm // D_HALO
    kern, in_specs, args, aliases = _into_existing(
        functools.partial(_pool_kernel, tm=tm, past=past),
        [pl.BlockSpec((tm, width), lambda b, i: (rb0 + b * nt + i, 0)),
         pl.BlockSpec((D_HALO, width), lambda b, i: (jnp.maximum(hb0 + (b * nt + i) * per - 1, 0), 0)),
         pl.BlockSpec((None, D_HALO, width), lambda b, i: (b, 0, 0)),
         pl.BlockSpec(w_grp.shape, lambda b, i: (0, 0, 0)),
         pl.BlockSpec((1, width), lambda b, i: (0, 0))],
        [d_in, d_in, hist, w_grp, scale], prev)
    return pl.pallas_call(
        kern, grid=(nb, nt), in_specs=in_specs,
        out_specs=pl.BlockSpec((tm, width), lambda b, i: (rb0 + b * nt + i, 0)),
        out_shape=jax.ShapeDtypeStruct((d_in.shape[0], width), BF16),
        scratch_shapes=[pltpu.VMEM((D_HALO + tm, width), F32)], input_output_aliases=aliases,
        compiler_params=_cparams(("parallel", "arbitrary")),
    )(*args)


def _merge_kernel(x_ref, wg0, wg1, wg2, wg3, b0, b1, b2, b3, wb_ref, o_ref):
    x = x_ref[...]
    y = None
    for n, (wg, br) in enumerate(((wg0, b0), (wg1, b1), (wg2, b2), (wg3, b3))):
        gate = jax.nn.sigmoid(jnp.dot(x, wg[...], preferred_element_type=F32))
        t = gate * jnp.dot(br[...], wb_ref[n], preferred_element_type=F32)
        y = t if y is None else y + t
    o_ref[...] = y.astype(o_ref.dtype)


def _merge(xb, w_gate, branches, w_branch):
    m, d = xb.shape
    bw = branches[0].shape[1]
    tm = _pick(m, (512, 256, 128))
    tn = 512
    nj = d // tn
    in_specs = [pl.BlockSpec((tm, d), lambda i, j: (i, 0))]
    in_specs += [pl.BlockSpec((d, tn), lambda i, j, n=n: (0, n * nj + j)) for n in range(N_BRANCH)]
    in_specs += [pl.BlockSpec((tm, bw), lambda i, j: (i, 0)) for _ in range(N_BRANCH)]
    in_specs += [pl.BlockSpec((N_BRANCH, bw, tn), lambda i, j: (0, 0, j))]
    return pl.pallas_call(
        _merge_kernel, grid=(m // tm, nj), in_specs=in_specs,
        out_specs=pl.BlockSpec((tm, tn), lambda i, j: (i, j)),
        out_shape=jax.ShapeDtypeStruct((m, d), BF16),
        compiler_params=_cparams(("parallel", "arbitrary")),
    )(xb, w_gate, w_gate, w_gate, w_gate, *branches, w_branch)


def _out_ln_kernel(y_ref, x_ref, w_ref, g_ref, b_ref, o_ref, ob_ref, *, alpha):
    m = jnp.dot(y_ref[...], w_ref[...], preferred_element_type=F32)
    z = _ln_rows(alpha * x_ref[...] + m, g_ref[...], b_ref[...])
    o_ref[...] = z
    ob_ref[...] = z.astype(ob_ref.dtype)


def _out_ln(y, x, w_out, g, b, alpha):
    m, d = x.shape
    tm = _pick(m, (256, 128))
    row = lambda i: (i, 0)
    fixed = lambda i: (0, 0)
    return pl.pallas_call(
        functools.partial(_out_ln_kernel, alpha=alpha), grid=(m // tm,),
        in_specs=[pl.BlockSpec((tm, d), row), pl.BlockSpec((tm, d), row), pl.BlockSpec((d, d), fixed),
                  pl.BlockSpec((1, d), fixed), pl.BlockSpec((1, d), fixed)],
        out_specs=[pl.BlockSpec((tm, d), row), pl.BlockSpec((tm, d), row)],
        out_shape=[jax.ShapeDtypeStruct((m, d), F32), jax.ShapeDtypeStruct((m, d), BF16)],
        compiler_params=_cparams(("parallel",)),
    )(y, x, w_out, g, b)


def _ffn_kernel(xb_ref, x_ref, wg_ref, wu_ref, wd_ref, g_ref, b_ref, o_ref, ob_ref, acc_sc, *, alpha):
    f = pl.program_id(1)

    @pl.when(f == 0)
    def _():
        acc_sc[...] = jnp.zeros(acc_sc.shape, F32)

    xb = xb_ref[...]
    hg = jnp.dot(xb, wg_ref[...], preferred_element_type=F32)
    hu = jnp.dot(xb, wu_ref[...], preferred_element_type=F32)
    h = hg * jax.nn.sigmoid(hg) * hu
    acc_sc[...] += jnp.dot(h.astype(BF16), wd_ref[...], preferred_element_type=F32)

    @pl.when(f == pl.num_programs(1) - 1)
    def _():
        z = _ln_rows(alpha * x_ref[...] + acc_sc[...], g_ref[...], b_ref[...])
        o_ref[...] = z
        ob_ref[...] = z.astype(ob_ref.dtype)


def _ffn(xb, x, wg, wu, wd, g, b, alpha):
    m, d = x.shape
    ff = wg.shape[1]
    tm = _pick(m, (512, 256, 128))
    tf = _pick(ff, (512, 256, 128))
    row = lambda i, f: (i, 0)
    fixed = lambda i, f: (0, 0)
    return pl.pallas_call(
        functools.partial(_ffn_kernel, alpha=alpha), grid=(m // tm, ff // tf),
        in_specs=[pl.BlockSpec((tm, d), row), pl.BlockSpec((tm, d), row),
                  pl.BlockSpec((d, tf), lambda i, f: (0, f)), pl.BlockSpec((d, tf), lambda i, f: (0, f)),
                  pl.BlockSpec((tf, d), lambda i, f: (f, 0)),
                  pl.BlockSpec((1, d), fixed), pl.BlockSpec((1, d), fixed)],
        out_specs=[pl.BlockSpec((tm, d), row), pl.BlockSpec((tm, d), row)],
        out_shape=[jax.ShapeDtypeStruct((m, d), F32), jax.ShapeDtypeStruct((m, d), BF16)],
        scratch_shapes=[pltpu.VMEM((tm, d), F32)],
        compiler_params=_cparams(("parallel", "arbitrary")),
    )(xb, x, wg, wu, wd, g, b)


def _router_kernel(x_ref, w_ref, o_ref, sel_ref, *, n_experts):
    logits = jnp.dot(x_ref[...], w_ref[...], preferred_element_type=F32,
                     precision=lax.Precision.HIGHEST)
    lane = lax.broadcasted_iota(jnp.int32, logits.shape, 1).astype(F32)
    lg = jnp.where(lane < n_experts, logits, NEG)
    m1 = jnp.max(lg, axis=1, keepdims=True)
    i1 = jnp.min(jnp.where(lg == m1, lane, float(LANE)), axis=1, keepdims=True)
    lg2 = jnp.where(lane == i1, NEG, lg)
    m2 = jnp.max(lg2, axis=1, keepdims=True)
    i2 = jnp.min(jnp.where(lg2 == m2, lane, float(LANE)), axis=1, keepdims=True)
    ex = jnp.exp(m2 - m1)
    p1 = 1.0 / (1.0 + ex)
    p2 = ex / (1.0 + ex)
    o_ref[...] = (jnp.where(lane == 0.0, i1, 0.0) + jnp.where(lane == 1.0, i2, 0.0)
                  + jnp.where(lane == 2.0, p1, 0.0) + jnp.where(lane == 3.0, p2, 0.0))
    sel_ref[...] = jnp.where((lane == i1) | (lane == i2), 1.0, 0.0).astype(sel_ref.dtype)


def _router(x, w_router):
    m, d = x.shape
    ne = w_router.shape[1]
    w = jnp.pad(w_router, ((0, 0), (0, LANE - ne)))
    tm = _pick(m, (512, 256, 128))
    row = lambda i: (i, 0)
    return pl.pallas_call(
        functools.partial(_router_kernel, n_experts=ne), grid=(m // tm,),
        in_specs=[pl.BlockSpec((tm, d), row), pl.BlockSpec((d, LANE), lambda i: (0, 0))],
        out_specs=[pl.BlockSpec((tm, LANE), row), pl.BlockSpec((tm, LANE), row)],
        out_shape=[jax.ShapeDtypeStruct((m, LANE), F32), jax.ShapeDtypeStruct((m, LANE), BF16)],
        compiler_params=_cparams(("parallel",)),
    )(x, w)


def _slot_kernel(sel_ref, rec_ref, pos_ref, meta_ref, cnt_sc, off_sc, *, tile, n_experts):
    phase = pl.program_id(0)
    i = pl.program_id(1)
    sel = sel_ref[...]
    tm = sel.shape[0]
    lane = lax.broadcasted_iota(jnp.int32, (1, LANE), 1)

    @pl.when((phase == 0) & (i == 0))
    def _():
        cnt_sc[...] = jnp.zeros(cnt_sc.shape, F32)

    @pl.when(phase == 0)
    def _():
        cnt_sc[...] += jnp.sum(sel.astype(F32), axis=0, keepdims=True)

    @pl.when((phase == 1) & (i == 0))
    def _():
        counts = cnt_sc[...]
        padded = jnp.ceil(counts / tile) * tile
        starts = jnp.zeros((1, LANE), F32)
        for e in range(1, n_experts):
            before = jnp.sum(jnp.where(lane < e, padded, 0.0), axis=1, keepdims=True)
            starts = jnp.where(lane == e, before, starts)
        off_sc[...] = starts
        row = lax.broadcasted_iota(jnp.int32, meta_ref.shape, 0)
        meta_ref[...] = jnp.where(row == 0, counts, jnp.where(row == 1, starts, 0.0))
        cnt_sc[...] = jnp.zeros(cnt_sc.shape, F32)

    @pl.when(phase == 1)
    def _():
        r = lax.broadcasted_iota(jnp.int32, (tm, tm), 0)
        c = lax.broadcasted_iota(jnp.int32, (tm, tm), 1)
        tri = jnp.where(r >= c, 1.0, 0.0).astype(BF16)
        csum = jnp.dot(tri, sel, preferred_element_type=F32)
        slot = off_sc[...] + cnt_sc[...] + csum - sel.astype(F32)
        rec = rec_ref[...]
        lanef = lane.astype(F32)
        pos1 = jnp.sum(jnp.where(lanef == rec[:, 0:1], slot, 0.0), axis=1, keepdims=True)
        pos2 = jnp.sum(jnp.where(lanef == rec[:, 1:2], slot, 0.0), axis=1, keepdims=True)
        pos_ref[...] = jnp.where(lane == 0, pos1, 0.0) + jnp.where(lane == 1, pos2, 0.0)
        cnt_sc[...] += csum[tm - 1:tm, :]


def _slots(sel, rec, tile, n_experts):
    m = sel.shape[0]
    tm = _pick(m, (512, 256, 128))
    row = lambda p, i: (i, 0)
    return pl.pallas_call(
        functools.partial(_slot_kernel, tile=tile, n_experts=n_experts), grid=(2, m // tm),
        in_specs=[pl.BlockSpec((tm, LANE), row), pl.BlockSpec((tm, LANE), row)],
        out_specs=[pl.BlockSpec((tm, LANE), lambda p, i: (i * p, 0)),
                   pl.BlockSpec((8, LANE), lambda p, i: (0, 0))],
        out_shape=[jax.ShapeDtypeStruct((m, LANE), F32), jax.ShapeDtypeStruct((8, LANE), F32)],
        scratch_shapes=[pltpu.VMEM((1, LANE), F32), pltpu.VMEM((1, LANE), F32)],
        compiler_params=_cparams(("arbitrary", "arbitrary")),
    )(sel, rec)


def _row_copy(src, src_row, dst, dst_row, sem):
    return pltpu.make_async_copy(src.at[pl.ds(src_row, 1)], dst.at[pl.ds(dst_row, 1)], sem)


def _dispatch_kernel(pos1_ref, pos2_ref, pad_row_ref, pad_len_ref, x_ref, xs_hbm, zero_sc, sem, *, tm):
    step = pl.program_id(0)
    base = step * tm

    @pl.when(step == 0)
    def _():
        zero_sc[...] = jnp.zeros(zero_sc.shape, zero_sc.dtype)
        for e in range(pad_row_ref.shape[0]):
            def fill(r, carry, e=e):
                _row_copy(zero_sc, 0, xs_hbm, pad_row_ref[e] + r, sem).start()
                return carry

            def fill_wait(r, carry):
                _row_copy(zero_sc, 0, xs_hbm, 0, sem).wait()
                return carry

            lax.fori_loop(0, pad_len_ref[e], fill, 0)
            lax.fori_loop(0, pad_len_ref[e], fill_wait, 0)

    def issue(r, carry):
        t = base + r
        _row_copy(x_ref, r, xs_hbm, pos1_ref[t], sem).start()
        _row_copy(x_ref, r, xs_hbm, pos2_ref[t], sem).start()
        return carry

    lax.fori_loop(0, tm, issue, 0)

    def drain(r, carry):
        _row_copy(x_ref, 0, xs_hbm, 0, sem).wait()
        _row_copy(x_ref, 0, xs_hbm, 0, sem).wait()
        return carry

    lax.fori_loop(0, tm, drain, 0)


def _dispatch(x, pos1, pos2, pad_row, pad_len, n_rows):
    m, d = x.shape
    tm = _pick(m, (512, 256, 128))
    kern = functools.partial(_dispatch_kernel, tm=tm)
    return pl.pallas_call(
        kern,
        grid_spec=pltpu.PrefetchScalarGridSpec(
            num_scalar_prefetch=4, grid=(m // tm,),
            in_specs=[pl.BlockSpec((tm, d), lambda i, *_: (i, 0))],
            out_specs=pl.BlockSpec(memory_space=pl.ANY),
            scratch_shapes=[pltpu.VMEM((8, d), x.dtype), pltpu.SemaphoreType.DMA]),
        out_shape=jax.ShapeDtypeStruct((n_rows, d), x.dtype),
        compiler_params=_cparams(("arbitrary",)),
    )(pos1, pos2, pad_row, pad_len, x)


def _grouped_ffn_kernel(tile_e_ref, n_used_ref, x_ref, wg_ref, wu_ref, wd_ref, o_ref, acc_sc):
    del tile_e_ref
    i = pl.program_id(0)
    f = pl.program_id(1)
    used = i < n_used_ref[0]

    @pl.when(f == 0)
    def _():
        acc_sc[...] = jnp.zeros(acc_sc.shape, F32)

    @pl.when(used)
    def _():
        xb = x_ref[...].astype(BF16)
        hg = jnp.dot(xb, wg_ref[...], preferred_element_type=F32)
        hu = jnp.dot(xb, wu_ref[...], preferred_element_type=F32)
        h = hg * jax.nn.sigmoid(hg) * hu
        acc_sc[...] += jnp.dot(h.astype(BF16), wd_ref[...], preferred_element_type=F32)

    @pl.when(f == pl.num_programs(1) - 1)
    def _():
        o_ref[...] = acc_sc[...]


def _grouped_ffn(xs, tile_e, n_used, wg, wu, wd, tile):
    p, d = xs.shape
    ff = wg.shape[2]
    tf = _pick(ff, (1024, 512, 256, 128))
    row = lambda i, f, te, nu: (i, 0)
    return pl.pallas_call(
        _grouped_ffn_kernel,
        grid_spec=pltpu.PrefetchScalarGridSpec(
            num_scalar_prefetch=2, grid=(p // tile, ff // tf),
            in_specs=[pl.BlockSpec((tile, d), lambda i, f, te, nu: (jnp.minimum(i, nu[0] - 1), 0)),
                      pl.BlockSpec((None, d, tf), lambda i, f, te, nu: (te[i], 0, f)),
                      pl.BlockSpec((None, d, tf), lambda i, f, te, nu: (te[i], 0, f)),
                      pl.BlockSpec((None, tf, d), lambda i, f, te, nu: (te[i], f, 0))],
            out_specs=pl.BlockSpec((tile, d), row),
            scratch_shapes=[pltpu.VMEM((tile, d), F32)]),
        out_shape=jax.ShapeDtypeStruct((p, d), F32),
        compiler_params=_cparams(("parallel", "arbitrary")),
    )(tile_e, n_used, xs, wg, wu, wd)


def _combine_kernel(pos1_ref, pos2_ref, x_ref, rec_ref, g_ref, b_ref, ys_hbm, o_ref, o2_ref,
                    y1_sc, y2_sc, sem, *, alpha, tm, n_prompt_tiles):
    step = pl.program_id(0)
    base = step * tm

    def issue(r, carry):
        t = base + r
        _row_copy(ys_hbm, pos1_ref[t], y1_sc, r, sem).start()
        _row_copy(ys_hbm, pos2_ref[t], y2_sc, r, sem).start()
        return carry

    lax.fori_loop(0, tm, issue, 0)

    def drain(r, carry):
        _row_copy(ys_hbm, 0, y1_sc, 0, sem).wait()
        _row_copy(ys_hbm, 0, y2_sc, 0, sem).wait()
        return carry

    lax.fori_loop(0, tm, drain, 0)
    rec = rec_ref[...]
    f = rec[:, 2:3] * y1_sc[...] + rec[:, 3:4] * y2_sc[...]
    z = _ln_rows(alpha * x_ref[...] + f, g_ref[...], b_ref[...])
    if n_prompt_tiles is None:
        o_ref[...] = z
        o2_ref[...] = z.astype(o2_ref.dtype)
    else:
        @pl.when(step < n_prompt_tiles)
        def _():
            o_ref[...] = z

        @pl.when(step >= n_prompt_tiles)
        def _():
            o2_ref[...] = z


def _combine(x, rec, ys, pos1, pos2, g, b, alpha, final_prompt_rows=None):
    m, d = x.shape
    tm = _pick(m, (256, 128))
    row = lambda i, p1, p2: (i, 0)
    fixed = lambda i, p1, p2: (0, 0)
    if final_prompt_rows is None:
        n_p = None
        out_specs = [pl.BlockSpec((tm, d), row), pl.BlockSpec((tm, d), row)]
        out_shape = [jax.ShapeDtypeStruct((m, d), F32), jax.ShapeDtypeStruct((m, d), BF16)]
    else:
        tm = math.gcd(tm, final_prompt_rows)
        n_p = final_prompt_rows // tm
        out_specs = [pl.BlockSpec((tm, d), lambda i, p1, p2: (jnp.minimum(i, n_p - 1), 0)),
                     pl.BlockSpec((tm, d), lambda i, p1, p2: (jnp.maximum(i - n_p, 0), 0))]
        out_shape = [jax.ShapeDtypeStruct((final_prompt_rows, d), F32),
                     jax.ShapeDtypeStruct((m - final_prompt_rows, d), F32)]
    return pl.pallas_call(
        functools.partial(_combine_kernel, alpha=alpha, tm=tm, n_prompt_tiles=n_p),
        grid_spec=pltpu.PrefetchScalarGridSpec(
            num_scalar_prefetch=2, grid=(m // tm,),
            in_specs=[pl.BlockSpec((tm, d), row), pl.BlockSpec((tm, LANE), row),
                      pl.BlockSpec((1, d), fixed), pl.BlockSpec((1, d), fixed),
                      pl.BlockSpec(memory_space=pl.ANY)],
            out_specs=out_specs,
            scratch_shapes=[pltpu.VMEM((tm, d), F32), pltpu.VMEM((tm, d), F32),
                            pltpu.SemaphoreType.DMA]),
        out_shape=out_shape,
        compiler_params=_cparams(("arbitrary",)),
    )(pos1, pos2, x, rec, g, b, ys)


MOE_TILE = 512


def _routed_moe(x, w_router, wg, wu, wd, g, b, alpha, final_prompt_rows=None):
    m, d = x.shape
    ne = w_router.shape[1]
    rec, sel = _router(x, w_router)
    pos, meta = _slots(sel, rec, MOE_TILE, ne)
    pos1 = pos[:, 0].astype(jnp.int32)
    pos2 = pos[:, 1].astype(jnp.int32)
    n_tiles = (2 * m) // MOE_TILE + ne
    counts, starts = meta[0, :ne], meta[1, :ne]
    ends = starts + jnp.ceil(counts / MOE_TILE) * MOE_TILE
    tile_start = jnp.arange(n_tiles, dtype=F32) * MOE_TILE
    tile_e = jnp.minimum(jnp.sum(ends[None, :] <= tile_start[:, None], axis=1), ne - 1).astype(jnp.int32)
    n_used = (ends[ne - 1:ne] / MOE_TILE).astype(jnp.int32)
    last = ends[ne - 1:ne]
    pad_row = jnp.concatenate([starts + counts, last]).astype(jnp.int32)
    pad_len = jnp.concatenate([ends - starts - counts, n_tiles * MOE_TILE - last]).astype(jnp.int32)
    xs = _dispatch(x, pos1, pos2, pad_row, pad_len, n_tiles * MOE_TILE)
    ys = _grouped_ffn(xs, tile_e, n_used, wg, wu, wd, MOE_TILE)
    return _combine(x, rec, ys, pos1, pos2, g, b, alpha, final_prompt_rows)


def _epi_scale(scale):
    return lambda accs, tabs, vecs: [accs[0] * scale]


def _epi_f32_and_bf16(accs, tabs, vecs):
    return [accs[0], accs[0]]


def _epi_rms(accs, tabs, vecs):
    r = _rms_rows(accs[0], vecs[0])
    return [r, r]


def _epi_rope(accs, tabs, vecs):
    return [accs[0] * tabs[0] + accs[1] * tabs[1]]


def _epi_gelu(accs, tabs, vecs):
    return [jax.nn.gelu(accs[0])]


def _epi_gelu_ln(accs, tabs, vecs):
    return [_ln_rows(jax.nn.gelu(accs[0]), vecs[0], vecs[1])]


def _epi_plain(accs, tabs, vecs):
    return [accs[0]]


def _epi_q_rope(scale):
    return lambda accs, tabs, vecs: [(accs[0] * tabs[0] + accs[1] * tabs[1]) * scale]


def _rope_tables(pos):
    half = B_ROPE // 2
    inv = ROPE_THETA ** (-jnp.arange(half, dtype=F32) / half)
    ang = pos.astype(F32)[:, None] * inv[None, :]
    cos, sin = jnp.cos(ang), jnp.sin(ang)
    cos2 = jnp.concatenate([cos, cos], axis=1)
    sin2 = jnp.concatenate([-sin, sin], axis=1)
    return cos2, sin2


def _swap_halves(w):
    half = w.shape[-1] // 2
    return jnp.concatenate([w[..., half:], w[..., :half]], axis=-1)


def kernel(x_prompt, x_sample, cache_a_k, cache_a_v, cache_b_latent, cache_b_krope, state_d_pool, w_in, a_lambda, a_subln, b_q_norm, b_w_uq, b_kv_norm, b_w_ukv, c_ln_g, c_ln_b, c_w_s, c_b_s, d_w_grp, d_scale, w_branch, w_out, ln1_g, ln1_b, ln2_g, ln2_b, f_w_gate, f_w_up, f_w_down, m_router, m_w_gate, m_w_up, m_w_down):
    depth = w_in.shape[0]
    nbp, lp, d_model = x_prompt.shape
    nbs, ls, _ = x_sample.shape
    past = cache_a_k.shape[2]
    lks = past + ls
    tp, ts = nbp * lp, nbs * ls
    aw = A_HEADS * 2 * A_DH
    q_rank = b_q_norm.shape[1]
    kv_rank = b_kv_norm.shape[1]
    c_width = c_ln_g.shape[1]
    d_width = d_scale.shape[1]
    alpha = (2.0 * depth) ** 0.25
    sizes = (aw, aw, aw, q_rank, kv_rank, B_ROPE, c_width, c_width, d_width) + (d_model,) * N_BRANCH
    offs = [0]
    for s in sizes:
        offs.append(offs[-1] + s)

    x = jnp.concatenate([x_prompt.reshape(tp, d_model), x_sample.reshape(ts, d_model)], axis=0)
    xb = x.astype(BF16)

    pos = jnp.concatenate([jnp.tile(jnp.arange(lp, dtype=jnp.int32), nbp),
                           jnp.tile(past + jnp.arange(ls, dtype=jnp.int32), nbs)])
    cos2, sin2 = _rope_tables(pos)
    rows = pos.shape[0]
    ones = jnp.ones((rows, B_NOPE), F32)
    zeros_n = jnp.zeros((rows, B_NOPE), F32)
    zeros_r = jnp.zeros((rows, 2 * LANE - B_NOPE - B_ROPE), F32)
    q_cos = jnp.tile(jnp.concatenate([ones, cos2, zeros_r], axis=1), (1, 2))
    q_sin = jnp.tile(jnp.concatenate([zeros_n, sin2, zeros_r], axis=1), (1, 2))

    slopes = jnp.exp2(-8.0 * jnp.arange(1, A_HEADS + 1, dtype=F32) / A_HEADS) * LOG2E
    slopes = jnp.broadcast_to(slopes[:, None, None], (A_HEADS, 1, LANE))

    states_p, states_s = [], []
    kv_state = lat_state = None
    for l in range(depth):
        wl = w_in[l]
        seg = lambda k: wl[:, offs[k]:offs[k + 1]].astype(BF16)
        lam_init = 0.8 - 0.6 * math.exp(-0.3 * l)

        (q_a,) = _fused_mm(_epi_scale(A_DH ** -0.5 * LOG2E), xb, [seg(0)], [BF16], tn=1024)
        w_kv = jnp.concatenate([seg(1), seg(2)], axis=1)
        kv_a_b, kv_state = _state_proj(xb, w_kv, None, tn=aw, layer=l, depth=depth, n_prompt_rows=tp,
                                       prev=kv_state)
        subln = a_subln[l].reshape(1, 2 * A_DH)
        o_a = _diff_attn(q_a, kv_a_b, slopes, a_lambda[l], subln, nb=nbp, lq=lp, lk=lp, q_off=0,
                         q_row0=0, lam_init=lam_init, out_rows=rows, prev=jnp.zeros((rows, aw), BF16))
        cache_kv = jnp.concatenate([cache_a_k[l].reshape(nbs, past, aw),
                                    cache_a_v[l].reshape(nbs, past, aw)], axis=2).astype(BF16)
        kv_s = jnp.concatenate([cache_kv, kv_a_b[tp:].reshape(nbs, ls, 2 * aw)], axis=1)
        o_a = _diff_attn(q_a, kv_s.reshape(nbs * lks, 2 * aw), slopes, a_lambda[l], subln, nb=nbs,
                         lq=ls, lk=lks, q_off=past, q_row0=tp, lam_init=lam_init, out_rows=rows, prev=o_a)

        (cq,) = _fused_mm(_epi_rms, xb, [seg(3)], [BF16], tn=q_rank,
                          vecs=[b_q_norm[l].reshape(1, q_rank)])
        lat_b, lat_state = _state_proj(xb, seg(4), b_kv_norm[l].reshape(1, kv_rank), tn=kv_rank, layer=l,
                                       depth=depth, n_prompt_rows=tp, prev=lat_state)
        w_kr = wl[:, offs[5]:offs[6]]
        (kr,) = _fused_mm(_epi_rope, xb, [w_kr.astype(BF16), _swap_halves(w_kr).astype(BF16)], [F32],
                          tn=B_ROPE, tabs=[cos2, sin2])
        w_uq = b_w_uq[l].reshape(q_rank, B_HEADS, B_NOPE + B_ROPE)
        w_uq_n, w_uq_r = w_uq[..., :B_NOPE], w_uq[..., B_NOPE:]
        zpad = jnp.zeros((q_rank, B_HEADS, 2 * LANE - B_NOPE - B_ROPE), F32)
        w_q1 = jnp.concatenate([w_uq_n, w_uq_r, zpad], axis=-1).reshape(q_rank, -1).astype(BF16)
        w_q2 = jnp.concatenate([jnp.zeros_like(w_uq_n), _swap_halves(w_uq_r), zpad],
                               axis=-1).reshape(q_rank, -1).astype(BF16)
        (q_b,) = _fused_mm(_epi_q_rope((B_NOPE + B_ROPE) ** -0.5 * LOG2E), cq, [w_q1, w_q2], [BF16],
                           tn=4 * LANE, tabs=[q_cos, q_sin])
        w_ukv = b_w_ukv[l].reshape(kv_rank, B_HEADS, B_NOPE + B_VDIM)
        w_up = jnp.concatenate([w_ukv[..., :B_NOPE].reshape(kv_rank, -1),
                                w_ukv[..., B_NOPE:].reshape(kv_rank, -1)], axis=1).astype(BF16)
        lat_s = jnp.concatenate([cache_b_latent[l].astype(BF16), lat_b[tp:].reshape(nbs, ls, kv_rank)],
                                axis=1).reshape(nbs * lks, kv_rank)
        lat_cat = jnp.concatenate([lat_b[:tp], lat_s], axis=0)
        (kv_b,) = _fused_mm(_epi_plain, lat_cat, [w_up], [BF16], tn=1024)
        kr_s = jnp.concatenate([cache_b_krope[l], kr[tp:].reshape(nbs, ls, B_ROPE)],
                               axis=1).reshape(nbs * lks, B_ROPE)
        kr_cat = jnp.pad(jnp.concatenate([kr[:tp], kr_s], axis=0),
                         ((0, 0), (0, LANE - B_ROPE))).astype(BF16)
        o_b = _mla_attn(q_b, kv_b, kr_cat, nb=nbp, lq=lp, lk=lp, q_off=0, q_row0=0, kv_row0=0,
                        out_rows=rows, prev=jnp.zeros((rows, B_HEADS * B_VDIM), BF16))
        if tp % lks:
            kv_b_s, kr_b_s, s_row0 = kv_b[tp:], kr_cat[tp:], 0
        else:
            kv_b_s, kr_b_s, s_row0 = kv_b, kr_cat, tp
        o_b = _mla_attn(q_b, kv_b_s, kr_b_s, nb=nbs, lq=ls, lk=lks, q_off=past, q_row0=tp,
                        kv_row0=s_row0, out_rows=rows, prev=o_b)

        (u_c,) = _fused_mm(_epi_gelu, xb, [seg(6)], [BF16], tn=1024)
        (v_c,) = _fused_mm(_epi_gelu_ln, xb, [seg(7)], [F32], tn=c_width,
                           vecs=[c_ln_g[l].reshape(1, c_width), c_ln_b[l].reshape(1, c_width)])
        w_s, b_s = c_w_s[l], c_b_s[l]
        reps = C_BLOCK // ls if ls < C_BLOCK else 1
        w_tl = w_s[:, :C_BLOCK // reps, :C_BLOCK // reps]
        w_smp = jnp.zeros_like(w_s)
        for r in range(reps):
            sl = slice(r * (C_BLOCK // reps), (r + 1) * (C_BLOCK // reps))
            w_smp = w_smp.at[:, sl, sl].set(w_tl)
        b_smp = jnp.tile(b_s[:, :C_BLOCK // reps], (1, reps))
        w_sel = jnp.stack([w_s, w_smp]).astype(BF16)
        b_sel = jnp.stack([b_s, b_smp])[..., None]
        o_c = _spatial_gate(u_c, v_c, w_sel, b_sel, tp)

        (d_in,) = _fused_mm(_epi_plain, xb, [seg(8)], [F32], tn=1024)
        w_grp = d_w_grp[l].astype(BF16)
        scale = d_scale[l].reshape(1, d_width)
        hist_p = jnp.zeros((nbp, D_HALO, d_width), F32)
        hist_s = jnp.pad(state_d_pool[l], ((0, 0), (D_HALO - D_HIST, 0), (0, 0)))
        o_d = _pool(d_in, hist_p, w_grp, scale, nb=nbp, lq=lp, row0=0, past=0,
                    prev=jnp.zeros((rows, d_width), BF16))
        o_d = _pool(d_in, hist_s, w_grp, scale, nb=nbs, lq=ls, row0=tp, past=past, prev=o_d)

        w_gate = wl[:, offs[9]:offs[13]].astype(BF16)
        y = _merge(xb, w_gate, [o_a, o_b, o_c, o_d], w_branch[l].astype(BF16))
        x, xb = _out_ln(y, x, w_out[l].astype(BF16), ln1_g[l].reshape(1, -1), ln1_b[l].reshape(1, -1),
                        alpha)

        i = l // 2
        g2, b2 = ln2_g[l].reshape(1, -1), ln2_b[l].reshape(1, -1)
        if l % 2 == 0:
            x, xb = _ffn(xb, x, f_w_gate[i].astype(BF16), f_w_up[i].astype(BF16),
                         f_w_down[i].astype(BF16), g2, b2, alpha)
        else:
            x, xb = _routed_moe(x, m_router[i], m_w_gate[i].astype(BF16), m_w_up[i].astype(BF16),
                                m_w_down[i].astype(BF16), g2, b2, alpha,
                                final_prompt_rows=tp if l == depth - 1 else None)

        states_p.append((kr[:tp].reshape(nbp, lp, B_ROPE),
                         d_in[:tp].reshape(nbp, lp, d_width)[:, lp - D_HIST:]))
        states_s.append((kr[tp:].reshape(nbs, ls, B_ROPE), v_c[tp:].reshape(nbs, ls, c_width),
                         jnp.concatenate([state_d_pool[l], d_in[tp:].reshape(nbs, ls, d_width)],
                                         axis=1)[:, -D_HIST:]))

    if depth % 2 == 0:
        yp, ys = x.reshape(nbp, lp, d_model), xb.reshape(nbs, ls, d_model)
    else:
        yp, ys = x[:tp].reshape(nbp, lp, d_model), x[tp:].reshape(nbs, ls, d_model)
    stack = lambda sts, k: jnp.stack([s[k] for s in sts])
    k_p, k_s, v_p, v_s = kv_state
    lat_p, lat_s = lat_state
    heads_p = (depth, nbp, lp, A_HEADS, 2 * A_DH)
    heads_s = (depth, nbs, ls, A_HEADS, 2 * A_DH)
    return (yp, ys, k_p.reshape(heads_p), v_p.reshape(heads_p), lat_p.reshape(depth, nbp, lp, kv_rank),
            stack(states_p, 0), stack(states_p, 1), k_s.reshape(heads_s), v_s.reshape(heads_s),
            lat_s.reshape(depth, nbs, ls, kv_rank), stack(states_s, 0), stack(states_s, 1),
            stack(states_s, 2))
```
